```python
import math
import jax, jax.numpy as jnp
from jax import lax
import numpy as np

D_MODEL = 1024
BATCH = 8
SEQ = 4096
DEPTH = 4
DEC_BATCH = 8
DEC_SEQ = 8192
PAST_LEN = 128

GRID_W = 64
HEAD_DIM = 64
N_Q_HEADS = 8
N_KV_HEADS = 2
Q_PER_KV = N_Q_HEADS // N_KV_HEADS
ATTN_W = N_Q_HEADS * HEAD_DIM
KV_W = N_KV_HEADS * HEAD_DIM
HYENA_W = 512
HYENA_ORDER = 2
MIX_W = ATTN_W + HYENA_W
SHORT_CONV = 3
FILT_EMB = 33
FILT_BANDS = (FILT_EMB - 1) // 2
FILT_HID = 64
N_FILT = HYENA_ORDER * 2 * HYENA_W
MIN_DECAY = math.log(100.0) / 1.5
MAX_DECAY = math.log(100.0) / 0.3
MOD_SHIFT = 0.05
ROPE_THETA = 10000.0
ROPE_FREQS = HEAD_DIM // 4
Q_BLOCK = 128
EPS = 1e-6
SPLITS = (ATTN_W, ATTN_W + KV_W, ATTN_W + 2 * KV_W, 2 * ATTN_W + 2 * KV_W,
          2 * ATTN_W + 2 * KV_W + (HYENA_ORDER + 1) * HYENA_W)
D_IN_PROJ = SPLITS[-1] + HYENA_W

kernel_name = "hymba_gqa_axialrope_hyena_encoder"


def rms_norm(x, w):
    xf = x.astype(jnp.float32)
    y = xf * lax.rsqrt(jnp.mean(xf * xf, axis=-1, keepdims=True) + EPS)
    return (y * w.astype(jnp.float32)).astype(x.dtype)


def axial_rope_tables(L):
    rows = L // GRID_W
    r, c = jnp.meshgrid(jnp.arange(rows), jnp.arange(GRID_W), indexing="ij")
    pos = jnp.stack([r.reshape(-1), c.reshape(-1)], axis=-1).astype(jnp.float32)
    freqs = ROPE_THETA ** (-jnp.arange(ROPE_FREQS, dtype=jnp.float32) / ROPE_FREQS)
    ang = pos[:, :, None] * freqs
    return jnp.cos(ang), jnp.sin(ang)


def apply_axial_rope(x, cos, sin):
    B, L, H, _ = x.shape
    xr = x.reshape(B, L, H, 2, 2, ROPE_FREQS).astype(jnp.float32)
    x1, x2 = xr[..., 0, :], xr[..., 1, :]
    c, s = cos[None, :, None], sin[None, :, None]
    out = jnp.stack([x1 * c - x2 * s, x2 * c + x1 * s], axis=-2)
    return out.reshape(B, L, H, HEAD_DIM).astype(x.dtype)


def blocked_gqa_attention(q, k, v):
    B, L, _, d = q.shape
    nb = L // Q_BLOCK
    scale = 1.0 / math.sqrt(d)
    qb = q.reshape(B, nb, Q_BLOCK, N_KV_HEADS, Q_PER_KV, d).transpose(1, 0, 2, 3, 4, 5)

    def one_block(qi):
        s = jnp.einsum("bqhgd,bkhd->bhgqk", qi, k, preferred_element_type=jnp.float32) * scale
        p = jax.nn.softmax(s, axis=-1)
        return jnp.einsum("bhgqk,bkhd->bqhgd", p.astype(v.dtype), v)

    o = lax.map(one_block, qb)
    return o.transpose(1, 0, 2, 3, 4, 5).reshape(B, L, N_Q_HEADS * d)


def centred_short_conv(u, w, b):
    L = u.shape[1]
    up = jnp.pad(u, ((0, 0), (1, 1), (0, 0)))
    return up[:, :L] * w[0] + up[:, 1:L + 1] * w[1] + up[:, 2:] * w[2] + b


def hyena_filters_fft(L, w1, b1, w2, b2, w3, freq, decay):
    f32 = jnp.float32
    t = jnp.linspace(0.0, 1.0, L, dtype=f32)[:, None]
    w = 2.0 * math.pi * jnp.arange(L, dtype=f32) / L
    bands = jnp.linspace(1e-4, FILT_BANDS - 1, FILT_BANDS, dtype=f32)
    ang = w[:, None] * bands[None, :]
    z = jnp.concatenate([t, jnp.cos(ang), -jnp.sin(ang)], axis=-1)
    fr = freq.astype(f32)
    h = jnp.sin(fr * (z @ w1.astype(f32) + b1.astype(f32)))
    h = jnp.sin(fr * (h @ w2.astype(f32) + b2.astype(f32)))
    h = h @ w3.astype(f32)
    h = h * (jnp.exp(-t * jnp.abs(decay.astype(f32))) + MOD_SHIFT)
    h = h.reshape(L, HYENA_ORDER, 2, HYENA_W)
    h_fwd, h_bwd = h[:, :, 0], h[:, :, 1]
    buf = jnp.concatenate([h_fwd, jnp.zeros_like(h_fwd[:1]), h_bwd[1:][::-1]], axis=0)
    buf = buf / jnp.sum(jnp.abs(buf), axis=0, keepdims=True)
    return jnp.fft.rfft(buf, axis=0)


def long_conv(u, h_f, bias):
    L = u.shape[1]
    uf = u.astype(jnp.float32)
    y = jnp.fft.irfft(jnp.fft.rfft(uf, n=2 * L, axis=1) * h_f[None], n=2 * L, axis=1)[:, :L]
    return (y + uf * bias.astype(jnp.float32)).astype(u.dtype)


def run_trunk(x, norm_w, w_in, q_norm_w, k_norm_w, conv_w, conv_b, filt_w1, filt_b1,
              filt_w2, filt_b2, filt_w3, filt_freq, filt_decay, hyena_bias,
              attn_out_norm_w, hyena_out_norm_w, w_out, final_norm_w):
    B, L, _ = x.shape
    cos, sin = axial_rope_tables(L)
    for l in range(DEPTH):
        h = rms_norm(x, norm_w[l])
        proj = h @ w_in[l]
        q, k, v, g_a, u_h, g_h = jnp.split(proj, SPLITS, axis=-1)
        q = apply_axial_rope(rms_norm(q.reshape(B, L, N_Q_HEADS, HEAD_DIM), q_norm_w[l]), cos, sin)
        k = apply_axial_rope(rms_norm(k.reshape(B, L, N_KV_HEADS, HEAD_DIM), k_norm_w[l]), cos, sin)
        v = v.reshape(B, L, N_KV_HEADS, HEAD_DIM)
        attn = blocked_gqa_attention(q, k, v)
        o_a = rms_norm(attn, attn_out_norm_w[l]) * jax.nn.silu(g_a)
        u = centred_short_conv(u_h, conv_w[l], conv_b[l])
        v_h, x1, x2 = jnp.split(u, HYENA_ORDER + 1, axis=-1)
        hf = hyena_filters_fft(L, filt_w1[l], filt_b1[l], filt_w2[l], filt_b2[l],
                               filt_w3[l], filt_freq[l], filt_decay[l])
        zz = x1 * long_conv(v_h, hf[:, 0], hyena_bias[l, 0])
        zz = x2 * long_conv(zz, hf[:, 1], hyena_bias[l, 1])
        o_h = rms_norm(zz, hyena_out_norm_w[l]) * jax.nn.silu(g_h)
        x = x + jnp.concatenate([o_a, o_h], axis=-1) @ w_out[l]
    return rms_norm(x, final_norm_w)


def setup_inputs(seed: int = 0) -> dict:
    key = jax.random.key(seed)
    ks = jax.random.split(key, 24)
    n = jax.random.normal
    f32 = jnp.float32
    return {
        "x_prompt": n(ks[0], (BATCH, SEQ, D_MODEL), f32),
        "x_sample": n(ks[1], (DEC_BATCH, DEC_SEQ, D_MODEL), f32),
        "norm_w": 1.0 + 0.02 * n(ks[2], (DEPTH, D_MODEL), f32),
        "w_in": n(ks[3], (DEPTH, D_MODEL, D_IN_PROJ), f32) * D_MODEL ** -0.5,
        "q_norm_w": 1.0 + 0.02 * n(ks[4], (DEPTH, HEAD_DIM), f32),
        "k_norm_w": 1.0 + 0.02 * n(ks[5], (DEPTH, HEAD_DIM), f32),
        "conv_w": n(ks[6], (DEPTH, SHORT_CONV, (HYENA_ORDER + 1) * HYENA_W), f32) * SHORT_CONV ** -0.5,
        "conv_b": 0.01 * n(ks[7], (DEPTH, (HYENA_ORDER + 1) * HYENA_W), f32),
        "filt_w1": n(ks[8], (DEPTH, FILT_EMB, FILT_HID), f32) * FILT_EMB ** -0.5,
        "filt_b1": 0.01 * n(ks[9], (DEPTH, FILT_HID), f32),
        "filt_w2": n(ks[10], (DEPTH, FILT_HID, FILT_HID), f32) * FILT_HID ** -0.5,
        "filt_b2": 0.01 * n(ks[11], (DEPTH, FILT_HID), f32),
        "filt_w3": n(ks[12], (DEPTH, FILT_HID, N_FILT), f32) * FILT_HID ** -0.5,
        "filt_freq": 1.0 + 0.1 * n(ks[13], (DEPTH, FILT_HID), f32),
        "filt_decay": jax.random.uniform(ks[14], (DEPTH, N_FILT), f32, MIN_DECAY, MAX_DECAY),
        "hyena_bias": 0.1 * n(ks[15], (DEPTH, HYENA_ORDER, HYENA_W), f32),
        "attn_out_norm_w": 1.0 + 0.02 * n(ks[16], (DEPTH, ATTN_W), f32),
        "hyena_out_norm_w": 1.0 + 0.02 * n(ks[17], (DEPTH, HYENA_W), f32),
        "w_out": n(ks[18], (DEPTH, MIX_W, D_MODEL), f32) * MIX_W ** -0.5,
        "final_norm_w": 1.0 + 0.02 * n(ks[19], (D_MODEL,), f32),
    }


def reference(x_prompt, x_sample, norm_w, w_in, q_norm_w, k_norm_w, conv_w, conv_b,
              filt_w1, filt_b1, filt_w2, filt_b2, filt_w3, filt_freq, filt_decay,
              hyena_bias, attn_out_norm_w, hyena_out_norm_w, w_out, final_norm_w):
    y_prompt = run_trunk(x_prompt, norm_w, w_in, q_norm_w, k_norm_w, conv_w, conv_b,
                         filt_w1, filt_b1, filt_w2, filt_b2, filt_w3, filt_freq, filt_decay,
                         hyena_bias, attn_out_norm_w, hyena_out_norm_w, w_out, final_norm_w)
    y_sample = run_trunk(x_sample, norm_w, w_in, q_norm_w, k_norm_w, conv_w, conv_b,
                         filt_w1, filt_b1, filt_w2, filt_b2, filt_w3, filt_freq, filt_decay,
                         hyena_bias, attn_out_norm_w, hyena_out_norm_w, w_out, final_norm_w)
    return (y_prompt, y_sample)
```

```python
import functools
import math

import jax
import jax.numpy as jnp
from jax import lax
from jax.experimental import pallas as pl
from jax.experimental.pallas import tpu as pltpu

F32 = jnp.float32
BF16 = jnp.bfloat16

D_MODEL = 1024
DEPTH = 4
GRID_W = 64
HEAD_DIM = 64
N_Q_HEADS = 8
N_KV_HEADS = 2
Q_PER_KV = N_Q_HEADS // N_KV_HEADS
ATTN_W = N_Q_HEADS * HEAD_DIM
KV_W = N_KV_HEADS * HEAD_DIM
HYENA_W = 512
HYENA_ORDER = 2
MIX_W = ATTN_W + HYENA_W
FILT_EMB = 33
FILT_BANDS = 16
FILT_HID = 64
N_FILT = HYENA_ORDER * 2 * HYENA_W
MOD_SHIFT = 0.05
ROPE_THETA = 10000.0
ROPE_FREQS = HEAD_DIM // 4
EPS = 1e-6
COL_K = ATTN_W
COL_V = COL_K + KV_W
COL_GA = COL_V + KV_W
COL_U = COL_GA + ATTN_W
COL_GH = COL_U + (HYENA_ORDER + 1) * HYENA_W
D_IN_PROJ = COL_GH + HYENA_W

LANES = 128
FFT_N2 = 128
VMEM_LIMIT_BYTES = 48 * 1024 * 1024
HIGHEST = lax.Precision.HIGHEST


def _cparams(*sem):
    return pltpu.CompilerParams(dimension_semantics=sem, vmem_limit_bytes=VMEM_LIMIT_BYTES)


def _dot(a, b):
    return jnp.dot(a, b, preferred_element_type=F32)


def _dot_f32(a, b):
    return jnp.dot(a, b, preferred_element_type=F32, precision=HIGHEST)


def _inproj_body(x_ref, nw_ref, w_ref, qnw_ref, knw_ref, p_ref, c_ref, s_ref,
                 q_ref, k_ref, v_ref, ga_ref, u_ref, gh_ref):
    x = x_ref[...]
    ms = jnp.mean(x * x, axis=-1, keepdims=True)
    h = (x * lax.rsqrt(ms + EPS) * nw_ref[...]).astype(BF16)
    p = p_ref[...]
    c = c_ref[...]
    s = s_ref[...]
    lane = lax.broadcasted_iota(jnp.int32, c.shape, 1)
    first_half = (lane % (2 * ROPE_FREQS)) < ROPE_FREQS

    def norm_rope(y, nw):
        y2 = y * y
        hi = y2.astype(BF16)
        lo = (y2 - hi.astype(F32)).astype(BF16)
        msq = _dot(hi, p) + _dot(lo, p)
        yn = y * lax.rsqrt(msq + EPS) * nw
        partner = jnp.where(first_half, pltpu.roll(yn, LANES - ROPE_FREQS, 1),
                            pltpu.roll(yn, ROPE_FREQS, 1))
        return yn * c + partner * s

    def store_heads(ref, j, r):
        ref[0, 2 * j] = r[:, :HEAD_DIM].astype(ref.dtype)
        ref[0, 2 * j + 1] = pltpu.roll(r, HEAD_DIM, 1)[:, :HEAD_DIM].astype(ref.dtype)

    q = _dot(h, w_ref[:, 0:COL_K])
    qnw = qnw_ref[...]
    for j in range(ATTN_W // LANES):
        store_heads(q_ref, j, norm_rope(q[:, j * LANES:(j + 1) * LANES], qnw) * (HEAD_DIM ** -0.5))
    kv = _dot(h, w_ref[:, COL_K:COL_GA])
    store_heads(k_ref, 0, norm_rope(kv[:, :KV_W], knw_ref[...]))
    store_heads(v_ref, 0, kv[:, KV_W:])
    ga_ref[...] = _dot(h, w_ref[:, COL_GA:COL_U])
    u_ref[...] = _dot(h, w_ref[:, COL_U:COL_GH])
    gh_ref[...] = _dot(h, w_ref[:, COL_GH:D_IN_PROJ])


def _inproj(x2d, nw, w_bf, qnw, knw, pmat, ctab, stab, B, L, tm=512):
    T = B * L
    nlb = L // tm
    row = lambda i: (i, 0)
    const = lambda i: (0, 0)
    pos = lambda i: (i % nlb, 0)
    heads = lambda i: (i // nlb, 0, i % nlb, 0)
    return pl.pallas_call(
        _inproj_body,
        grid=(T // tm,),
        in_specs=[
            pl.BlockSpec((tm, D_MODEL), row),
            pl.BlockSpec((1, D_MODEL), const),
            pl.BlockSpec((D_MODEL, D_IN_PROJ), const),
            pl.BlockSpec((1, LANES), const),
            pl.BlockSpec((1, LANES), const),
            pl.BlockSpec((LANES, LANES), const),
            pl.BlockSpec((tm, LANES), pos),
            pl.BlockSpec((tm, LANES), pos),
        ],
        out_specs=[
            pl.BlockSpec((1, N_Q_HEADS, tm, HEAD_DIM), heads),
            pl.BlockSpec((1, N_KV_HEADS, tm, HEAD_DIM), heads),
            pl.BlockSpec((1, N_KV_HEADS, tm, HEAD_DIM), heads),
            pl.BlockSpec((tm, ATTN_W), row),
            pl.BlockSpec((tm, (HYENA_ORDER + 1) * HYENA_W), row),
            pl.BlockSpec((tm, HYENA_W), row),
        ],
        out_shape=[
            jax.ShapeDtypeStruct((B, N_Q_HEADS, L, HEAD_DIM), BF16),
            jax.ShapeDtypeStruct((B, N_KV_HEADS, L, HEAD_DIM), BF16),
            jax.ShapeDtypeStruct((B, N_KV_HEADS, L, HEAD_DIM), BF16),
            jax.ShapeDtypeStruct((T, ATTN_W), F32),
            jax.ShapeDtypeStruct((T, (HYENA_ORDER + 1) * HYENA_W), F32),
            jax.ShapeDtypeStruct((T, HYENA_W), F32),
        ],
        compiler_params=_cparams("parallel"),
        name="inproj",
    )(x2d, nw, w_bf, qnw, knw, pmat, ctab, stab)


def _attn_body(q_ref, k_ref, v_ref, o_ref, *, tq, tk, nk):
    rows = Q_PER_KV * tq
    q = q_ref[0].reshape(rows, HEAD_DIM)

    def step(j, carry):
        m, l, acc = carry
        start = pl.multiple_of(j * tk, tk)
        k = k_ref[0, 0, pl.ds(start, tk), :]
        v = v_ref[0, 0, pl.ds(start, tk), :]
        s = lax.dot_general(q, k, (((1,), (1,)), ((), ())), preferred_element_type=F32)
        m_new = jnp.maximum(m, jnp.max(s, axis=-1, keepdims=True))
        p = jnp.exp(s - m_new)
        alpha = jnp.exp(m - m_new)
        l = alpha * l + jnp.sum(p, axis=-1, keepdims=True)
        acc = alpha * acc + _dot(p.astype(BF16), v)
        return m_new, l, acc

    m0 = jnp.full((rows, 1), -jnp.inf, F32)
    l0 = jnp.zeros((rows, 1), F32)
    a0 = jnp.zeros((rows, HEAD_DIM), F32)
    _, l, acc = lax.fori_loop(0, nk, step, (m0, l0, a0))
    o = acc / l
    for h in range(Q_PER_KV):
        o_ref[0, :, h * HEAD_DIM:(h + 1) * HEAD_DIM] = o[h * tq:(h + 1) * tq]


def _attention(q, k, v, B, L, tq=128, tk=512):
    body = functools.partial(_attn_body, tq=tq, tk=tk, nk=L // tk)
    return pl.pallas_call(
        body,
        grid=(B, N_KV_HEADS, L // tq),
        in_specs=[
            pl.BlockSpec((1, Q_PER_KV, tq, HEAD_DIM), lambda b, g, i: (b, g, i, 0)),
            pl.BlockSpec((1, 1, L, HEAD_DIM), lambda b, g, i: (b, g, 0, 0)),
            pl.BlockSpec((1, 1, L, HEAD_DIM), lambda b, g, i: (b, g, 0, 0)),
        ],
        out_specs=pl.BlockSpec((1, tq, Q_PER_KV * HEAD_DIM), lambda b, g, i: (b, i, g)),
        out_shape=jax.ShapeDtypeStruct((B, L, ATTN_W), F32),
        compiler_params=_cparams("parallel", "parallel", "parallel"),
        name="attention",
    )(q, k, v)


def _sconv_body(u_ref, prev_ref, next_ref, w_ref, b_ref, o_ref, *, nblk):
    i = pl.program_id(2)
    x = u_ref[0]
    R = x.shape[0]
    halo = prev_ref.shape[1]
    prev_row = jnp.where(i == 0, 0.0, prev_ref[0, halo - 1:halo, :])
    next_row = jnp.where(i == nblk - 1, 0.0, next_ref[0, 0:1, :])
    row = lax.broadcasted_iota(jnp.int32, x.shape, 0)
    xm = jnp.where(row == 0, prev_row, pltpu.roll(x, 1, 0))
    xp = jnp.where(row == R - 1, next_row, pltpu.roll(x, R - 1, 0))
    w = w_ref[...]
    o_ref[0] = xm * w[0:1] + x * w[1:2] + xp * w[2:3] + b_ref[...]


def _short_conv(u3, cw, cb, B, L, R=1024, halo=8):
    C = u3.shape[-1]
    nblk = L // R
    rpb = R // halo
    cpp = HYENA_W // LANES
    body = functools.partial(_sconv_body, nblk=nblk)
    return pl.pallas_call(
        body,
        grid=(B, C // LANES, nblk),
        in_specs=[
            pl.BlockSpec((1, R, LANES), lambda b, j, i: (b, i, j)),
            pl.BlockSpec((1, halo, LANES), lambda b, j, i: (b, jnp.maximum(i * rpb - 1, 0), j)),
            pl.BlockSpec((1, halo, LANES), lambda b, j, i: (b, jnp.minimum((i + 1) * rpb, L // halo - 1), j)),
            pl.BlockSpec((3, LANES), lambda b, j, i: (0, j)),
            pl.BlockSpec((1, LANES), lambda b, j, i: (0, j)),
        ],
        out_specs=pl.BlockSpec((None, 1, R, LANES), lambda b, j, i: (j // cpp, b, i, j % cpp)),
        out_shape=jax.ShapeDtypeStruct((C // HYENA_W, B, L, HYENA_W), F32),
        compiler_params=_cparams("parallel", "parallel", "parallel"),
        name="short_conv",
    )(u3, u3, u3, cw, cb)


def _filt_body(z_ref, valid_ref, w1_ref, b1_ref, w2_ref, b2_ref, fr_ref, w3_ref, dec_ref,
               buf_ref, sum_ref):
    i = pl.program_id(0)
    z = z_ref[...]
    fr = fr_ref[...]
    h = jnp.sin(fr * (_dot_f32(z, w1_ref[...]) + b1_ref[...]))
    h = jnp.sin(fr * (_dot_f32(h, w2_ref[...]) + b2_ref[...]))
    h = _dot_f32(h, w3_ref[0])
    t = z[:, 0:1]
    win = jnp.exp(-t * jnp.abs(dec_ref[0])) + MOD_SHIFT
    out = h * win * valid_ref[...]
    buf_ref[...] = out

    @pl.when(i == 0)
    def _():
        sum_ref[...] = jnp.zeros_like(sum_ref)

    sum_ref[...] += jnp.sum(jnp.abs(out), axis=0, keepdims=True)


def _filters(ztab, valid, w1, b1, w2, b2, fr, w3d, decd, L, rb=512):
    n = 2 * L
    nb = n // rb
    half = nb // 2
    W = HYENA_ORDER * HYENA_W
    const = lambda i: (0, 0)
    return pl.pallas_call(
        _filt_body,
        grid=(nb,),
        in_specs=[
            pl.BlockSpec((rb, LANES), lambda i: (i, 0)),
            pl.BlockSpec((rb, 1), lambda i: (i, 0)),
            pl.BlockSpec((LANES, LANES), const),
            pl.BlockSpec((1, LANES), const),
            pl.BlockSpec((LANES, LANES), const),
            pl.BlockSpec((1, LANES), const),
            pl.BlockSpec((1, LANES), const),
            pl.BlockSpec((1, LANES, W), lambda i: (i // half, 0, 0)),
            pl.BlockSpec((1, 1, W), lambda i: (i // half, 0, 0)),
        ],
        out_specs=[
            pl.BlockSpec((rb, W), lambda i: (i, 0)),
            pl.BlockSpec((1, W), const),
        ],
        out_shape=[
            jax.ShapeDtypeStruct((n, W), F32),
            jax.ShapeDtypeStruct((1, W), F32),
        ],
        compiler_params=_cparams("arbitrary"),
        name="filters",
    )(ztab, valid, w1, b1, w2, b2, fr, w3d, decd)


def _s1_body(a_ref, w_ref, o_ref):
    a = a_ref[...]
    z = a.reshape(a.shape[0] * a.shape[1], a.shape[2]).astype(BF16)
    o_ref[0] = _dot(w_ref[...], z).astype(o_ref.dtype)


def _s1(src, which, w1m, P, N1, cb=2048):
    N1h = N1 // 2
    cols = src.shape[-1]
    return pl.pallas_call(
        _s1_body,
        grid=(P, cols // cb),
        in_specs=[
            pl.BlockSpec((None, 2, None, N1h, cb), lambda p, j: (which, 0, p, 0, j)),
            pl.BlockSpec((2 * N1, N1), lambda p, j: (0, 0)),
        ],
        out_specs=pl.BlockSpec((1, 2 * N1, cb), lambda p, j: (p, 0, j)),
        out_shape=jax.ShapeDtypeStruct((P, 2 * N1, cols), BF16),
        compiler_params=_cparams("parallel", "parallel"),
        name="dft_n1",
    )(src, w1m)


def _s1f_body(b_ref, s_ref, w_ref, o_ref):
    z = (b_ref[...] / s_ref[...]).astype(BF16)
    o_ref[...] = _dot(w_ref[...], z).astype(o_ref.dtype)


def _s1_filter(buf2d, ssum_t, w1f, N1, cb=2048):
    cols = buf2d.shape[-1]
    return pl.pallas_call(
        _s1f_body,
        grid=(cols // cb,),
        in_specs=[
            pl.BlockSpec((N1, cb), lambda j: (0, j)),
            pl.BlockSpec((1, cb), lambda j: (0, j)),
            pl.BlockSpec((2 * N1, N1), lambda j: (0, 0)),
        ],
        out_specs=pl.BlockSpec((2 * N1, cb), lambda j: (0, j)),
        out_shape=jax.ShapeDtypeStruct((2 * N1, cols), BF16),
        compiler_params=_cparams("parallel"),
        name="dft_n1_filter",
    )(buf2d, ssum_t, w1f)


def _midf_body(x_ref, g_ref, h_ref, *, kb):
    for k in range(kb):
        h_ref[k] = _dot(g_ref[k], x_ref[k])


def _mid_filter(hpre, g, N1, kb=4):
    W = hpre.shape[-1]
    body = functools.partial(_midf_body, kb=kb)
    return pl.pallas_call(
        body,
        grid=(N1 // kb,),
        in_specs=[
            pl.BlockSpec((kb, 2 * FFT_N2, W), lambda i: (i, 0, 0)),
            pl.BlockSpec((kb, 2 * FFT_N2, 2 * FFT_N2), lambda i: (i, 0, 0)),
        ],
        out_specs=pl.BlockSpec((kb, 2 * FFT_N2, W), lambda i: (i, 0, 0)),
        out_shape=jax.ShapeDtypeStruct((N1, 2 * FFT_N2, W), F32),
        compiler_params=_cparams("parallel"),
        name="dft_n2_filter",
    )(hpre, g)


def _mid_body(x_ref, g_ref, gi_ref, h_ref, o_ref, *, kb):
    for k in range(kb):
        d = _dot(g_ref[k], x_ref[0, k])
        hh = h_ref[k]
        dr, di = d[:FFT_N2], d[FFT_N2:]
        hr, hi = hh[:FFT_N2], hh[FFT_N2:]
        y = jnp.concatenate([dr * hr - di * hi, dr * hi + di * hr], axis=0).astype(BF16)
        o_ref[0, k] = _dot(gi_ref[k], y).astype(o_ref.dtype)


def _mid(x4, g, gi, hspec, order, P, N1, kb=4):
    C = HYENA_W
    body = functools.partial(_mid_body, kb=kb)
    return pl.pallas_call(
        body,
        grid=(N1 // kb, P),
        in_specs=[
            pl.BlockSpec((1, kb, 2 * FFT_N2, C), lambda i, p: (p, i, 0, 0)),
            pl.BlockSpec((kb, 2 * FFT_N2, 2 * FFT_N2), lambda i, p: (i, 0, 0)),
            pl.BlockSpec((kb, 2 * FFT_N2, 2 * FFT_N2), lambda i, p: (i, 0, 0)),
            pl.BlockSpec((kb, 2 * FFT_N2, C), lambda i, p: (i, 0, order)),
        ],
        out_specs=pl.BlockSpec((1, kb, 2 * FFT_N2, C), lambda i, p: (p, i, 0, 0)),
        out_shape=jax.ShapeDtypeStruct((P, N1, 2 * FFT_N2, C), BF16),
        compiler_params=_cparams("parallel", "parallel"),
        name="dft_n2_conv",
    )(x4, g, gi, hspec)


def _s1inv_body(e_ref, w_ref, src_ref, mul_ref, bias_ref, o_ref):
    y = _dot(w_ref[...], e_ref[0])
    src = src_ref[...]
    y = y.reshape(src.shape)
    o_ref[...] = mul_ref[...] * (y + src * bias_ref[...])


def _s1inv(e3, w1i, src, src_which, mul, mul_which, bias_t, P, N1, cb=2048):
    N1h = N1 // 2
    cols = e3.shape[-1]
    member = lambda which: pl.BlockSpec((None, 2, None, N1h, cb), lambda p, j: (which, 0, p, 0, j))
    return pl.pallas_call(
        _s1inv_body,
        grid=(P, cols // cb),
        in_specs=[
            pl.BlockSpec((1, 2 * N1, cb), lambda p, j: (p, 0, j)),
            pl.BlockSpec((N1, 2 * N1), lambda p, j: (0, 0)),
            member(src_which),
            member(mul_which),
            pl.BlockSpec((1, cb), lambda p, j: (0, j)),
        ],
        out_specs=pl.BlockSpec((None, 2, None, N1h, cb), lambda p, j: (0, 0, p, 0, j)),
        out_shape=jax.ShapeDtypeStruct((1, 2, P, N1h, cols), F32),
        compiler_params=_cparams("parallel", "parallel"),
        name="idft_n1",
    )(e3, w1i, src, mul, bias_t)


def _outproj_body(a_ref, ga_ref, z_ref, gh_ref, x_ref, wa_ref, wh_ref, w_ref, fw_ref, o_ref, *, final):
    def norm_gate(y, w, g):
        ms = jnp.mean(y * y, axis=-1, keepdims=True)
        return (y * lax.rsqrt(ms + EPS) * w) * (g * (1.0 / (1.0 + jnp.exp(-g))))

    oa = norm_gate(a_ref[...], wa_ref[...], ga_ref[...]).astype(BF16)
    oh = norm_gate(z_ref[...], wh_ref[...], gh_ref[...]).astype(BF16)
    y = x_ref[...] + (_dot(oa, w_ref[0:ATTN_W, :]) + _dot(oh, w_ref[ATTN_W:MIX_W, :]))
    if final:
        ms = jnp.mean(y * y, axis=-1, keepdims=True)
        y = y * lax.rsqrt(ms + EPS) * fw_ref[...]
    o_ref[...] = y


def _outproj(attn2d, ga, zz2d, gh, x2d, wa, wh, w_bf, fw, final, tm=512):
    T = x2d.shape[0]
    row = lambda i: (i, 0)
    const = lambda i: (0, 0)
    body = functools.partial(_outproj_body, final=final)
    return pl.pallas_call(
        body,
        grid=(T // tm,),
        in_specs=[
            pl.BlockSpec((tm, ATTN_W), row),
            pl.BlockSpec((tm, ATTN_W), row),
            pl.BlockSpec((tm, HYENA_W), row),
            pl.BlockSpec((tm, HYENA_W), row),
            pl.BlockSpec((tm, D_MODEL), row),
            pl.BlockSpec((1, ATTN_W), const),
            pl.BlockSpec((1, HYENA_W), const),
            pl.BlockSpec((MIX_W, D_MODEL), const),
            pl.BlockSpec((1, D_MODEL), const),
        ],
        out_specs=pl.BlockSpec((tm, D_MODEL), row),
        out_shape=jax.ShapeDtypeStruct((T, D_MODEL), F32),
        compiler_params=_cparams("parallel"),
        name="outproj",
    )(attn2d, ga, zz2d, gh, x2d, wa, wh, w_bf, fw)


def _rope_tables(L):
    t = jnp.arange(L, dtype=jnp.int32)
    pos = jnp.stack([t // GRID_W, t % GRID_W], axis=-1).astype(F32)
    freqs = ROPE_THETA ** (-jnp.arange(ROPE_FREQS, dtype=F32) / ROPE_FREQS)
    ang = pos[:, :, None] * freqs
    cos, sin = jnp.cos(ang), jnp.sin(ang)
    c_head = jnp.stack([cos, cos], axis=2).reshape(L, HEAD_DIM)
    s_head = jnp.stack([-sin, sin], axis=2).reshape(L, HEAD_DIM)
    reps = LANES // HEAD_DIM
    return jnp.tile(c_head, (1, reps)), jnp.tile(s_head, (1, reps))


def _head_mean_matrix():
    idx = jnp.arange(LANES) // HEAD_DIM
    return jnp.where(idx[:, None] == idx[None, :], 1.0 / HEAD_DIM, 0.0).astype(BF16)


def _dft_tables(L):
    N = 2 * L
    N1 = N // FFT_N2
    N1h = N1 // 2
    two_pi = 2.0 * math.pi
    k1 = jnp.arange(N1, dtype=jnp.int32)
    th = ((k1[:, None] * k1[None, :]) % N1).astype(F32) * (two_pi / N1)
    c, s = jnp.cos(th), jnp.sin(th)
    ch, sh = c[:, :N1h], s[:, :N1h]
    w1 = jnp.stack([jnp.concatenate([ch, sh], axis=1),
                    jnp.concatenate([-sh, ch], axis=1)], axis=1).reshape(2 * N1, N1)
    w1f = jnp.stack([c, -s], axis=1).reshape(2 * N1, N1)
    ct, st = ch.T, sh.T
    top = jnp.stack([ct, -st], axis=2).reshape(N1h, 2 * N1)
    bot = jnp.stack([st, ct], axis=2).reshape(N1h, 2 * N1)
    w1i = jnp.concatenate([top, bot], axis=0) * (1.0 / N)
    k2 = jnp.arange(FFT_N2, dtype=jnp.int32)
    freq = k1[:, None, None] + N1 * k2[None, :, None]
    ph = ((freq * k2[None, None, :]) % N).astype(F32) * (two_pi / N)
    cp, sp = jnp.cos(ph), jnp.sin(ph)
    g = jnp.concatenate([jnp.concatenate([cp, sp], axis=2),
                         jnp.concatenate([-sp, cp], axis=2)], axis=1)
    cpt, spt = jnp.swapaxes(cp, 1, 2), jnp.swapaxes(sp, 1, 2)
    gi = jnp.concatenate([jnp.concatenate([cpt, -spt], axis=2),
                          jnp.concatenate([spt, cpt], axis=2)], axis=1)
    return w1.astype(BF16), w1f.astype(BF16), w1i.astype(BF16), g.astype(BF16), gi.astype(BF16)


def _filter_positions(L):
    m = jnp.arange(2 * L, dtype=jnp.int32)
    pos = jnp.where(m < L, m, 2 * L - m)
    valid = (m != L).astype(F32)[:, None]
    pos = jnp.where(m == L, 0, pos).astype(F32)
    t = pos / (L - 1)
    w = 2.0 * math.pi * pos / L
    bands = jnp.linspace(1e-4, FILT_BANDS - 1, FILT_BANDS, dtype=F32)
    ang = w[:, None] * bands[None, :]
    z = jnp.concatenate([t[:, None], jnp.cos(ang), -jnp.sin(ang)], axis=-1)
    z = jnp.pad(z, ((0, 0), (0, LANES - FILT_EMB)))
    return z, valid


def _pad_to(a, shape):
    return jnp.pad(a, [(0, s - d) for d, s in zip(a.shape, shape)])


def _trunk(x, norm_w, w_in, q_norm_w, k_norm_w, conv_w, conv_b, filt_w1, filt_b1,
           filt_w2, filt_b2, filt_w3, filt_freq, filt_decay, hyena_bias,
           attn_out_norm_w, hyena_out_norm_w, w_out, final_norm_w):
    B, L, _ = x.shape
    assert B % 2 == 0 and L % 1024 == 0
    P = B // 2
    N1 = 2 * L // FFT_N2
    N1h = N1 // 2
    cols = FFT_N2 * HYENA_W
    T = B * L

    ctab, stab = _rope_tables(L)
    pmat = _head_mean_matrix()
    w1m, w1f, w1i, gmat, gimat = _dft_tables(L)
    ztab, valid = _filter_positions(L)
    reps = LANES // HEAD_DIM

    x2d = x.reshape(T, D_MODEL)
    for l in range(DEPTH):
        q, k, v, ga, u, gh = _inproj(
            x2d, norm_w[l][None], w_in[l].astype(BF16),
            jnp.tile(q_norm_w[l], reps)[None], jnp.tile(k_norm_w[l], reps)[None],
            pmat, ctab, stab, B, L)
        attn = _attention(q, k, v, B, L)

        w3 = filt_w3[l].reshape(FILT_HID, HYENA_ORDER, 2, HYENA_W)
        w3d = _pad_to(jnp.transpose(w3, (2, 0, 1, 3)).reshape(2, FILT_HID, HYENA_ORDER * HYENA_W),
                      (2, LANES, HYENA_ORDER * HYENA_W))
        dec = filt_decay[l].reshape(HYENA_ORDER, 2, HYENA_W)
        decd = jnp.transpose(dec, (1, 0, 2)).reshape(2, 1, HYENA_ORDER * HYENA_W)
        buf, ssum = _filters(
            ztab, valid,
            _pad_to(filt_w1[l], (LANES, LANES)), _pad_to(filt_b1[l][None], (1, LANES)),
            _pad_to(filt_w2[l], (LANES, LANES)), _pad_to(filt_b2[l][None], (1, LANES)),
            _pad_to(filt_freq[l][None], (1, LANES)), w3d, decd, L)
        W = HYENA_ORDER * HYENA_W
        hpre = _s1_filter(buf.reshape(N1, FFT_N2 * W), jnp.tile(ssum, (1, FFT_N2)), w1f, N1)
        hspec = _mid_filter(hpre.reshape(N1, 2 * FFT_N2, W), gmat, N1)

        uc = _short_conv(u.reshape(B, L, (HYENA_ORDER + 1) * HYENA_W),
                         conv_w[l], conv_b[l][None], B, L)
        parts = uc.reshape(HYENA_ORDER + 1, 2, P, N1h, cols)
        zz = parts
        zz_which = 0
        for o in range(HYENA_ORDER):
            o1 = _s1(zz, zz_which, w1m, P, N1)
            e = _mid(o1.reshape(P, N1, 2 * FFT_N2, HYENA_W), gmat, gimat, hspec, o, P, N1)
            bias_t = jnp.tile(hyena_bias[l, o][None], (1, FFT_N2))
            zz = _s1inv(e.reshape(P, 2 * N1, cols), w1i, zz, zz_which, parts, o + 1, bias_t, P, N1)
            zz_which = 0
        zz2d = zz.reshape(T, HYENA_W)

        x2d = _outproj(attn.reshape(T, ATTN_W), ga, zz2d, gh, x2d,
                       attn_out_norm_w[l][None], hyena_out_norm_w[l][None],
                       w_out[l].astype(BF16), final_norm_w[None], final=(l == DEPTH - 1))
    return x2d.reshape(B, L, D_MODEL)


def kernel(x_prompt, x_sample, norm_w, w_in, q_norm_w, k_norm_w, conv_w, conv_b, filt_w1, filt_b1, filt_w2, filt_b2, filt_w3, filt_freq, filt_decay, hyena_bias, attn_out_norm_w, hyena_out_norm_w, w_out, final_norm_w):
    weights = (norm_w, w_in, q_norm_w, k_norm_w, conv_w, conv_b, filt_w1, filt_b1, filt_w2, filt_b2,
               filt_w3, filt_freq, filt_decay, hyena_bias, attn_out_norm_w, hyena_out_norm_w, w_out,
               final_norm_w)
    return (_trunk(x_prompt, *weights), _trunk(x_sample, *weights))
```

```python
import functools
import math

import jax
import jax.numpy as jnp
from jax import lax
from jax.experimental import pallas as pl
from jax.experimental.pallas import tpu as pltpu

F32 = jnp.float32
BF16 = jnp.bfloat16

D_MODEL = 1024
DEPTH = 4
GRID_W = 64
HEAD_DIM = 64
N_Q_HEADS = 8
N_KV_HEADS = 2
Q_PER_KV = N_Q_HEADS // N_KV_HEADS
ATTN_W = N_Q_HEADS * HEAD_DIM
KV_W = N_KV_HEADS * HEAD_DIM
HYENA_W = 512
HYENA_ORDER = 2
MIX_W = ATTN_W + HYENA_W
FILT_EMB = 33
FILT_BANDS = 16
FILT_HID = 64
N_FILT = HYENA_ORDER * 2 * HYENA_W
MOD_SHIFT = 0.05
ROPE_THETA = 10000.0
ROPE_FREQS = HEAD_DIM // 4
EPS = 1e-6
COL_K = ATTN_W
COL_V = COL_K + KV_W
COL_GA = COL_V + KV_W
COL_U = COL_GA + ATTN_W
COL_GH = COL_U + (HYENA_ORDER + 1) * HYENA_W
D_IN_PROJ = COL_GH + HYENA_W

QK_SCALE = math.log2(math.e) / math.sqrt(HEAD_DIM)
KV_CHUNK = 512
Q_TILE = 512
V_ROWS = HEAD_DIM + 16
LANES = 128
FFT_N2 = 128
VMEM_LIMIT_BYTES = 48 * 1024 * 1024
HIGHEST = lax.Precision.HIGHEST


def _cparams(*sem):
    return pltpu.CompilerParams(dimension_semantics=sem, vmem_limit_bytes=VMEM_LIMIT_BYTES)


def _dot(a, b):
    return jnp.dot(a, b, preferred_element_type=F32)


def _dot_f32(a, b):
    return jnp.dot(a, b, preferred_element_type=F32, precision=HIGHEST)


def _inproj_body(x_ref, nw_ref, w_ref, qnw_ref, knw_ref, p_ref, c_ref, s_ref,
                 q_ref, k_ref, v_ref, ga_ref, u_ref, gh_ref):
    x = x_ref[...]
    ms = jnp.mean(x * x, axis=-1, keepdims=True)
    h = (x * lax.rsqrt(ms + EPS) * nw_ref[...]).astype(BF16)
    p = p_ref[...]
    c = c_ref[...]
    s = s_ref[...]
    lane = lax.broadcasted_iota(jnp.int32, c.shape, 1)
    first_half = (lane % (2 * ROPE_FREQS)) < ROPE_FREQS

    def norm_rope(y, nw):
        y2 = y * y
        hi = y2.astype(BF16)
        lo = (y2 - hi.astype(F32)).astype(BF16)
        msq = _dot(hi, p) + _dot(lo, p)
        yn = y * lax.rsqrt(msq + EPS) * nw
        partner = jnp.where(first_half, pltpu.roll(yn, LANES - ROPE_FREQS, 1),
                            pltpu.roll(yn, ROPE_FREQS, 1))
        return yn * c + partner * s

    q = _dot(h, w_ref[:, 0:COL_K])
    qnw = qnw_ref[...]
    for j in range(ATTN_W // LANES):
        r = norm_rope(q[:, j * LANES:(j + 1) * LANES], qnw) * QK_SCALE
        rt = r.T
        q_ref[0, 2 * j] = rt[:HEAD_DIM].astype(BF16)
        q_ref[0, 2 * j + 1] = rt[HEAD_DIM:].astype(BF16)
    kv = _dot(h, w_ref[:, COL_K:COL_GA])
    kr = norm_rope(kv[:, :KV_W], knw_ref[...])
    k_ref[0, 0] = kr[:, :HEAD_DIM].astype(BF16)
    k_ref[0, 1] = pltpu.roll(kr, HEAD_DIM, 1)[:, :HEAD_DIM].astype(BF16)
    vt = kv[:, KV_W:].T
    extra = lax.broadcasted_iota(jnp.int32, (V_ROWS - HEAD_DIM, KV_CHUNK), 0)
    ones_row = jnp.where(extra == 0, 1.0, 0.0).astype(BF16)
    for c in range(vt.shape[1] // KV_CHUNK):
        for g in range(N_KV_HEADS):
            v_ref[0, g, c, 0:HEAD_DIM, :] = vt[g * HEAD_DIM:(g + 1) * HEAD_DIM,
                                               c * KV_CHUNK:(c + 1) * KV_CHUNK].astype(BF16)
            v_ref[0, g, c, HEAD_DIM:V_ROWS, :] = ones_row
    ga_ref[...] = _dot(h, w_ref[:, COL_GA:COL_U])
    u_ref[...] = _dot(h, w_ref[:, COL_U:COL_GH])
    gh_ref[...] = _dot(h, w_ref[:, COL_GH:D_IN_PROJ])


def _inproj(x2d, nw, w_bf, qnw, knw, pmat, ctab, stab, B, L, tm=512):
    T = B * L
    nlb = L // tm
    row = lambda i: (i, 0)
    const = lambda i: (0, 0)
    pos = lambda i: (i % nlb, 0)
    heads = lambda i: (i // nlb, 0, i % nlb, 0)
    return pl.pallas_call(
        _inproj_body,
        grid=(T // tm,),
        in_specs=[
            pl.BlockSpec((tm, D_MODEL), row),
            pl.BlockSpec((1, D_MODEL), const),
            pl.BlockSpec((D_MODEL, D_IN_PROJ), const),
            pl.BlockSpec((1, LANES), const),
            pl.BlockSpec((1, LANES), const),
            pl.BlockSpec((LANES, LANES), const),
            pl.BlockSpec((tm, LANES), pos),
            pl.BlockSpec((tm, LANES), pos),
        ],
        out_specs=[
            pl.BlockSpec((1, N_Q_HEADS, HEAD_DIM, tm), lambda i: (i // nlb, 0, 0, i % nlb)),
            pl.BlockSpec((1, N_KV_HEADS, tm, HEAD_DIM), heads),
            pl.BlockSpec((1, N_KV_HEADS, tm // KV_CHUNK, V_ROWS, KV_CHUNK),
                         lambda i: (i // nlb, 0, i % nlb, 0, 0)),
            pl.BlockSpec((tm, ATTN_W), row),
            pl.BlockSpec((tm, (HYENA_ORDER + 1) * HYENA_W), row),
            pl.BlockSpec((tm, HYENA_W), row),
        ],
        out_shape=[
            jax.ShapeDtypeStruct((B, N_Q_HEADS, HEAD_DIM, L), BF16),
            jax.ShapeDtypeStruct((B, N_KV_HEADS, L, HEAD_DIM), BF16),
            jax.ShapeDtypeStruct((B, N_KV_HEADS, L // KV_CHUNK, V_ROWS, KV_CHUNK), BF16),
            jax.ShapeDtypeStruct((T, ATTN_W), F32),
            jax.ShapeDtypeStruct((T, (HYENA_ORDER + 1) * HYENA_W), F32),
            jax.ShapeDtypeStruct((T, HYENA_W), F32),
        ],
        compiler_params=_cparams("parallel"),
        name="inproj",
    )(x2d, nw, w_bf, qnw, knw, pmat, ctab, stab)


def _attn_body(qt_ref, k_ref, vt_ref, o_ref, s_ref, p_ref, al_ref, m_ref, acc_ref, *, tq, nk):
    cols = Q_PER_KV * tq
    m_ref[...] = jnp.full(m_ref.shape, -jnp.inf, F32)
    acc_ref[...] = jnp.zeros(acc_ref.shape, F32)
    p_ref[1] = jnp.zeros(p_ref.shape[1:], BF16)
    al_ref[1] = jnp.ones(al_ref.shape[1:], F32)

    def scores(j, slot):
        start = pl.multiple_of(j * KV_CHUNK, KV_CHUNK)
        k = k_ref[0, 0, pl.ds(start, KV_CHUNK), :]
        for h in range(Q_PER_KV):
            s_ref[slot, :, h * tq:(h + 1) * tq] = _dot(k, qt_ref[0, h])

    def accumulate(j, slot):
        acc_ref[...] = al_ref[slot] * acc_ref[...] + _dot(vt_ref[0, 0, j], p_ref[slot])

    def softmax(slot):
        for c in range(cols // LANES):
            cs = slice(c * LANES, (c + 1) * LANES)
            s = s_ref[slot, :, cs]
            m_old = m_ref[:, cs]
            m_new = jnp.maximum(m_old, jnp.max(s, axis=0, keepdims=True))
            m_ref[:, cs] = m_new
            al_ref[slot, :, cs] = jnp.exp2(m_old - m_new)
            p_ref[slot, :, cs] = jnp.exp2(s - m_new).astype(BF16)

    scores(0, 0)

    def step(jj, carry):
        j = 2 * jj
        scores(j + 1, 1)
        accumulate(jnp.maximum(j - 1, 0), 1)
        softmax(0)
        scores(jnp.minimum(j + 2, nk - 1), 0)
        accumulate(j, 0)
        softmax(1)
        return carry

    lax.fori_loop(0, nk // 2, step, 0)
    accumulate(nk - 1, 1)
    o = acc_ref[0:HEAD_DIM, :] / acc_ref[HEAD_DIM:HEAD_DIM + 1, :]
    ot = jnp.concatenate([o[:, h * tq:(h + 1) * tq] for h in range(Q_PER_KV)], axis=0)
    o_ref[0] = ot.T


def _attention(qt, k, vt, B, L):
    tq = Q_TILE
    nk = L // KV_CHUNK
    assert nk % 2 == 0
    cols = Q_PER_KV * tq
    body = functools.partial(_attn_body, tq=tq, nk=nk)
    return pl.pallas_call(
        body,
        grid=(B, N_KV_HEADS, L // tq),
        in_specs=[
            pl.BlockSpec((1, Q_PER_KV, HEAD_DIM, tq), lambda b, g, i: (b, g, 0, i)),
            pl.BlockSpec((1, 1, L, HEAD_DIM), lambda b, g, i: (b, g, 0, 0)),
            pl.BlockSpec((1, 1, nk, V_ROWS, KV_CHUNK), lambda b, g, i: (b, g, 0, 0, 0)),
        ],
        out_specs=pl.BlockSpec((1, tq, Q_PER_KV * HEAD_DIM), lambda b, g, i: (b, i, g)),
        out_shape=jax.ShapeDtypeStruct((B, L, ATTN_W), F32),
        scratch_shapes=[
            pltpu.VMEM((2, KV_CHUNK, cols), F32),
            pltpu.VMEM((2, KV_CHUNK, cols), BF16),
            pltpu.VMEM((2, 1, cols), F32),
            pltpu.VMEM((1, cols), F32),
            pltpu.VMEM((V_ROWS, cols), F32),
        ],
        compiler_params=_cparams("parallel", "parallel", "parallel"),
        name="attention",
    )(qt, k, vt)


def _sconv_body(u_ref, prev_ref, next_ref, w_ref, b_ref, o_ref, *, nblk):
    i = pl.program_id(2)
    x = u_ref[0]
    R = x.shape[0]
    halo = prev_ref.shape[1]
    prev_row = jnp.where(i == 0, 0.0, prev_ref[0, halo - 1:halo, :])
    next_row = jnp.where(i == nblk - 1, 0.0, next_ref[0, 0:1, :])
    row = lax.broadcasted_iota(jnp.int32, x.shape, 0)
    xm = jnp.where(row == 0, prev_row, pltpu.roll(x, 1, 0))
    xp = jnp.where(row == R - 1, next_row, pltpu.roll(x, R - 1, 0))
    w = w_ref[...]
    o_ref[0] = xm * w[0:1] + x * w[1:2] + xp * w[2:3] + b_ref[...]


def _short_conv(u3, cw, cb, B, L, R=2048, halo=8):
    C = u3.shape[-1]
    W = HYENA_W
    nblk = L // R
    rpb = R // halo
    body = functools.partial(_sconv_body, nblk=nblk)
    return pl.pallas_call(
        body,
        grid=(B, C // W, nblk),
        in_specs=[
            pl.BlockSpec((1, R, W), lambda b, j, i: (b, i, j)),
            pl.BlockSpec((1, halo, W), lambda b, j, i: (b, jnp.maximum(i * rpb - 1, 0), j)),
            pl.BlockSpec((1, halo, W), lambda b, j, i: (b, jnp.minimum((i + 1) * rpb, L // halo - 1), j)),
            pl.BlockSpec((3, W), lambda b, j, i: (0, j)),
            pl.BlockSpec((1, W), lambda b, j, i: (0, j)),
        ],
        out_specs=pl.BlockSpec((None, 1, R, W), lambda b, j, i: (j, b, i, 0)),
        out_shape=jax.ShapeDtypeStruct((C // HYENA_W, B, L, HYENA_W), F32),
        compiler_params=_cparams("parallel", "parallel", "parallel"),
        name="short_conv",
    )(u3, u3, u3, cw, cb)


def _filt_body(z_ref, valid_ref, w1_ref, b1_ref, w2_ref, b2_ref, fr_ref, w3_ref, dec_ref,
               buf_ref, sum_ref):
    i = pl.program_id(0)
    z = z_ref[...]
    fr = fr_ref[...]
    h = jnp.sin(fr * (_dot_f32(z, w1_ref[...]) + b1_ref[...]))
    h = jnp.sin(fr * (_dot_f32(h, w2_ref[...]) + b2_ref[...]))
    h = _dot_f32(h, w3_ref[0])
    t = z[:, 0:1]
    win = jnp.exp(-t * jnp.abs(dec_ref[0])) + MOD_SHIFT
    out = h * win * valid_ref[...]
    buf_ref[...] = out

    @pl.when(i == 0)
    def _():
        sum_ref[...] = jnp.zeros_like(sum_ref)

    sum_ref[...] += jnp.sum(jnp.abs(out), axis=0, keepdims=True)


def _filters(ztab, valid, w1, b1, w2, b2, fr, w3d, decd, L, rb=512):
    n = 2 * L
    nb = n // rb
    half = nb // 2
    W = HYENA_ORDER * HYENA_W
    const = lambda i: (0, 0)
    return pl.pallas_call(
        _filt_body,
        grid=(nb,),
        in_specs=[
            pl.BlockSpec((rb, LANES), lambda i: (i, 0)),
            pl.BlockSpec((rb, 1), lambda i: (i, 0)),
            pl.BlockSpec((LANES, LANES), const),
            pl.BlockSpec((1, LANES), const),
            pl.BlockSpec((LANES, LANES), const),
            pl.BlockSpec((1, LANES), const),
            pl.BlockSpec((1, LANES), const),
            pl.BlockSpec((1, LANES, W), lambda i: (i // half, 0, 0)),
            pl.BlockSpec((1, 1, W), lambda i: (i // half, 0, 0)),
        ],
        out_specs=[
            pl.BlockSpec((rb, W), lambda i: (i, 0)),
            pl.BlockSpec((1, W), const),
        ],
        out_shape=[
            jax.ShapeDtypeStruct((n, W), F32),
            jax.ShapeDtypeStruct((1, W), F32),
        ],
        compiler_params=_cparams("arbitrary"),
        name="filters",
    )(ztab, valid, w1, b1, w2, b2, fr, w3d, decd)


def _s1_body(a_ref, w_ref, o_ref):
    a = a_ref[...]
    z = a.reshape(a.shape[0] * a.shape[1], a.shape[2]).astype(BF16)
    o_ref[0] = _dot(w_ref[...], z).astype(o_ref.dtype)


def _s1(src, which, w1m, P, N1, cb=2048):
    N1h = N1 // 2
    cols = src.shape[-1]
    return pl.pallas_call(
        _s1_body,
        grid=(P, cols // cb),
        in_specs=[
            pl.BlockSpec((None, 2, None, N1h, cb), lambda p, j: (which, 0, p, 0, j)),
            pl.BlockSpec((2 * N1, N1), lambda p, j: (0, 0)),
        ],
        out_specs=pl.BlockSpec((1, 2 * N1, cb), lambda p, j: (p, 0, j)),
        out_shape=jax.ShapeDtypeStruct((P, 2 * N1, cols), BF16),
        compiler_params=_cparams("parallel", "parallel"),
        name="dft_n1",
    )(src, w1m)


def _s1f_body(b_ref, s_ref, w_ref, o_ref):
    z = (b_ref[...] / s_ref[...]).astype(BF16)
    o_ref[...] = _dot(w_ref[...], z).astype(o_ref.dtype)


def _s1_filter(buf2d, ssum_t, w1f, N1, cb=2048):
    cols = buf2d.shape[-1]
    return pl.pallas_call(
        _s1f_body,
        grid=(cols // cb,),
        in_specs=[
            pl.BlockSpec((N1, cb), lambda j: (0, j)),
            pl.BlockSpec((1, cb), lambda j: (0, j)),
            pl.BlockSpec((2 * N1, N1), lambda j: (0, 0)),
        ],
        out_specs=pl.BlockSpec((2 * N1, cb), lambda j: (0, j)),
        out_shape=jax.ShapeDtypeStruct((2 * N1, cols), BF16),
        compiler_params=_cparams("parallel"),
        name="dft_n1_filter",
    )(buf2d, ssum_t, w1f)


def _midf_body(x_ref, g_ref, h_ref, *, kb):
    for k in range(kb):
        h_ref[k] = _dot(g_ref[k], x_ref[k])


def _mid_filter(hpre, g, N1, kb=4):
    W = hpre.shape[-1]
    body = functools.partial(_midf_body, kb=kb)
    return pl.pallas_call(
        body,
        grid=(N1 // kb,),
        in_specs=[
            pl.BlockSpec((kb, 2 * FFT_N2, W), lambda i: (i, 0, 0)),
            pl.BlockSpec((kb, 2 * FFT_N2, 2 * FFT_N2), lambda i: (i, 0, 0)),
        ],
        out_specs=pl.BlockSpec((kb, 2 * FFT_N2, W), lambda i: (i, 0, 0)),
        out_shape=jax.ShapeDtypeStruct((N1, 2 * FFT_N2, W), F32),
        compiler_params=_cparams("parallel"),
        name="dft_n2_filter",
    )(hpre, g)


def _mid_body(x_ref, g_ref, gi_ref, h_ref, o_ref, *, kb):
    for k in range(kb):
        d = _dot(g_ref[k], x_ref[0, k])
        hh = h_ref[k]
        dr, di = d[:FFT_N2], d[FFT_N2:]
        hr, hi = hh[:FFT_N2], hh[FFT_N2:]
        y = jnp.concatenate([dr * hr - di * hi, dr * hi + di * hr], axis=0).astype(BF16)
        o_ref[0, k] = _dot(gi_ref[k], y).astype(o_ref.dtype)


def _mid(x4, g, gi, hspec, order, P, N1, kb=4):
    C = HYENA_W
    body = functools.partial(_mid_body, kb=kb)
    return pl.pallas_call(
        body,
        grid=(N1 // kb, P),
        in_specs=[
            pl.BlockSpec((1, kb, 2 * FFT_N2, C), lambda i, p: (p, i, 0, 0)),
            pl.BlockSpec((kb, 2 * FFT_N2, 2 * FFT_N2), lambda i, p: (i, 0, 0)),
            pl.BlockSpec((kb, 2 * FFT_N2, 2 * FFT_N2), lambda i, p: (i, 0, 0)),
            pl.BlockSpec((kb, 2 * FFT_N2, C), lambda i, p: (i, 0, order)),
        ],
        out_specs=pl.BlockSpec((1, kb, 2 * FFT_N2, C), lambda i, p: (p, i, 0, 0)),
        out_shape=jax.ShapeDtypeStruct((P, N1, 2 * FFT_N2, C), BF16),
        compiler_params=_cparams("parallel", "parallel"),
        name="dft_n2_conv",
    )(x4, g, gi, hspec)


def _s1inv_body(e_ref, w_ref, src_ref, mul_ref, bias_ref, o_ref):
    y = _dot(w_ref[...], e_ref[0])
    src = src_ref[...]
    y = y.reshape(src.shape)
    o_ref[...] = mul_ref[...] * (y + src * bias_ref[...])


def _s1inv(e3, w1i, src, src_which, mul, mul_which, bias_t, P, N1, cb=2048):
    N1h = N1 // 2
    cols = e3.shape[-1]
    member = lambda which: pl.BlockSpec((None, 2, None, N1h, cb), lambda p, j: (which, 0, p, 0, j))
    return pl.pallas_call(
        _s1inv_body,
        grid=(P, cols // cb),
        in_specs=[
            pl.BlockSpec((1, 2 * N1, cb), lambda p, j: (p, 0, j)),
            pl.BlockSpec((N1, 2 * N1), lambda p, j: (0, 0)),
            member(src_which),
            member(mul_which),
            pl.BlockSpec((1, cb), lambda p, j: (0, j)),
        ],
        out_specs=pl.BlockSpec((None, 2, None, N1h, cb), lambda p, j: (0, 0, p, 0, j)),
        out_shape=jax.ShapeDtypeStruct((1, 2, P, N1h, cols), F32),
        compiler_params=_cparams("parallel", "parallel"),
        name="idft_n1",
    )(e3, w1i, src, mul, bias_t)


def _outproj_body(a_ref, ga_ref, z_ref, gh_ref, x_ref, wa_ref, wh_ref, w_ref, fw_ref, o_ref, *, final):
    def norm_gate(y, w, g):
        ms = jnp.mean(y * y, axis=-1, keepdims=True)
        return (y * lax.rsqrt(ms + EPS) * w) * (g * (1.0 / (1.0 + jnp.exp(-g))))

    oa = norm_gate(a_ref[...], wa_ref[...], ga_ref[...]).astype(BF16)
    oh = norm_gate(z_ref[...], wh_ref[...], gh_ref[...]).astype(BF16)
    y = x_ref[...] + (_dot(oa, w_ref[0:ATTN_W, :]) + _dot(oh, w_ref[ATTN_W:MIX_W, :]))
    if final:
        ms = jnp.mean(y * y, axis=-1, keepdims=True)
        y = y * lax.rsqrt(ms + EPS) * fw_ref[...]
    o_ref[...] = y


def _outproj(attn2d, ga, zz2d, gh, x2d, wa, wh, w_bf, fw, final, tm=512):
    T = x2d.shape[0]
    row = lambda i: (i, 0)
    const = lambda i: (0, 0)
    body = functools.partial(_outproj_body, final=final)
    return pl.pallas_call(
        body,
        grid=(T // tm,),
        in_specs=[
            pl.BlockSpec((tm, ATTN_W), row),
            pl.BlockSpec((tm, ATTN_W), row),
            pl.BlockSpec((tm, HYENA_W), row),
            pl.BlockSpec((tm, HYENA_W), row),
            pl.BlockSpec((tm, D_MODEL), row),
            pl.BlockSpec((1, ATTN_W), const),
            pl.BlockSpec((1, HYENA_W), const),
            pl.BlockSpec((MIX_W, D_MODEL), const),
            pl.BlockSpec((1, D_MODEL), const),
        ],
        out_specs=pl.BlockSpec((tm, D_MODEL), row),
        out_shape=jax.ShapeDtypeStruct((T, D_MODEL), F32),
        compiler_params=_cparams("parallel"),
        name="outproj",
    )(attn2d, ga, zz2d, gh, x2d, wa, wh, w_bf, fw)


def _rope_tables(L):
    t = jnp.arange(L, dtype=jnp.int32)
    pos = jnp.stack([t // GRID_W, t % GRID_W], axis=-1).astype(F32)
    freqs = ROPE_THETA ** (-jnp.arange(ROPE_FREQS, dtype=F32) / ROPE_FREQS)
    ang = pos[:, :, None] * freqs
    cos, sin = jnp.cos(ang), jnp.sin(ang)
    c_head = jnp.stack([cos, cos], axis=2).reshape(L, HEAD_DIM)
    s_head = jnp.stack([-sin, sin], axis=2).reshape(L, HEAD_DIM)
    reps = LANES // HEAD_DIM
    return jnp.tile(c_head, (1, reps)), jnp.tile(s_head, (1, reps))


def _head_mean_matrix():
    idx = jnp.arange(LANES) // HEAD_DIM
    return jnp.where(idx[:, None] == idx[None, :], 1.0 / HEAD_DIM, 0.0).astype(BF16)


def _dft_tables(L):
    N = 2 * L
    N1 = N // FFT_N2
    N1h = N1 // 2
    two_pi = 2.0 * math.pi
    k1 = jnp.arange(N1, dtype=jnp.int32)
    th = ((k1[:, None] * k1[None, :]) % N1).astype(F32) * (two_pi / N1)
    c, s = jnp.cos(th), jnp.sin(th)
    ch, sh = c[:, :N1h], s[:, :N1h]
    w1 = jnp.stack([jnp.concatenate([ch, sh], axis=1),
                    jnp.concatenate([-sh, ch], axis=1)], axis=1).reshape(2 * N1, N1)
    w1f = jnp.stack([c, -s], axis=1).reshape(2 * N1, N1)
    ct, st = ch.T, sh.T
    top = jnp.stack([ct, -st], axis=2).reshape(N1h, 2 * N1)
    bot = jnp.stack([st, ct], axis=2).reshape(N1h, 2 * N1)
    w1i = jnp.concatenate([top, bot], axis=0) * (1.0 / N)
    k2 = jnp.arange(FFT_N2, dtype=jnp.int32)
    freq = k1[:, None, None] + N1 * k2[None, :, None]
    ph = ((freq * k2[None, None, :]) % N).astype(F32) * (two_pi / N)
    cp, sp = jnp.cos(ph), jnp.sin(ph)
    g = jnp.concatenate([jnp.concatenate([cp, sp], axis=2),
                         jnp.concatenate([-sp, cp], axis=2)], axis=1)
    cpt, spt = jnp.swapaxes(cp, 1, 2), jnp.swapaxes(sp, 1, 2)
    gi = jnp.concatenate([jnp.concatenate([cpt, -spt], axis=2),
                          jnp.concatenate([spt, cpt], axis=2)], axis=1)
    return w1.astype(BF16), w1f.astype(BF16), w1i.astype(BF16), g.astype(BF16), gi.astype(BF16)


def _filter_positions(L):
    m = jnp.arange(2 * L, dtype=jnp.int32)
    pos = jnp.where(m < L, m, 2 * L - m)
    valid = (m != L).astype(F32)[:, None]
    pos = jnp.where(m == L, 0, pos).astype(F32)
    t = pos / (L - 1)
    w = 2.0 * math.pi * pos / L
    bands = jnp.linspace(1e-4, FILT_BANDS - 1, FILT_BANDS, dtype=F32)
    ang = w[:, None] * bands[None, :]
    z = jnp.concatenate([t[:, None], jnp.cos(ang), -jnp.sin(ang)], axis=-1)
    z = jnp.pad(z, ((0, 0), (0, LANES - FILT_EMB)))
    return z, valid


def _pad_to(a, shape):
    return jnp.pad(a, [(0, s - d) for d, s in zip(a.shape, shape)])


def _trunk(x, norm_w, w_in, q_norm_w, k_norm_w, conv_w, conv_b, filt_w1, filt_b1,
           filt_w2, filt_b2, filt_w3, filt_freq, filt_decay, hyena_bias,
           attn_out_norm_w, hyena_out_norm_w, w_out, final_norm_w):
    B, L, _ = x.shape
    assert B % 2 == 0 and L % 1024 == 0
    P = B // 2
    N1 = 2 * L // FFT_N2
    N1h = N1 // 2
    cols = FFT_N2 * HYENA_W
    T = B * L

    ctab, stab = _rope_tables(L)
    pmat = _head_mean_matrix()
    w1m, w1f, w1i, gmat, gimat = _dft_tables(L)
    ztab, valid = _filter_positions(L)
    reps = LANES // HEAD_DIM

    x2d = x.reshape(T, D_MODEL)
    for l in range(DEPTH):
        q, k, v, ga, u, gh = _inproj(
            x2d, norm_w[l][None], w_in[l].astype(BF16),
            jnp.tile(q_norm_w[l], reps)[None], jnp.tile(k_norm_w[l], reps)[None],
            pmat, ctab, stab, B, L)
        attn = _attention(q, k, v, B, L)

        w3 = filt_w3[l].reshape(FILT_HID, HYENA_ORDER, 2, HYENA_W)
        w3d = _pad_to(jnp.transpose(w3, (2, 0, 1, 3)).reshape(2, FILT_HID, HYENA_ORDER * HYENA_W),
                      (2, LANES, HYENA_ORDER * HYENA_W))
        dec = filt_decay[l].reshape(HYENA_ORDER, 2, HYENA_W)
        decd = jnp.transpose(dec, (1, 0, 2)).reshape(2, 1, HYENA_ORDER * HYENA_W)
        buf, ssum = _filters(
            ztab, valid,
            _pad_to(filt_w1[l], (LANES, LANES)), _pad_to(filt_b1[l][None], (1, LANES)),
            _pad_to(filt_w2[l], (LANES, LANES)), _pad_to(filt_b2[l][None], (1, LANES)),
            _pad_to(filt_freq[l][None], (1, LANES)), w3d, decd, L)
        W = HYENA_ORDER * HYENA_W
        hpre = _s1_filter(buf.reshape(N1, FFT_N2 * W), jnp.tile(ssum, (1, FFT_N2)), w1f, N1)
        hspec = _mid_filter(hpre.reshape(N1, 2 * FFT_N2, W), gmat, N1)

        uc = _short_conv(u.reshape(B, L, (HYENA_ORDER + 1) * HYENA_W),
                         conv_w[l], conv_b[l][None], B, L)
        parts = uc.reshape(HYENA_ORDER + 1, 2, P, N1h, cols)
        zz = parts
        zz_which = 0
        for o in range(HYENA_ORDER):
            o1 = _s1(zz, zz_which, w1m, P, N1)
            e = _mid(o1.reshape(P, N1, 2 * FFT_N2, HYENA_W), gmat, gimat, hspec, o, P, N1)
            bias_t = jnp.tile(hyena_bias[l, o][None], (1, FFT_N2))
            zz = _s1inv(e.reshape(P, 2 * N1, cols), w1i, zz, zz_which, parts, o + 1, bias_t, P, N1)
            zz_which = 0
        zz2d = zz.reshape(T, HYENA_W)

        x2d = _outproj(attn.reshape(T, ATTN_W), ga, zz2d, gh, x2d,
                       attn_out_norm_w[l][None], hyena_out_norm_w[l][None],
                       w_out[l].astype(BF16), final_norm_w[None], final=(l == DEPTH - 1))
    return x2d.reshape(B, L, D_MODEL)


def kernel(x_prompt, x_sample, norm_w, w_in, q_norm_w, k_norm_w, conv_w, conv_b, filt_w1, filt_b1, filt_w2, filt_b2, filt_w3, filt_freq, filt_decay, hyena_bias, attn_out_norm_w, hyena_out_norm_w, w_out, final_norm_w):
    weights = (norm_w, w_in, q_norm_w, k_norm_w, conv_w, conv_b, filt_w1, filt_b1, filt_w2, filt_b2,
               filt_w3, filt_freq, filt_decay, hyena_bias, attn_out_norm_w, hyena_out_norm_w, w_out,
               final_norm_w)
    return (_trunk(x_prompt, *weights), _trunk(x_sample, *weights))
```

```python
import functools
import math

import jax
import jax.numpy as jnp
from jax import lax
from jax.experimental import pallas as pl
from jax.experimental.pallas import tpu as pltpu

F32 = jnp.float32
BF16 = jnp.bfloat16

D_MODEL = 1024
DEPTH = 4
GRID_W = 64
HEAD_DIM = 64
N_Q_HEADS = 8
N_KV_HEADS = 2
Q_PER_KV = N_Q_HEADS // N_KV_HEADS
ATTN_W = N_Q_HEADS * HEAD_DIM
KV_W = N_KV_HEADS * HEAD_DIM
HYENA_W = 512
HYENA_ORDER = 2
MIX_W = ATTN_W + HYENA_W
FILT_EMB = 33
FILT_BANDS = 16
FILT_HID = 64
N_FILT = HYENA_ORDER * 2 * HYENA_W
MOD_SHIFT = 0.05
ROPE_THETA = 10000.0
ROPE_FREQS = HEAD_DIM // 4
EPS = 1e-6
COL_K = ATTN_W
COL_V = COL_K + KV_W
COL_GA = COL_V + KV_W
COL_U = COL_GA + ATTN_W
COL_GH = COL_U + (HYENA_ORDER + 1) * HYENA_W
D_IN_PROJ = COL_GH + HYENA_W

QK_SCALE = math.log2(math.e) / math.sqrt(HEAD_DIM)
KV_CHUNK = 512
Q_TILE = 512
Q_SUB = 256
V_ROWS = HEAD_DIM + 16
LANES = 128
FFT_N2 = 128
CH_BLOCKS = HYENA_W // LANES
FILT_BLOCKS = HYENA_ORDER * CH_BLOCKS
R_BLOCK = 8
K_BLOCK = 8
VMEM_LIMIT_BYTES = 48 * 1024 * 1024
HIGHEST = lax.Precision.HIGHEST


def _cparams(*sem):
    return pltpu.CompilerParams(dimension_semantics=sem, vmem_limit_bytes=VMEM_LIMIT_BYTES)


def _dot(a, b):
    return jnp.dot(a, b, preferred_element_type=F32)


def _dot_f32(a, b):
    return jnp.dot(a, b, preferred_element_type=F32, precision=HIGHEST)


def _inproj_body(x_ref, nw_ref, w_ref, qnw_ref, knw_ref, p_ref, c_ref, s_ref,
                 q_ref, k_ref, v_ref, ga_ref, u_ref, gh_ref):
    x = x_ref[...]
    ms = jnp.mean(x * x, axis=-1, keepdims=True)
    h = (x * lax.rsqrt(ms + EPS) * nw_ref[...]).astype(BF16)
    p = p_ref[...]
    c = c_ref[...]
    s = s_ref[...]
    lane = lax.broadcasted_iota(jnp.int32, c.shape, 1)
    first_half = (lane % (2 * ROPE_FREQS)) < ROPE_FREQS

    def norm_rope(y, nw):
        y2 = y * y
        hi = y2.astype(BF16)
        lo = (y2 - hi.astype(F32)).astype(BF16)
        msq = _dot(hi, p) + _dot(lo, p)
        yn = y * lax.rsqrt(msq + EPS) * nw
        partner = jnp.where(first_half, pltpu.roll(yn, LANES - ROPE_FREQS, 1),
                            pltpu.roll(yn, ROPE_FREQS, 1))
        return yn * c + partner * s

    q = _dot(h, w_ref[:, 0:COL_K])
    qnw = qnw_ref[...]
    for j in range(ATTN_W // LANES):
        r = norm_rope(q[:, j * LANES:(j + 1) * LANES], qnw) * QK_SCALE
        rt = r.T
        q_ref[0, 2 * j] = rt[:HEAD_DIM].astype(BF16)
        q_ref[0, 2 * j + 1] = rt[HEAD_DIM:].astype(BF16)
    kv = _dot(h, w_ref[:, COL_K:COL_GA])
    kr = norm_rope(kv[:, :KV_W], knw_ref[...])
    k_ref[0, 0] = kr[:, :HEAD_DIM].astype(BF16)
    k_ref[0, 1] = pltpu.roll(kr, HEAD_DIM, 1)[:, :HEAD_DIM].astype(BF16)
    vt = kv[:, KV_W:].T
    extra = lax.broadcasted_iota(jnp.int32, (V_ROWS - HEAD_DIM, KV_CHUNK), 0)
    ones_row = jnp.where(extra == 0, 1.0, 0.0).astype(BF16)
    for c in range(vt.shape[1] // KV_CHUNK):
        for g in range(N_KV_HEADS):
            v_ref[0, g, c, 0:HEAD_DIM, :] = vt[g * HEAD_DIM:(g + 1) * HEAD_DIM,
                                               c * KV_CHUNK:(c + 1) * KV_CHUNK].astype(BF16)
            v_ref[0, g, c, HEAD_DIM:V_ROWS, :] = ones_row
    ga_ref[...] = _dot(h, w_ref[:, COL_GA:COL_U])
    u_ref[...] = _dot(h, w_ref[:, COL_U:COL_GH])
    gh_ref[...] = _dot(h, w_ref[:, COL_GH:D_IN_PROJ])


def _inproj(x2d, nw, w_bf, qnw, knw, pmat, ctab, stab, B, L, tm=512):
    T = B * L
    nlb = L // tm
    row = lambda i: (i, 0)
    const = lambda i: (0, 0)
    pos = lambda i: (i % nlb, 0)
    heads = lambda i: (i // nlb, 0, i % nlb, 0)
    return pl.pallas_call(
        _inproj_body,
        grid=(T // tm,),
        in_specs=[
            pl.BlockSpec((tm, D_MODEL), row),
            pl.BlockSpec((1, D_MODEL), const),
            pl.BlockSpec((D_MODEL, D_IN_PROJ), const),
            pl.BlockSpec((1, LANES), const),
            pl.BlockSpec((1, LANES), const),
            pl.BlockSpec((LANES, LANES), const),
            pl.BlockSpec((tm, LANES), pos),
            pl.BlockSpec((tm, LANES), pos),
        ],
        out_specs=[
            pl.BlockSpec((1, N_Q_HEADS, HEAD_DIM, tm), lambda i: (i // nlb, 0, 0, i % nlb)),
            pl.BlockSpec((1, N_KV_HEADS, tm, HEAD_DIM), heads),
            pl.BlockSpec((1, N_KV_HEADS, tm // KV_CHUNK, V_ROWS, KV_CHUNK),
                         lambda i: (i // nlb, 0, i % nlb, 0, 0)),
            pl.BlockSpec((tm, ATTN_W), row),
            pl.BlockSpec((tm, (HYENA_ORDER + 1) * HYENA_W), row),
            pl.BlockSpec((tm, HYENA_W), row),
        ],
        out_shape=[
            jax.ShapeDtypeStruct((B, N_Q_HEADS, HEAD_DIM, L), BF16),
            jax.ShapeDtypeStruct((B, N_KV_HEADS, L, HEAD_DIM), BF16),
            jax.ShapeDtypeStruct((B, N_KV_HEADS, L // KV_CHUNK, V_ROWS, KV_CHUNK), BF16),
            jax.ShapeDtypeStruct((T, ATTN_W), F32),
            jax.ShapeDtypeStruct((T, (HYENA_ORDER + 1) * HYENA_W), F32),
            jax.ShapeDtypeStruct((T, HYENA_W), F32),
        ],
        compiler_params=_cparams("parallel"),
        name="inproj",
    )(x2d, nw, w_bf, qnw, knw, pmat, ctab, stab)


def _attn_body(qt_ref, k_ref, vt_ref, o_ref, s_ref, p_ref, al_ref, cm_ref, m_ref, acc_ref, *, tq, nk):
    cols = Q_PER_KV * tq
    m_ref[...] = jnp.full(m_ref.shape, -jnp.inf, F32)
    acc_ref[...] = jnp.zeros(acc_ref.shape, F32)
    p_ref[1] = jnp.zeros(p_ref.shape[1:], BF16)
    al_ref[1] = jnp.ones(al_ref.shape[1:], F32)

    def scores(j, slot):
        start = pl.multiple_of(j * KV_CHUNK, KV_CHUNK)
        k = k_ref[0, 0, pl.ds(start, KV_CHUNK), :]
        for h in range(Q_PER_KV):
            for q0 in range(0, tq, Q_SUB):
                cs = slice(h * tq + q0, h * tq + q0 + Q_SUB)
                r = _dot(k, qt_ref[0, h, :, q0:q0 + Q_SUB])
                s_ref[slot, :, cs] = r
                cm_ref[slot, :, cs] = jnp.max(r, axis=0, keepdims=True)

    def accumulate(j, slot):
        acc_ref[...] = al_ref[slot] * acc_ref[...] + _dot(vt_ref[0, 0, j], p_ref[slot])

    def softmax(slot):
        m_old = m_ref[...]
        m_new = jnp.maximum(m_old, cm_ref[slot])
        m_ref[...] = m_new
        al_ref[slot] = jnp.exp2(m_old - m_new)
        for c in range(cols // LANES):
            cs = slice(c * LANES, (c + 1) * LANES)
            p_ref[slot, :, cs] = jnp.exp2(s_ref[slot, :, cs] - m_new[:, cs]).astype(BF16)

    scores(0, 0)

    def step(jj, carry):
        j = 2 * jj
        scores(j + 1, 1)
        accumulate(jnp.maximum(j - 1, 0), 1)
        softmax(0)
        scores(jnp.minimum(j + 2, nk - 1), 0)
        accumulate(j, 0)
        softmax(1)
        return carry

    lax.fori_loop(0, nk // 2, step, 0)
    accumulate(nk - 1, 1)
    o = acc_ref[0:HEAD_DIM, :] / acc_ref[HEAD_DIM:HEAD_DIM + 1, :]
    ot = jnp.concatenate([o[:, h * tq:(h + 1) * tq] for h in range(Q_PER_KV)], axis=0)
    o_ref[0] = ot.T


def _attention(qt, k, vt, B, L):
    tq = Q_TILE
    nk = L // KV_CHUNK
    assert nk % 2 == 0
    cols = Q_PER_KV * tq
    body = functools.partial(_attn_body, tq=tq, nk=nk)
    return pl.pallas_call(
        body,
        grid=(B, N_KV_HEADS, L // tq),
        in_specs=[
            pl.BlockSpec((1, Q_PER_KV, HEAD_DIM, tq), lambda b, g, i: (b, g, 0, i)),
            pl.BlockSpec((1, 1, L, HEAD_DIM), lambda b, g, i: (b, g, 0, 0)),
            pl.BlockSpec((1, 1, nk, V_ROWS, KV_CHUNK), lambda b, g, i: (b, g, 0, 0, 0)),
        ],
        out_specs=pl.BlockSpec((1, tq, Q_PER_KV * HEAD_DIM), lambda b, g, i: (b, i, g)),
        out_shape=jax.ShapeDtypeStruct((B, L, ATTN_W), F32),
        scratch_shapes=[
            pltpu.VMEM((2, KV_CHUNK, cols), F32),
            pltpu.VMEM((2, KV_CHUNK, cols), BF16),
            pltpu.VMEM((2, 1, cols), F32),
            pltpu.VMEM((2, 1, cols), F32),
            pltpu.VMEM((1, cols), F32),
            pltpu.VMEM((V_ROWS, cols), F32),
        ],
        compiler_params=_cparams("parallel", "parallel", "parallel"),
        name="attention",
    )(qt, k, vt)


def _sconv_body(u_ref, prev_ref, next_ref, w_ref, b_ref, o_ref, *, nblk):
    i = pl.program_id(2)
    x = u_ref[0]
    R = x.shape[0]
    halo = prev_ref.shape[1]
    prev_row = jnp.where(i == 0, 0.0, prev_ref[0, halo - 1:halo, :])
    next_row = jnp.where(i == nblk - 1, 0.0, next_ref[0, 0:1, :])
    row = lax.broadcasted_iota(jnp.int32, x.shape, 0)
    xm = jnp.where(row == 0, prev_row, pltpu.roll(x, 1, 0))
    xp = jnp.where(row == R - 1, next_row, pltpu.roll(x, R - 1, 0))
    w = w_ref[...]
    res = xm * w[0:1] + x * w[1:2] + xp * w[2:3] + b_ref[...]
    for c in range(CH_BLOCKS):
        o_ref[0, c] = res[:, c * LANES:(c + 1) * LANES]


def _short_conv(u3, cw, cb, B, L, R=2048, halo=8):
    C = u3.shape[-1]
    W = HYENA_W
    nblk = L // R
    rpb = R // halo
    body = functools.partial(_sconv_body, nblk=nblk)
    return pl.pallas_call(
        body,
        grid=(B, C // W, nblk),
        in_specs=[
            pl.BlockSpec((1, R, W), lambda b, j, i: (b, i, j)),
            pl.BlockSpec((1, halo, W), lambda b, j, i: (b, jnp.maximum(i * rpb - 1, 0), j)),
            pl.BlockSpec((1, halo, W), lambda b, j, i: (b, jnp.minimum((i + 1) * rpb, L // halo - 1), j)),
            pl.BlockSpec((3, W), lambda b, j, i: (0, j)),
            pl.BlockSpec((1, W), lambda b, j, i: (0, j)),
        ],
        out_specs=pl.BlockSpec((None, 1, CH_BLOCKS, R, LANES), lambda b, j, i: (j, b, 0, i, 0)),
        out_shape=jax.ShapeDtypeStruct((C // HYENA_W, B, CH_BLOCKS, L, LANES), F32),
        compiler_params=_cparams("parallel", "parallel", "parallel"),
        name="short_conv",
    )(u3, u3, u3, cw, cb)


def _filt_body(z_ref, valid_ref, w1_ref, b1_ref, w2_ref, b2_ref, fr_ref, w3_ref, dec_ref,
               buf_ref, sum_ref):
    i = pl.program_id(0)
    z = z_ref[...]
    fr = fr_ref[...]
    h = jnp.sin(fr * (_dot_f32(z, w1_ref[...]) + b1_ref[...]))
    h = jnp.sin(fr * (_dot_f32(h, w2_ref[...]) + b2_ref[...]))
    h = _dot_f32(h, w3_ref[0])
    t = z[:, 0:1]
    win = jnp.exp(-t * jnp.abs(dec_ref[0])) + MOD_SHIFT
    out = h * win * valid_ref[...]
    for c in range(FILT_BLOCKS):
        buf_ref[c] = out[:, c * LANES:(c + 1) * LANES]

    @pl.when(i == 0)
    def _():
        sum_ref[...] = jnp.zeros_like(sum_ref)

    sum_ref[...] += jnp.sum(jnp.abs(out), axis=0, keepdims=True)


def _filters(ztab, valid, w1, b1, w2, b2, fr, w3d, decd, L, rb=512):
    n = 2 * L
    nb = n // rb
    half = nb // 2
    W = HYENA_ORDER * HYENA_W
    const = lambda i: (0, 0)
    return pl.pallas_call(
        _filt_body,
        grid=(nb,),
        in_specs=[
            pl.BlockSpec((rb, LANES), lambda i: (i, 0)),
            pl.BlockSpec((rb, 1), lambda i: (i, 0)),
            pl.BlockSpec((LANES, LANES), const),
            pl.BlockSpec((1, LANES), const),
            pl.BlockSpec((LANES, LANES), const),
            pl.BlockSpec((1, LANES), const),
            pl.BlockSpec((1, LANES), const),
            pl.BlockSpec((1, LANES, W), lambda i: (i // half, 0, 0)),
            pl.BlockSpec((1, 1, W), lambda i: (i // half, 0, 0)),
        ],
        out_specs=[
            pl.BlockSpec((FILT_BLOCKS, rb, LANES), lambda i: (0, i, 0)),
            pl.BlockSpec((1, W), const),
        ],
        out_shape=[
            jax.ShapeDtypeStruct((FILT_BLOCKS, n, LANES), F32),
            jax.ShapeDtypeStruct((1, W), F32),
        ],
        compiler_params=_cparams("arbitrary"),
        name="filters",
    )(ztab, valid, w1, b1, w2, b2, fr, w3d, decd)


def _strided_rows(ref, lead):
    n = len(lead)
    nblk, count, stride = ref.shape[n:n + 3]
    return ref.reshape(ref.shape[:n + 1] + (count * stride, LANES)), nblk, count, stride


def _gather_rows(ref, lead, r):
    flat, nblk, count, stride = _strided_rows(ref, lead)
    return jnp.concatenate([flat[lead + (c, pl.ds(r, count, stride=stride), slice(None))]
                            for c in range(nblk)], axis=1)


def _scatter_rows(ref, lead, r, val):
    flat, nblk, count, stride = _strided_rows(ref, lead)
    for c in range(nblk):
        flat[lead + (c, pl.ds(r, count, stride=stride), slice(None))] = val[:, c * LANES:(c + 1) * LANES]


def _s1_body(a_ref, w_ref, o_ref):
    _, nb, n1h, rb, _ = a_ref.shape
    for r in range(rb):
        z = jnp.concatenate([_gather_rows(a_ref, (m,), r) for m in range(2)], axis=0)
        y = _dot(w_ref[...], z.astype(BF16))
        for c in range(nb):
            o_ref[0, c, r] = y[:, c * LANES:(c + 1) * LANES]


def _s1(src, which, w1m, P, N1):
    N1h = N1 // 2
    return pl.pallas_call(
        _s1_body,
        grid=(P, FFT_N2 // R_BLOCK),
        in_specs=[
            pl.BlockSpec((None, 2, None, CH_BLOCKS, N1h, R_BLOCK, LANES), lambda p, j: (which, 0, p, 0, 0, j, 0)),
            pl.BlockSpec((2 * N1, N1), lambda p, j: (0, 0)),
        ],
        out_specs=pl.BlockSpec((1, CH_BLOCKS, R_BLOCK, 2 * N1, LANES), lambda p, j: (p, 0, j, 0, 0)),
        out_shape=jax.ShapeDtypeStruct((P, CH_BLOCKS, FFT_N2, 2 * N1, LANES), F32),
        compiler_params=_cparams("parallel", "parallel"),
        name="dft_n1",
    )(src, w1m)


def _s1f_body(b_ref, s_ref, w_ref, o_ref):
    nb, _, rb, _ = b_ref.shape
    for r in range(rb):
        z = (_gather_rows(b_ref, (), r) / s_ref[...]).astype(BF16)
        y = _dot(w_ref[...], z)
        for c in range(nb):
            o_ref[c, r] = y[:, c * LANES:(c + 1) * LANES]


def _s1_filter(buf4, ssum, w1f, N1):
    W = FILT_BLOCKS * LANES
    return pl.pallas_call(
        _s1f_body,
        grid=(FFT_N2 // R_BLOCK,),
        in_specs=[
            pl.BlockSpec((FILT_BLOCKS, N1, R_BLOCK, LANES), lambda j: (0, 0, j, 0)),
            pl.BlockSpec((1, W), lambda j: (0, 0)),
            pl.BlockSpec((2 * N1, N1), lambda j: (0, 0)),
        ],
        out_specs=pl.BlockSpec((FILT_BLOCKS, R_BLOCK, 2 * N1, LANES), lambda j: (0, j, 0, 0)),
        out_shape=jax.ShapeDtypeStruct((FILT_BLOCKS, FFT_N2, 2 * N1, LANES), F32),
        compiler_params=_cparams("parallel"),
        name="dft_n1_filter",
    )(buf4, ssum, w1f)


def _midf_body(x_ref, g_ref, h_ref):
    rows = x_ref.shape[2]
    for k in range(rows // 2):
        z = jnp.concatenate([_gather_rows(x_ref, (), 2 * k + ri) for ri in range(2)], axis=0)
        h_ref[k] = _dot(g_ref[k], z.astype(BF16))


def _mid_filter(hpre, g, N1):
    W = FILT_BLOCKS * LANES
    kb = K_BLOCK
    return pl.pallas_call(
        _midf_body,
        grid=(N1 // kb,),
        in_specs=[
            pl.BlockSpec((FILT_BLOCKS, FFT_N2, 2 * kb, LANES), lambda i: (0, 0, i, 0)),
            pl.BlockSpec((kb, 2 * FFT_N2, 2 * FFT_N2), lambda i: (i, 0, 0)),
        ],
        out_specs=pl.BlockSpec((kb, 2 * FFT_N2, W), lambda i: (i, 0, 0)),
        out_shape=jax.ShapeDtypeStruct((N1, 2 * FFT_N2, W), F32),
        compiler_params=_cparams("parallel"),
        name="dft_n2_filter",
    )(hpre, g)


def _mid_body(x_ref, g_ref, gi_ref, h_ref, o_ref):
    rows = x_ref.shape[3]
    for k in range(rows // 2):
        z = jnp.concatenate([_gather_rows(x_ref, (0,), 2 * k + ri) for ri in range(2)], axis=0)
        d = _dot(g_ref[k], z.astype(BF16))
        hh = h_ref[k]
        dr, di = d[:FFT_N2], d[FFT_N2:]
        hr, hi = hh[:FFT_N2], hh[FFT_N2:]
        y = jnp.concatenate([dr * hr - di * hi, dr * hi + di * hr], axis=0).astype(BF16)
        e = _dot(gi_ref[k], y)
        for ro in range(2):
            _scatter_rows(o_ref, (0,), 2 * k + ro, e[ro * FFT_N2:(ro + 1) * FFT_N2])


def _mid(x5, g, gi, hspec, order, P, N1):
    kb = K_BLOCK
    blk = pl.BlockSpec((1, CH_BLOCKS, FFT_N2, 2 * kb, LANES), lambda i, p: (p, 0, 0, i, 0))
    return pl.pallas_call(
        _mid_body,
        grid=(N1 // kb, P),
        in_specs=[
            blk,
            pl.BlockSpec((kb, 2 * FFT_N2, 2 * FFT_N2), lambda i, p: (i, 0, 0)),
            pl.BlockSpec((kb, 2 * FFT_N2, 2 * FFT_N2), lambda i, p: (i, 0, 0)),
            pl.BlockSpec((kb, 2 * FFT_N2, HYENA_W), lambda i, p: (i, 0, order)),
        ],
        out_specs=blk,
        out_shape=jax.ShapeDtypeStruct((P, CH_BLOCKS, FFT_N2, 2 * N1, LANES), F32),
        compiler_params=_cparams("parallel", "parallel"),
        name="dft_n2_conv",
    )(x5, g, gi, hspec)


def _s1inv_body(e_ref, w_ref, src_ref, mul_ref, bias_ref, o_ref, y_ref):
    _, nb, n1h, rb, _ = src_ref.shape
    for r in range(rb):
        e = jnp.concatenate([e_ref[0, c, r] for c in range(nb)], axis=1).astype(BF16)
        y = _dot(w_ref[...], e)
        for ro in range(2):
            _scatter_rows(y_ref, (ro,), r, y[ro * n1h:(ro + 1) * n1h])
    o_ref[...] = mul_ref[...] * (y_ref[...] + src_ref[...] * bias_ref[...][None, :, None])


def _s1inv(e5, w1i, src, src_which, mul, mul_which, bias3, P, N1):
    N1h = N1 // 2
    member = lambda which: pl.BlockSpec((None, 2, None, CH_BLOCKS, N1h, R_BLOCK, LANES),
                                        lambda p, j: (which, 0, p, 0, 0, j, 0))
    return pl.pallas_call(
        _s1inv_body,
        grid=(P, FFT_N2 // R_BLOCK),
        in_specs=[
            pl.BlockSpec((1, CH_BLOCKS, R_BLOCK, 2 * N1, LANES), lambda p, j: (p, 0, j, 0, 0)),
            pl.BlockSpec((N1, 2 * N1), lambda p, j: (0, 0)),
            member(src_which),
            member(mul_which),
            pl.BlockSpec((CH_BLOCKS, 1, LANES), lambda p, j: (0, 0, 0)),
        ],
        out_specs=member(0),
        out_shape=jax.ShapeDtypeStruct((1, 2, P, CH_BLOCKS, N1h, FFT_N2, LANES), F32),
        scratch_shapes=[pltpu.VMEM((2, CH_BLOCKS, N1h, R_BLOCK, LANES), F32)],
        compiler_params=_cparams("parallel", "parallel"),
        name="idft_n1",
    )(e5, w1i, src, mul, bias3)


def _outproj_body(a_ref, ga_ref, z_ref, gh_ref, x_ref, wa_ref, wh_ref, w_ref, fw_ref, o_ref, *, final):
    def norm_gate(y, w, g):
        ms = jnp.mean(y * y, axis=-1, keepdims=True)
        return (y * lax.rsqrt(ms + EPS) * w) * (g * (1.0 / (1.0 + jnp.exp(-g))))

    oa = norm_gate(a_ref[...], wa_ref[...], ga_ref[...]).astype(BF16)
    zz = jnp.concatenate([z_ref[0, c] for c in range(CH_BLOCKS)], axis=1)
    oh = norm_gate(zz, wh_ref[...], gh_ref[...]).astype(BF16)
    y = x_ref[...] + (_dot(oa, w_ref[0:ATTN_W, :]) + _dot(oh, w_ref[ATTN_W:MIX_W, :]))
    if final:
        ms = jnp.mean(y * y, axis=-1, keepdims=True)
        y = y * lax.rsqrt(ms + EPS) * fw_ref[...]
    o_ref[...] = y


def _outproj(attn2d, ga, zz4, gh, x2d, wa, wh, w_bf, fw, final, tm=512):
    T = x2d.shape[0]
    nlb = zz4.shape[2] // tm
    row = lambda i: (i, 0)
    const = lambda i: (0, 0)
    body = functools.partial(_outproj_body, final=final)
    return pl.pallas_call(
        body,
        grid=(T // tm,),
        in_specs=[
            pl.BlockSpec((tm, ATTN_W), row),
            pl.BlockSpec((tm, ATTN_W), row),
            pl.BlockSpec((1, CH_BLOCKS, tm, LANES), lambda i: (i // nlb, 0, i % nlb, 0)),
            pl.BlockSpec((tm, HYENA_W), row),
            pl.BlockSpec((tm, D_MODEL), row),
            pl.BlockSpec((1, ATTN_W), const),
            pl.BlockSpec((1, HYENA_W), const),
            pl.BlockSpec((MIX_W, D_MODEL), const),
            pl.BlockSpec((1, D_MODEL), const),
        ],
        out_specs=pl.BlockSpec((tm, D_MODEL), row),
        out_shape=jax.ShapeDtypeStruct((T, D_MODEL), F32),
        compiler_params=_cparams("parallel"),
        name="outproj",
    )(attn2d, ga, zz4, gh, x2d, wa, wh, w_bf, fw)


def _rope_tables(L):
    t = jnp.arange(L, dtype=jnp.int32)
    pos = jnp.stack([t // GRID_W, t % GRID_W], axis=-1).astype(F32)
    freqs = ROPE_THETA ** (-jnp.arange(ROPE_FREQS, dtype=F32) / ROPE_FREQS)
    ang = pos[:, :, None] * freqs
    cos, sin = jnp.cos(ang), jnp.sin(ang)
    c_head = jnp.stack([cos, cos], axis=2).reshape(L, HEAD_DIM)
    s_head = jnp.stack([-sin, sin], axis=2).reshape(L, HEAD_DIM)
    reps = LANES // HEAD_DIM
    return jnp.tile(c_head, (1, reps)), jnp.tile(s_head, (1, reps))


def _head_mean_matrix():
    idx = jnp.arange(LANES) // HEAD_DIM
    return jnp.where(idx[:, None] == idx[None, :], 1.0 / HEAD_DIM, 0.0).astype(BF16)


def _dft_tables(L):
    N = 2 * L
    N1 = N // FFT_N2
    N1h = N1 // 2
    two_pi = 2.0 * math.pi
    k1 = jnp.arange(N1, dtype=jnp.int32)
    th = ((k1[:, None] * k1[None, :]) % N1).astype(F32) * (two_pi / N1)
    c, s = jnp.cos(th), jnp.sin(th)
    ch, sh = c[:, :N1h], s[:, :N1h]
    w1 = jnp.stack([jnp.concatenate([ch, sh], axis=1),
                    jnp.concatenate([-sh, ch], axis=1)], axis=1).reshape(2 * N1, N1)
    w1f = jnp.stack([c, -s], axis=1).reshape(2 * N1, N1)
    ct, st = ch.T, sh.T
    top = jnp.stack([ct, -st], axis=2).reshape(N1h, 2 * N1)
    bot = jnp.stack([st, ct], axis=2).reshape(N1h, 2 * N1)
    w1i = jnp.concatenate([top, bot], axis=0) * (1.0 / N)
    k2 = jnp.arange(FFT_N2, dtype=jnp.int32)
    freq = k1[:, None, None] + N1 * k2[None, :, None]
    ph = ((freq * k2[None, None, :]) % N).astype(F32) * (two_pi / N)
    cp, sp = jnp.cos(ph), jnp.sin(ph)
    g = jnp.concatenate([jnp.concatenate([cp, sp], axis=2),
                         jnp.concatenate([-sp, cp], axis=2)], axis=1)
    cpt, spt = jnp.swapaxes(cp, 1, 2), jnp.swapaxes(sp, 1, 2)
    gi = jnp.concatenate([jnp.concatenate([cpt, -spt], axis=2),
                          jnp.concatenate([spt, cpt], axis=2)], axis=1)
    return w1.astype(BF16), w1f.astype(BF16), w1i.astype(BF16), g.astype(BF16), gi.astype(BF16)


def _filter_positions(L):
    m = jnp.arange(2 * L, dtype=jnp.int32)
    pos = jnp.where(m < L, m, 2 * L - m)
    valid = (m != L).astype(F32)[:, None]
    pos = jnp.where(m == L, 0, pos).astype(F32)
    t = pos / (L - 1)
    w = 2.0 * math.pi * pos / L
    bands = jnp.linspace(1e-4, FILT_BANDS - 1, FILT_BANDS, dtype=F32)
    ang = w[:, None] * bands[None, :]
    z = jnp.concatenate([t[:, None], jnp.cos(ang), -jnp.sin(ang)], axis=-1)
    z = jnp.pad(z, ((0, 0), (0, LANES - FILT_EMB)))
    return z, valid


def _pad_to(a, shape):
    return jnp.pad(a, [(0, s - d) for d, s in zip(a.shape, shape)])


def _trunk(x, norm_w, w_in, q_norm_w, k_norm_w, conv_w, conv_b, filt_w1, filt_b1,
           filt_w2, filt_b2, filt_w3, filt_freq, filt_decay, hyena_bias,
           attn_out_norm_w, hyena_out_norm_w, w_out, final_norm_w):
    B, L, _ = x.shape
    assert B % 2 == 0 and L % 1024 == 0
    P = B // 2
    N1 = 2 * L // FFT_N2
    N1h = N1 // 2
    T = B * L

    ctab, stab = _rope_tables(L)
    pmat = _head_mean_matrix()
    w1m, w1f, w1i, gmat, gimat = _dft_tables(L)
    ztab, valid = _filter_positions(L)
    reps = LANES // HEAD_DIM

    x2d = x.reshape(T, D_MODEL)
    for l in range(DEPTH):
        q, k, v, ga, u, gh = _inproj(
            x2d, norm_w[l][None], w_in[l].astype(BF16),
            jnp.tile(q_norm_w[l], reps)[None], jnp.tile(k_norm_w[l], reps)[None],
            pmat, ctab, stab, B, L)
        attn = _attention(q, k, v, B, L)

        w3 = filt_w3[l].reshape(FILT_HID, HYENA_ORDER, 2, HYENA_W)
        w3d = _pad_to(jnp.transpose(w3, (2, 0, 1, 3)).reshape(2, FILT_HID, HYENA_ORDER * HYENA_W),
                      (2, LANES, HYENA_ORDER * HYENA_W))
        dec = filt_decay[l].reshape(HYENA_ORDER, 2, HYENA_W)
        decd = jnp.transpose(dec, (1, 0, 2)).reshape(2, 1, HYENA_ORDER * HYENA_W)
        buf, ssum = _filters(
            ztab, valid,
            _pad_to(filt_w1[l], (LANES, LANES)), _pad_to(filt_b1[l][None], (1, LANES)),
            _pad_to(filt_w2[l], (LANES, LANES)), _pad_to(filt_b2[l][None], (1, LANES)),
            _pad_to(filt_freq[l][None], (1, LANES)), w3d, decd, L)
        hpre = _s1_filter(buf.reshape(FILT_BLOCKS, N1, FFT_N2, LANES), ssum, w1f, N1)
        hspec = _mid_filter(hpre, gmat, N1)

        uc = _short_conv(u.reshape(B, L, (HYENA_ORDER + 1) * HYENA_W),
                         conv_w[l], conv_b[l][None], B, L)
        parts = uc.reshape(HYENA_ORDER + 1, 2, P, CH_BLOCKS, N1h, FFT_N2, LANES)
        zz = parts
        for o in range(HYENA_ORDER):
            o1 = _s1(zz, 0, w1m, P, N1)
            e = _mid(o1, gmat, gimat, hspec, o, P, N1)
            bias3 = hyena_bias[l, o].reshape(CH_BLOCKS, 1, LANES)
            zz = _s1inv(e, w1i, zz, 0, parts, o + 1, bias3, P, N1)
        zz4 = zz.reshape(B, CH_BLOCKS, L, LANES)

        x2d = _outproj(attn.reshape(T, ATTN_W), ga, zz4, gh, x2d,
                       attn_out_norm_w[l][None], hyena_out_norm_w[l][None],
                       w_out[l].astype(BF16), final_norm_w[None], final=(l == DEPTH - 1))
    return x2d.reshape(B, L, D_MODEL)


def kernel(x_prompt, x_sample, norm_w, w_in, q_norm_w, k_norm_w, conv_w, conv_b, filt_w1, filt_b1, filt_w2, filt_b2, filt_w3, filt_freq, filt_decay, hyena_bias, attn_out_norm_w, hyena_out_norm_w, w_out, final_norm_w):
    weights = (norm_w, w_in, q_norm_w, k_norm_w, conv_w, conv_b, filt_w1, filt_b1, filt_w2, filt_b2,
               filt_w3, filt_freq, filt_decay, hyena_bias, attn_out_norm_w, hyena_out_norm_w, w_out,
               final_norm_w)
    return (_trunk(x_prompt, *weights), _trunk(x_sample, *weights))
```

```python
import functools
import math

import jax
import jax.numpy as jnp
from jax import lax
from jax.experimental import pallas as pl
from jax.experimental.pallas import tpu as pltpu

F32 = jnp.float32
BF16 = jnp.bfloat16
PACKED = jnp.uint32

D_MODEL = 1024
DEPTH = 4
GRID_W = 64
HEAD_DIM = 64
N_Q_HEADS = 8
N_KV_HEADS = 2
Q_PER_KV = N_Q_HEADS // N_KV_HEADS
ATTN_W = N_Q_HEADS * HEAD_DIM
KV_W = N_KV_HEADS * HEAD_DIM
HYENA_W = 512
HYENA_ORDER = 2
MIX_W = ATTN_W + HYENA_W
FILT_EMB = 33
FILT_BANDS = 16
FILT_HID = 64
N_FILT = HYENA_ORDER * 2 * HYENA_W
MOD_SHIFT = 0.05
ROPE_THETA = 10000.0
ROPE_FREQS = HEAD_DIM // 4
EPS = 1e-6
COL_K = ATTN_W
COL_V = COL_K + KV_W
COL_GA = COL_V + KV_W
COL_U = COL_GA + ATTN_W
COL_GH = COL_U + (HYENA_ORDER + 1) * HYENA_W
D_IN_PROJ = COL_GH + HYENA_W

QK_SCALE = math.log2(math.e) / math.sqrt(HEAD_DIM)
KV_CHUNK = 512
Q_TILE = 512
Q_SUB = 256
V_ROWS = HEAD_DIM + 16
LANES = 128
FFT_N2 = 128
CH_BLOCKS = HYENA_W // LANES
FILT_BLOCKS = HYENA_ORDER * CH_BLOCKS
R_BLOCK = 8
K_BLOCK = 8
VMEM_LIMIT_BYTES = 48 * 1024 * 1024
HIGHEST = lax.Precision.HIGHEST


def _cparams(*sem):
    return pltpu.CompilerParams(dimension_semantics=sem, vmem_limit_bytes=VMEM_LIMIT_BYTES)


def _dot(a, b):
    return jnp.dot(a, b, preferred_element_type=F32)


def _dot_f32(a, b):
    return jnp.dot(a, b, preferred_element_type=F32, precision=HIGHEST)


def _inproj_body(x_ref, nw_ref, w_ref, qnw_ref, knw_ref, p_ref, c_ref, s_ref,
                 q_ref, k_ref, v_ref, ga_ref, u_ref, gh_ref):
    x = x_ref[...]
    ms = jnp.mean(x * x, axis=-1, keepdims=True)
    h = (x * lax.rsqrt(ms + EPS) * nw_ref[...]).astype(BF16)
    p = p_ref[...]
    c = c_ref[...]
    s = s_ref[...]
    lane = lax.broadcasted_iota(jnp.int32, c.shape, 1)
    first_half = (lane % (2 * ROPE_FREQS)) < ROPE_FREQS

    def norm_rope(y, nw):
        y2 = y * y
        hi = y2.astype(BF16)
        lo = (y2 - hi.astype(F32)).astype(BF16)
        msq = _dot(hi, p) + _dot(lo, p)
        yn = y * lax.rsqrt(msq + EPS) * nw
        partner = jnp.where(first_half, pltpu.roll(yn, LANES - ROPE_FREQS, 1),
                            pltpu.roll(yn, ROPE_FREQS, 1))
        return yn * c + partner * s

    q = _dot(h, w_ref[:, 0:COL_K])
    qnw = qnw_ref[...]
    for j in range(ATTN_W // LANES):
        r = norm_rope(q[:, j * LANES:(j + 1) * LANES], qnw) * QK_SCALE
        rt = r.T
        q_ref[0, 2 * j] = rt[:HEAD_DIM].astype(BF16)
        q_ref[0, 2 * j + 1] = rt[HEAD_DIM:].astype(BF16)
    kv = _dot(h, w_ref[:, COL_K:COL_GA])
    kr = norm_rope(kv[:, :KV_W], knw_ref[...])
    k_ref[0, 0] = kr[:, :HEAD_DIM].astype(BF16)
    k_ref[0, 1] = pltpu.roll(kr, HEAD_DIM, 1)[:, :HEAD_DIM].astype(BF16)
    vt = kv[:, KV_W:].T
    extra = lax.broadcasted_iota(jnp.int32, (V_ROWS - HEAD_DIM, KV_CHUNK), 0)
    ones_row = jnp.where(extra == 0, 1.0, 0.0).astype(BF16)
    for c in range(vt.shape[1] // KV_CHUNK):
        for g in range(N_KV_HEADS):
            v_ref[0, g, c, 0:HEAD_DIM, :] = vt[g * HEAD_DIM:(g + 1) * HEAD_DIM,
                                               c * KV_CHUNK:(c + 1) * KV_CHUNK].astype(BF16)
            v_ref[0, g, c, HEAD_DIM:V_ROWS, :] = ones_row
    ga_ref[...] = _dot(h, w_ref[:, COL_GA:COL_U])
    u_ref[...] = _dot(h, w_ref[:, COL_U:COL_GH])
    gh_ref[...] = _dot(h, w_ref[:, COL_GH:D_IN_PROJ])


def _inproj(x2d, nw, w_bf, qnw, knw, pmat, ctab, stab, B, L, tm=512):
    T = B * L
    nlb = L // tm
    row = lambda i: (i, 0)
    const = lambda i: (0, 0)
    pos = lambda i: (i % nlb, 0)
    heads = lambda i: (i // nlb, 0, i % nlb, 0)
    return pl.pallas_call(
        _inproj_body,
        grid=(T // tm,),
        in_specs=[
            pl.BlockSpec((tm, D_MODEL), row),
            pl.BlockSpec((1, D_MODEL), const),
            pl.BlockSpec((D_MODEL, D_IN_PROJ), const),
            pl.BlockSpec((1, LANES), const),
            pl.BlockSpec((1, LANES), const),
            pl.BlockSpec((LANES, LANES), const),
            pl.BlockSpec((tm, LANES), pos),
            pl.BlockSpec((tm, LANES), pos),
        ],
        out_specs=[
            pl.BlockSpec((1, N_Q_HEADS, HEAD_DIM, tm), lambda i: (i // nlb, 0, 0, i % nlb)),
            pl.BlockSpec((1, N_KV_HEADS, tm, HEAD_DIM), heads),
            pl.BlockSpec((1, N_KV_HEADS, tm // KV_CHUNK, V_ROWS, KV_CHUNK),
                         lambda i: (i // nlb, 0, i % nlb, 0, 0)),
            pl.BlockSpec((tm, ATTN_W), row),
            pl.BlockSpec((tm, (HYENA_ORDER + 1) * HYENA_W), row),
            pl.BlockSpec((tm, HYENA_W), row),
        ],
        out_shape=[
            jax.ShapeDtypeStruct((B, N_Q_HEADS, HEAD_DIM, L), BF16),
            jax.ShapeDtypeStruct((B, N_KV_HEADS, L, HEAD_DIM), BF16),
            jax.ShapeDtypeStruct((B, N_KV_HEADS, L // KV_CHUNK, V_ROWS, KV_CHUNK), BF16),
            jax.ShapeDtypeStruct((T, ATTN_W), F32),
            jax.ShapeDtypeStruct((T, (HYENA_ORDER + 1) * HYENA_W), F32),
            jax.ShapeDtypeStruct((T, HYENA_W), F32),
        ],
        compiler_params=_cparams("parallel"),
        name="inproj",
    )(x2d, nw, w_bf, qnw, knw, pmat, ctab, stab)


def _attn_body(qt_ref, k_ref, vt_ref, o_ref, s_ref, p_ref, al_ref, cm_ref, m_ref, acc_ref, *, tq, nk):
    cols = Q_PER_KV * tq
    m_ref[...] = jnp.full(m_ref.shape, -jnp.inf, F32)
    acc_ref[...] = jnp.zeros(acc_ref.shape, F32)
    p_ref[1] = jnp.zeros(p_ref.shape[1:], BF16)
    al_ref[1] = jnp.ones(al_ref.shape[1:], F32)

    def scores(j, slot):
        start = pl.multiple_of(j * KV_CHUNK, KV_CHUNK)
        k = k_ref[0, 0, pl.ds(start, KV_CHUNK), :]
        for h in range(Q_PER_KV):
            for q0 in range(0, tq, Q_SUB):
                cs = slice(h * tq + q0, h * tq + q0 + Q_SUB)
                r = _dot(k, qt_ref[0, h, :, q0:q0 + Q_SUB])
                s_ref[slot, :, cs] = r
                cm_ref[slot, :, cs] = jnp.max(r, axis=0, keepdims=True)

    def accumulate(j, slot):
        acc_ref[...] = al_ref[slot] * acc_ref[...] + _dot(vt_ref[0, 0, j], p_ref[slot])

    def softmax(slot):
        m_old = m_ref[...]
        m_new = jnp.maximum(m_old, cm_ref[slot])
        m_ref[...] = m_new
        al_ref[slot] = jnp.exp2(m_old - m_new)
        for c in range(cols // LANES):
            cs = slice(c * LANES, (c + 1) * LANES)
            p_ref[slot, :, cs] = jnp.exp2(s_ref[slot, :, cs] - m_new[:, cs]).astype(BF16)

    scores(0, 0)

    def step(jj, carry):
        j = 2 * jj
        scores(j + 1, 1)
        accumulate(jnp.maximum(j - 1, 0), 1)
        softmax(0)
        scores(jnp.minimum(j + 2, nk - 1), 0)
        accumulate(j, 0)
        softmax(1)
        return carry

    lax.fori_loop(0, nk // 2, step, 0)
    accumulate(nk - 1, 1)
    o = acc_ref[0:HEAD_DIM, :] / acc_ref[HEAD_DIM:HEAD_DIM + 1, :]
    ot = jnp.concatenate([o[:, h * tq:(h + 1) * tq] for h in range(Q_PER_KV)], axis=0)
    o_ref[0] = ot.T


def _attention(qt, k, vt, B, L):
    tq = Q_TILE
    nk = L // KV_CHUNK
    assert nk % 2 == 0
    cols = Q_PER_KV * tq
    body = functools.partial(_attn_body, tq=tq, nk=nk)
    return pl.pallas_call(
        body,
        grid=(B, N_KV_HEADS, L // tq),
        in_specs=[
            pl.BlockSpec((1, Q_PER_KV, HEAD_DIM, tq), lambda b, g, i: (b, g, 0, i)),
            pl.BlockSpec((1, 1, L, HEAD_DIM), lambda b, g, i: (b, g, 0, 0)),
            pl.BlockSpec((1, 1, nk, V_ROWS, KV_CHUNK), lambda b, g, i: (b, g, 0, 0, 0)),
        ],
        out_specs=pl.BlockSpec((1, tq, Q_PER_KV * HEAD_DIM), lambda b, g, i: (b, i, g)),
        out_shape=jax.ShapeDtypeStruct((B, L, ATTN_W), F32),
        scratch_shapes=[
            pltpu.VMEM((2, KV_CHUNK, cols), F32),
            pltpu.VMEM((2, KV_CHUNK, cols), BF16),
            pltpu.VMEM((2, 1, cols), F32),
            pltpu.VMEM((2, 1, cols), F32),
            pltpu.VMEM((1, cols), F32),
            pltpu.VMEM((V_ROWS, cols), F32),
        ],
        compiler_params=_cparams("parallel", "parallel", "parallel"),
        name="attention",
    )(qt, k, vt)


def _sconv_body(u_ref, prev_ref, next_ref, w_ref, b_ref, o_ref, *, nblk):
    i = pl.program_id(2)
    x = u_ref[0]
    R = x.shape[0]
    halo = prev_ref.shape[1]
    prev_row = jnp.where(i == 0, 0.0, prev_ref[0, halo - 1:halo, :])
    next_row = jnp.where(i == nblk - 1, 0.0, next_ref[0, 0:1, :])
    row = lax.broadcasted_iota(jnp.int32, x.shape, 0)
    xm = jnp.where(row == 0, prev_row, pltpu.roll(x, 1, 0))
    xp = jnp.where(row == R - 1, next_row, pltpu.roll(x, R - 1, 0))
    w = w_ref[...]
    res = xm * w[0:1] + x * w[1:2] + xp * w[2:3] + b_ref[...]
    for c in range(CH_BLOCKS):
        o_ref[0, c] = res[:, c * LANES:(c + 1) * LANES]


def _short_conv(u3, cw, cb, B, L, R=2048, halo=8):
    C = u3.shape[-1]
    W = HYENA_W
    nblk = L // R
    rpb = R // halo
    body = functools.partial(_sconv_body, nblk=nblk)
    return pl.pallas_call(
        body,
        grid=(B, C // W, nblk),
        in_specs=[
            pl.BlockSpec((1, R, W), lambda b, j, i: (b, i, j)),
            pl.BlockSpec((1, halo, W), lambda b, j, i: (b, jnp.maximum(i * rpb - 1, 0), j)),
            pl.BlockSpec((1, halo, W), lambda b, j, i: (b, jnp.minimum((i + 1) * rpb, L // halo - 1), j)),
            pl.BlockSpec((3, W), lambda b, j, i: (0, j)),
            pl.BlockSpec((1, W), lambda b, j, i: (0, j)),
        ],
        out_specs=pl.BlockSpec((None, 1, CH_BLOCKS, R, LANES), lambda b, j, i: (j, b, 0, i, 0)),
        out_shape=jax.ShapeDtypeStruct((C // HYENA_W, B, CH_BLOCKS, L, LANES), F32),
        compiler_params=_cparams("parallel", "parallel", "parallel"),
        name="short_conv",
    )(u3, u3, u3, cw, cb)


def _filt_body(z_ref, valid_ref, w1_ref, b1_ref, w2_ref, b2_ref, fr_ref, w3_ref, dec_ref,
               buf_ref, sum_ref):
    i = pl.program_id(0)
    z = z_ref[...]
    fr = fr_ref[...]
    h = jnp.sin(fr * (_dot_f32(z, w1_ref[...]) + b1_ref[...]))
    h = jnp.sin(fr * (_dot_f32(h, w2_ref[...]) + b2_ref[...]))
    h = _dot_f32(h, w3_ref[0])
    t = z[:, 0:1]
    win = jnp.exp(-t * jnp.abs(dec_ref[0])) + MOD_SHIFT
    out = h * win * valid_ref[...]
    for c in range(FILT_BLOCKS):
        buf_ref[c] = out[:, c * LANES:(c + 1) * LANES]

    @pl.when(i == 0)
    def _():
        sum_ref[...] = jnp.zeros_like(sum_ref)

    sum_ref[...] += jnp.sum(jnp.abs(out), axis=0, keepdims=True)


def _filters(ztab, valid, w1, b1, w2, b2, fr, w3d, decd, L, rb=512):
    n = 2 * L
    nb = n // rb
    half = nb // 2
    W = HYENA_ORDER * HYENA_W
    const = lambda i: (0, 0)
    return pl.pallas_call(
        _filt_body,
        grid=(nb,),
        in_specs=[
            pl.BlockSpec((rb, LANES), lambda i: (i, 0)),
            pl.BlockSpec((rb, 1), lambda i: (i, 0)),
            pl.BlockSpec((LANES, LANES), const),
            pl.BlockSpec((1, LANES), const),
            pl.BlockSpec((LANES, LANES), const),
            pl.BlockSpec((1, LANES), const),
            pl.BlockSpec((1, LANES), const),
            pl.BlockSpec((1, LANES, W), lambda i: (i // half, 0, 0)),
            pl.BlockSpec((1, 1, W), lambda i: (i // half, 0, 0)),
        ],
        out_specs=[
            pl.BlockSpec((FILT_BLOCKS, rb, LANES), lambda i: (0, i, 0)),
            pl.BlockSpec((1, W), const),
        ],
        out_shape=[
            jax.ShapeDtypeStruct((FILT_BLOCKS, n, LANES), F32),
            jax.ShapeDtypeStruct((1, W), F32),
        ],
        compiler_params=_cparams("arbitrary"),
        name="filters",
    )(ztab, valid, w1, b1, w2, b2, fr, w3d, decd)


def _strided_rows(ref, lead):
    n = len(lead)
    nblk, count, stride = ref.shape[n:n + 3]
    return ref.reshape(ref.shape[:n + 1] + (count * stride, LANES)), nblk, count, stride


def _gather_rows(ref, lead, r):
    flat, nblk, count, stride = _strided_rows(ref, lead)
    return jnp.concatenate([flat[lead + (c, pl.ds(r, count, stride=stride), slice(None))]
                            for c in range(nblk)], axis=1)


def _scatter_rows(ref, lead, r, val):
    flat, nblk, count, stride = _strided_rows(ref, lead)
    for c in range(nblk):
        flat[lead + (c, pl.ds(r, count, stride=stride), slice(None))] = val[:, c * LANES:(c + 1) * LANES]


def _pack_pairs(y):
    return pltpu.bitcast(y.astype(BF16), PACKED)


def _unpack_pairs(w):
    return pltpu.bitcast(w, BF16)


def _s1_body(a_ref, w_ref, o_ref):
    _, nb, n1h, rb, _ = a_ref.shape
    for r in range(rb):
        z = jnp.concatenate([_gather_rows(a_ref, (m,), r) for m in range(2)], axis=0)
        y = _pack_pairs(_dot(w_ref[...], z.astype(BF16)))
        for c in range(nb):
            o_ref[0, c, r] = y[:, c * LANES:(c + 1) * LANES]


def _s1(src, which, w1m, P, N1):
    N1h = N1 // 2
    return pl.pallas_call(
        _s1_body,
        grid=(P, FFT_N2 // R_BLOCK),
        in_specs=[
            pl.BlockSpec((None, 2, None, CH_BLOCKS, N1h, R_BLOCK, LANES), lambda p, j: (which, 0, p, 0, 0, j, 0)),
            pl.BlockSpec((2 * N1, N1), lambda p, j: (0, 0)),
        ],
        out_specs=pl.BlockSpec((1, CH_BLOCKS, R_BLOCK, N1, LANES), lambda p, j: (p, 0, j, 0, 0)),
        out_shape=jax.ShapeDtypeStruct((P, CH_BLOCKS, FFT_N2, N1, LANES), PACKED),
        compiler_params=_cparams("parallel", "parallel"),
        name="dft_n1",
    )(src, w1m)


def _s1f_body(b_ref, s_ref, w_ref, o_ref):
    nb, _, rb, _ = b_ref.shape
    for r in range(rb):
        z = (_gather_rows(b_ref, (), r) / s_ref[...]).astype(BF16)
        y = _pack_pairs(_dot(w_ref[...], z))
        for c in range(nb):
            o_ref[c, r] = y[:, c * LANES:(c + 1) * LANES]


def _s1_filter(buf4, ssum, w1f, N1):
    W = FILT_BLOCKS * LANES
    return pl.pallas_call(
        _s1f_body,
        grid=(FFT_N2 // R_BLOCK,),
        in_specs=[
            pl.BlockSpec((FILT_BLOCKS, N1, R_BLOCK, LANES), lambda j: (0, 0, j, 0)),
            pl.BlockSpec((1, W), lambda j: (0, 0)),
            pl.BlockSpec((2 * N1, N1), lambda j: (0, 0)),
        ],
        out_specs=pl.BlockSpec((FILT_BLOCKS, R_BLOCK, N1, LANES), lambda j: (0, j, 0, 0)),
        out_shape=jax.ShapeDtypeStruct((FILT_BLOCKS, FFT_N2, N1, LANES), PACKED),
        compiler_params=_cparams("parallel"),
        name="dft_n1_filter",
    )(buf4, ssum, w1f)


def _midf_body(x_ref, g_ref, h_ref):
    for k in range(x_ref.shape[2]):
        z = _unpack_pairs(_gather_rows(x_ref, (), k))
        h_ref[k] = _dot(g_ref[k], z)


def _mid_filter(hpre, g, N1):
    W = FILT_BLOCKS * LANES
    kb = K_BLOCK
    return pl.pallas_call(
        _midf_body,
        grid=(N1 // kb,),
        in_specs=[
            pl.BlockSpec((FILT_BLOCKS, FFT_N2, kb, LANES), lambda i: (0, 0, i, 0)),
            pl.BlockSpec((kb, 2 * FFT_N2, 2 * FFT_N2), lambda i: (i, 0, 0)),
        ],
        out_specs=pl.BlockSpec((kb, 2 * FFT_N2, W), lambda i: (i, 0, 0)),
        out_shape=jax.ShapeDtypeStruct((N1, 2 * FFT_N2, W), F32),
        compiler_params=_cparams("parallel"),
        name="dft_n2_filter",
    )(hpre, g)


def _mid_body(x_ref, g_ref, gi_ref, h_ref, o_ref):
    for k in range(x_ref.shape[3]):
        z = _unpack_pairs(_gather_rows(x_ref, (0,), k))
        d = _dot(g_ref[k], z)
        hh = h_ref[k]
        dr, di = d[:FFT_N2], d[FFT_N2:]
        hr, hi = hh[:FFT_N2], hh[FFT_N2:]
        y = jnp.concatenate([dr * hr - di * hi, dr * hi + di * hr], axis=0).astype(BF16)
        e = _dot(gi_ref[k], y)
        _scatter_rows(o_ref, (0,), k, _pack_pairs(e))


def _mid(x5, g, gi, hspec, order, P, N1):
    kb = K_BLOCK
    blk = pl.BlockSpec((1, CH_BLOCKS, FFT_N2, kb, LANES), lambda i, p: (p, 0, 0, i, 0))
    return pl.pallas_call(
        _mid_body,
        grid=(N1 // kb, P),
        in_specs=[
            blk,
            pl.BlockSpec((kb, 2 * FFT_N2, 2 * FFT_N2), lambda i, p: (i, 0, 0)),
            pl.BlockSpec((kb, 2 * FFT_N2, 2 * FFT_N2), lambda i, p: (i, 0, 0)),
            pl.BlockSpec((kb, 2 * FFT_N2, HYENA_W), lambda i, p: (i, 0, order)),
        ],
        out_specs=blk,
        out_shape=jax.ShapeDtypeStruct((P, CH_BLOCKS, FFT_N2, N1, LANES), PACKED),
        compiler_params=_cparams("parallel", "parallel"),
        name="dft_n2_conv",
    )(x5, g, gi, hspec)


def _s1inv_body(e_ref, w_ref, src_ref, mul_ref, bias_ref, o_ref, y_ref):
    _, nb, n1h, rb, _ = src_ref.shape
    for r in range(rb):
        e = _unpack_pairs(jnp.concatenate([e_ref[0, c, r] for c in range(nb)], axis=1))
        y = _dot(w_ref[...], e)
        for ro in range(2):
            _scatter_rows(y_ref, (ro,), r, y[ro * n1h:(ro + 1) * n1h])
    o_ref[...] = mul_ref[...] * (y_ref[...] + src_ref[...] * bias_ref[...][None, :, None])


def _s1inv(e5, w1i, src, src_which, mul, mul_which, bias3, P, N1):
    N1h = N1 // 2
    member = lambda which: pl.BlockSpec((None, 2, None, CH_BLOCKS, N1h, R_BLOCK, LANES),
                                        lambda p, j: (which, 0, p, 0, 0, j, 0))
    return pl.pallas_call(
        _s1inv_body,
        grid=(P, FFT_N2 // R_BLOCK),
        in_specs=[
            pl.BlockSpec((1, CH_BLOCKS, R_BLOCK, N1, LANES), lambda p, j: (p, 0, j, 0, 0)),
            pl.BlockSpec((N1, 2 * N1), lambda p, j: (0, 0)),
            member(src_which),
            member(mul_which),
            pl.BlockSpec((CH_BLOCKS, 1, LANES), lambda p, j: (0, 0, 0)),
        ],
        out_specs=member(0),
        out_shape=jax.ShapeDtypeStruct((1, 2, P, CH_BLOCKS, N1h, FFT_N2, LANES), F32),
        scratch_shapes=[pltpu.VMEM((2, CH_BLOCKS, N1h, R_BLOCK, LANES), F32)],
        compiler_params=_cparams("parallel", "parallel"),
        name="idft_n1",
    )(e5, w1i, src, mul, bias3)


def _outproj_body(a_ref, ga_ref, z_ref, gh_ref, x_ref, wa_ref, wh_ref, w_ref, fw_ref, o_ref, *, final):
    def norm_gate(y, w, g):
        ms = jnp.mean(y * y, axis=-1, keepdims=True)
        return (y * lax.rsqrt(ms + EPS) * w) * (g * (1.0 / (1.0 + jnp.exp(-g))))

    oa = norm_gate(a_ref[...], wa_ref[...], ga_ref[...]).astype(BF16)
    zz = jnp.concatenate([z_ref[0, c] for c in range(CH_BLOCKS)], axis=1)
    oh = norm_gate(zz, wh_ref[...], gh_ref[...]).astype(BF16)
    y = x_ref[...] + (_dot(oa, w_ref[0:ATTN_W, :]) + _dot(oh, w_ref[ATTN_W:MIX_W, :]))
    if final:
        ms = jnp.mean(y * y, axis=-1, keepdims=True)
        y = y * lax.rsqrt(ms + EPS) * fw_ref[...]
    o_ref[...] = y


def _outproj(attn2d, ga, zz4, gh, x2d, wa, wh, w_bf, fw, final, tm=512):
    T = x2d.shape[0]
    nlb = zz4.shape[2] // tm
    row = lambda i: (i, 0)
    const = lambda i: (0, 0)
    body = functools.partial(_outproj_body, final=final)
    return pl.pallas_call(
        body,
        grid=(T // tm,),
        in_specs=[
            pl.BlockSpec((tm, ATTN_W), row),
            pl.BlockSpec((tm, ATTN_W), row),
            pl.BlockSpec((1, CH_BLOCKS, tm, LANES), lambda i: (i // nlb, 0, i % nlb, 0)),
            pl.BlockSpec((tm, HYENA_W), row),
            pl.BlockSpec((tm, D_MODEL), row),
            pl.BlockSpec((1, ATTN_W), const),
            pl.BlockSpec((1, HYENA_W), const),
            pl.BlockSpec((MIX_W, D_MODEL), const),
            pl.BlockSpec((1, D_MODEL), const),
        ],
        out_specs=pl.BlockSpec((tm, D_MODEL), row),
        out_shape=jax.ShapeDtypeStruct((T, D_MODEL), F32),
        compiler_params=_cparams("parallel"),
        name="outproj",
    )(attn2d, ga, zz4, gh, x2d, wa, wh, w_bf, fw)


def _rope_tables(L):
    t = jnp.arange(L, dtype=jnp.int32)
    pos = jnp.stack([t // GRID_W, t % GRID_W], axis=-1).astype(F32)
    freqs = ROPE_THETA ** (-jnp.arange(ROPE_FREQS, dtype=F32) / ROPE_FREQS)
    ang = pos[:, :, None] * freqs
    cos, sin = jnp.cos(ang), jnp.sin(ang)
    c_head = jnp.stack([cos, cos], axis=2).reshape(L, HEAD_DIM)
    s_head = jnp.stack([-sin, sin], axis=2).reshape(L, HEAD_DIM)
    reps = LANES // HEAD_DIM
    return jnp.tile(c_head, (1, reps)), jnp.tile(s_head, (1, reps))


def _head_mean_matrix():
    idx = jnp.arange(LANES) // HEAD_DIM
    return jnp.where(idx[:, None] == idx[None, :], 1.0 / HEAD_DIM, 0.0).astype(BF16)


def _dft_tables(L):
    N = 2 * L
    N1 = N // FFT_N2
    N1h = N1 // 2
    two_pi = 2.0 * math.pi
    k1 = jnp.arange(N1, dtype=jnp.int32)
    th = ((k1[:, None] * k1[None, :]) % N1).astype(F32) * (two_pi / N1)
    c, s = jnp.cos(th), jnp.sin(th)
    ch, sh = c[:, :N1h], s[:, :N1h]
    w1 = jnp.stack([jnp.concatenate([ch, sh], axis=1),
                    jnp.concatenate([-sh, ch], axis=1)], axis=1).reshape(2 * N1, N1)
    w1f = jnp.stack([c, -s], axis=1).reshape(2 * N1, N1)
    ct, st = ch.T, sh.T
    top = jnp.stack([ct, -st], axis=2).reshape(N1h, 2 * N1)
    bot = jnp.stack([st, ct], axis=2).reshape(N1h, 2 * N1)
    w1i = jnp.concatenate([top, bot], axis=0) * (1.0 / N)
    k2 = jnp.arange(FFT_N2, dtype=jnp.int32)
    freq = k1[:, None, None] + N1 * k2[None, :, None]
    ph = ((freq * k2[None, None, :]) % N).astype(F32) * (two_pi / N)
    cp, sp = jnp.cos(ph), jnp.sin(ph)
    g = jnp.concatenate([jnp.stack([cp, sp], axis=3).reshape(N1, FFT_N2, 2 * FFT_N2),
                         jnp.stack([-sp, cp], axis=3).reshape(N1, FFT_N2, 2 * FFT_N2)], axis=1)
    cpt, spt = jnp.swapaxes(cp, 1, 2), jnp.swapaxes(sp, 1, 2)
    gi = jnp.stack([jnp.concatenate([cpt, -spt], axis=2),
                    jnp.concatenate([spt, cpt], axis=2)], axis=2).reshape(N1, 2 * FFT_N2, 2 * FFT_N2)
    return w1.astype(BF16), w1f.astype(BF16), w1i.astype(BF16), g.astype(BF16), gi.astype(BF16)


def _filter_positions(L):
    m = jnp.arange(2 * L, dtype=jnp.int32)
    pos = jnp.where(m < L, m, 2 * L - m)
    valid = (m != L).astype(F32)[:, None]
    pos = jnp.where(m == L, 0, pos).astype(F32)
    t = pos / (L - 1)
    w = 2.0 * math.pi * pos / L
    bands = jnp.linspace(1e-4, FILT_BANDS - 1, FILT_BANDS, dtype=F32)
    ang = w[:, None] * bands[None, :]
    z = jnp.concatenate([t[:, None], jnp.cos(ang), -jnp.sin(ang)], axis=-1)
    z = jnp.pad(z, ((0, 0), (0, LANES - FILT_EMB)))
    return z, valid


def _pad_to(a, shape):
    return jnp.pad(a, [(0, s - d) for d, s in zip(a.shape, shape)])


def _trunk(x, norm_w, w_in, q_norm_w, k_norm_w, conv_w, conv_b, filt_w1, filt_b1,
           filt_w2, filt_b2, filt_w3, filt_freq, filt_decay, hyena_bias,
           attn_out_norm_w, hyena_out_norm_w, w_out, final_norm_w):
    B, L, _ = x.shape
    assert B % 2 == 0 and L % 1024 == 0
    P = B // 2
    N1 = 2 * L // FFT_N2
    N1h = N1 // 2
    T = B * L

    ctab, stab = _rope_tables(L)
    pmat = _head_mean_matrix()
    w1m, w1f, w1i, gmat, gimat = _dft_tables(L)
    ztab, valid = _filter_positions(L)
    reps = LANES // HEAD_DIM

    x2d = x.reshape(T, D_MODEL)
    for l in range(DEPTH):
        q, k, v, ga, u, gh = _inproj(
            x2d, norm_w[l][None], w_in[l].astype(BF16),
            jnp.tile(q_norm_w[l], reps)[None], jnp.tile(k_norm_w[l], reps)[None],
            pmat, ctab, stab, B, L)
        attn = _attention(q, k, v, B, L)

        w3 = filt_w3[l].reshape(FILT_HID, HYENA_ORDER, 2, HYENA_W)
        w3d = _pad_to(jnp.transpose(w3, (2, 0, 1, 3)).reshape(2, FILT_HID, HYENA_ORDER * HYENA_W),
                      (2, LANES, HYENA_ORDER * HYENA_W))
        dec = filt_decay[l].reshape(HYENA_ORDER, 2, HYENA_W)
        decd = jnp.transpose(dec, (1, 0, 2)).reshape(2, 1, HYENA_ORDER * HYENA_W)
        buf, ssum = _filters(
            ztab, valid,
            _pad_to(filt_w1[l], (LANES, LANES)), _pad_to(filt_b1[l][None], (1, LANES)),
            _pad_to(filt_w2[l], (LANES, LANES)), _pad_to(filt_b2[l][None], (1, LANES)),
            _pad_to(filt_freq[l][None], (1, LANES)), w3d, decd, L)
        hpre = _s1_filter(buf.reshape(FILT_BLOCKS, N1, FFT_N2, LANES), ssum, w1f, N1)
        hspec = _mid_filter(hpre, gmat, N1)

        uc = _short_conv(u.reshape(B, L, (HYENA_ORDER + 1) * HYENA_W),
                         conv_w[l], conv_b[l][None], B, L)
        parts = uc.reshape(HYENA_ORDER + 1, 2, P, CH_BLOCKS, N1h, FFT_N2, LANES)
        zz = parts
        for o in range(HYENA_ORDER):
            o1 = _s1(zz, 0, w1m, P, N1)
            e = _mid(o1, gmat, gimat, hspec, o, P, N1)
            bias3 = hyena_bias[l, o].reshape(CH_BLOCKS, 1, LANES)
            zz = _s1inv(e, w1i, zz, 0, parts, o + 1, bias3, P, N1)
        zz4 = zz.reshape(B, CH_BLOCKS, L, LANES)

        x2d = _outproj(attn.reshape(T, ATTN_W), ga, zz4, gh, x2d,
                       attn_out_norm_w[l][None], hyena_out_norm_w[l][None],
                       w_out[l].astype(BF16), final_norm_w[None], final=(l == DEPTH - 1))
    return x2d.reshape(B, L, D_MODEL)


def kernel(x_prompt, x_sample, norm_w, w_in, q_norm_w, k_norm_w, conv_w, conv_b, filt_w1, filt_b1, filt_w2, filt_b2, filt_w3, filt_freq, filt_decay, hyena_bias, attn_out_norm_w, hyena_out_norm_w, w_out, final_norm_w):
    weights = (norm_w, w_in, q_norm_w, k_norm_w, conv_w, conv_b, filt_w1, filt_b1, filt_w2, filt_b2,
               filt_w3, filt_freq, filt_decay, hyena_bias, attn_out_norm_w, hyena_out_norm_w, w_out,
               final_norm_w)
    return (_trunk(x_prompt, *weights), _trunk(x_sample, *weights))
```

```python
import functools
import math

import jax
import jax.numpy as jnp
from jax import lax
from jax.experimental import pallas as pl
from jax.experimental.pallas import tpu as pltpu

F32 = jnp.float32
BF16 = jnp.bfloat16
PACKED = jnp.uint32

D_MODEL = 1024
DEPTH = 4
GRID_W = 64
HEAD_DIM = 64
N_Q_HEADS = 8
N_KV_HEADS = 2
Q_PER_KV = N_Q_HEADS // N_KV_HEADS
ATTN_W = N_Q_HEADS * HEAD_DIM
KV_W = N_KV_HEADS * HEAD_DIM
HYENA_W = 512
HYENA_ORDER = 2
MIX_W = ATTN_W + HYENA_W
SHORT_CONV = 3
FILT_EMB = 33
FILT_BANDS = 16
FILT_HID = 64
N_FILT = HYENA_ORDER * 2 * HYENA_W
MOD_SHIFT = 0.05
ROPE_THETA = 10000.0
ROPE_FREQS = HEAD_DIM // 4
EPS = 1e-6
COL_K = ATTN_W
COL_V = COL_K + KV_W
COL_GA = COL_V + KV_W
COL_U = COL_GA + ATTN_W
COL_GH = COL_U + (HYENA_ORDER + 1) * HYENA_W
D_IN_PROJ = COL_GH + HYENA_W

QK_SCALE = math.log2(math.e) / math.sqrt(HEAD_DIM)
KV_CHUNK = 512
Q_TILE = 512
Q_SUB = 256
V_ROWS = HEAD_DIM + 16
LANES = 128
FFT_N2 = 128
CH_BLOCKS = HYENA_W // LANES
FILT_BLOCKS = HYENA_ORDER * CH_BLOCKS
R_BLOCK = 8
K_BLOCK = 8
VMEM_LIMIT_BYTES = 48 * 1024 * 1024
HIGHEST = lax.Precision.HIGHEST


def _cparams(*sem):
    return pltpu.CompilerParams(dimension_semantics=sem, vmem_limit_bytes=VMEM_LIMIT_BYTES)


def _dot(a, b):
    return jnp.dot(a, b, preferred_element_type=F32)


def _dot_f32(a, b):
    return jnp.dot(a, b, preferred_element_type=F32, precision=HIGHEST)


def _inproj_body(x_ref, xprev_ref, xnext_ref, nw_ref, w_ref, qnw_ref, knw_ref, p_ref, c_ref, s_ref,
                 cw_ref, cb_ref, q_ref, k_ref, v_ref, ga_ref, parts_ref, gh_ref, *, nlb):
    def normed(x):
        ms = jnp.mean(x * x, axis=-1, keepdims=True)
        return (x * lax.rsqrt(ms + EPS) * nw_ref[...]).astype(BF16)

    h = normed(x_ref[...])
    p = p_ref[...]
    c = c_ref[...]
    s = s_ref[...]
    lane = lax.broadcasted_iota(jnp.int32, c.shape, 1)
    first_half = (lane % (2 * ROPE_FREQS)) < ROPE_FREQS

    def norm_rope(y, nw):
        y2 = y * y
        hi = y2.astype(BF16)
        lo = (y2 - hi.astype(F32)).astype(BF16)
        msq = _dot(hi, p) + _dot(lo, p)
        yn = y * lax.rsqrt(msq + EPS) * nw
        partner = jnp.where(first_half, pltpu.roll(yn, LANES - ROPE_FREQS, 1),
                            pltpu.roll(yn, ROPE_FREQS, 1))
        return yn * c + partner * s

    q = _dot(h, w_ref[:, 0:COL_K])
    qnw = qnw_ref[...]
    for j in range(ATTN_W // LANES):
        r = norm_rope(q[:, j * LANES:(j + 1) * LANES], qnw) * QK_SCALE
        rt = r.T
        q_ref[0, 2 * j] = rt[:HEAD_DIM].astype(BF16)
        q_ref[0, 2 * j + 1] = rt[HEAD_DIM:].astype(BF16)
    kv = _dot(h, w_ref[:, COL_K:COL_GA])
    kr = norm_rope(kv[:, :KV_W], knw_ref[...])
    k_ref[0, 0] = kr[:, :HEAD_DIM].astype(BF16)
    k_ref[0, 1] = pltpu.roll(kr, HEAD_DIM, 1)[:, :HEAD_DIM].astype(BF16)
    vt = kv[:, KV_W:].T
    extra = lax.broadcasted_iota(jnp.int32, (V_ROWS - HEAD_DIM, KV_CHUNK), 0)
    ones_row = jnp.where(extra == 0, 1.0, 0.0).astype(BF16)
    for c in range(vt.shape[1] // KV_CHUNK):
        for g in range(N_KV_HEADS):
            v_ref[0, g, c, 0:HEAD_DIM, :] = vt[g * HEAD_DIM:(g + 1) * HEAD_DIM,
                                               c * KV_CHUNK:(c + 1) * KV_CHUNK].astype(BF16)
            v_ref[0, g, c, HEAD_DIM:V_ROWS, :] = ones_row
    ga_ref[...] = _dot(h, w_ref[:, COL_GA:COL_U])
    gh_ref[...] = _dot(h, w_ref[:, COL_GH:D_IN_PROJ])

    u = _dot(h, w_ref[:, COL_U:COL_GH])
    halo = xprev_ref.shape[0]
    hh = normed(jnp.concatenate([xprev_ref[...], xnext_ref[...]], axis=0))
    uh = _dot(hh, w_ref[:, COL_U:COL_GH])
    li = pl.program_id(0) % nlb
    prev_row = jnp.where(li == 0, 0.0, uh[halo - 1:halo])
    next_row = jnp.where(li == nlb - 1, 0.0, uh[halo:halo + 1])
    tm = u.shape[0]
    row = lax.broadcasted_iota(jnp.int32, u.shape, 0)
    um = jnp.where(row == 0, prev_row, pltpu.roll(u, 1, 0))
    up = jnp.where(row == tm - 1, next_row, pltpu.roll(u, tm - 1, 0))
    cw = cw_ref[...]
    res = um * cw[0:1] + u * cw[1:2] + up * cw[2:3] + cb_ref[...]
    for part in range(HYENA_ORDER + 1):
        for cblk in range(CH_BLOCKS):
            lo = part * HYENA_W + cblk * LANES
            parts_ref[part, 0, cblk] = res[:, lo:lo + LANES]


def _inproj(x2d, nw, w_bf, qnw, knw, pmat, ctab, stab, cw, cb, B, L, tm=512, halo=8):
    T = B * L
    nlb = L // tm
    rpb = tm // halo
    UW = (HYENA_ORDER + 1) * HYENA_W
    row = lambda i: (i, 0)
    const = lambda i: (0, 0)
    pos = lambda i: (i % nlb, 0)
    heads = lambda i: (i // nlb, 0, i % nlb, 0)
    return pl.pallas_call(
        functools.partial(_inproj_body, nlb=nlb),
        grid=(T // tm,),
        in_specs=[
            pl.BlockSpec((tm, D_MODEL), row),
            pl.BlockSpec((halo, D_MODEL), lambda i: (jnp.maximum(i * rpb - 1, 0), 0)),
            pl.BlockSpec((halo, D_MODEL), lambda i: (jnp.minimum((i + 1) * rpb, T // halo - 1), 0)),
            pl.BlockSpec((1, D_MODEL), const),
            pl.BlockSpec((D_MODEL, D_IN_PROJ), const),
            pl.BlockSpec((1, LANES), const),
            pl.BlockSpec((1, LANES), const),
            pl.BlockSpec((LANES, LANES), const),
            pl.BlockSpec((tm, LANES), pos),
            pl.BlockSpec((tm, LANES), pos),
            pl.BlockSpec((SHORT_CONV, UW), const),
            pl.BlockSpec((1, UW), const),
        ],
        out_specs=[
            pl.BlockSpec((1, N_Q_HEADS, HEAD_DIM, tm), lambda i: (i // nlb, 0, 0, i % nlb)),
            pl.BlockSpec((1, N_KV_HEADS, tm, HEAD_DIM), heads),
            pl.BlockSpec((1, N_KV_HEADS, tm // KV_CHUNK, V_ROWS, KV_CHUNK),
                         lambda i: (i // nlb, 0, i % nlb, 0, 0)),
            pl.BlockSpec((tm, ATTN_W), row),
            pl.BlockSpec((HYENA_ORDER + 1, 1, CH_BLOCKS, tm, LANES), lambda i: (0, i // nlb, 0, i % nlb, 0)),
            pl.BlockSpec((tm, HYENA_W), row),
        ],
        out_shape=[
            jax.ShapeDtypeStruct((B, N_Q_HEADS, HEAD_DIM, L), BF16),
            jax.ShapeDtypeStruct((B, N_KV_HEADS, L, HEAD_DIM), BF16),
            jax.ShapeDtypeStruct((B, N_KV_HEADS, L // KV_CHUNK, V_ROWS, KV_CHUNK), BF16),
            jax.ShapeDtypeStruct((T, ATTN_W), F32),
            jax.ShapeDtypeStruct((HYENA_ORDER + 1, B, CH_BLOCKS, L, LANES), F32),
            jax.ShapeDtypeStruct((T, HYENA_W), F32),
        ],
        compiler_params=_cparams("parallel"),
        name="inproj",
    )(x2d, x2d, x2d, nw, w_bf, qnw, knw, pmat, ctab, stab, cw, cb)


def _attn_body(qt_ref, k_ref, vt_ref, o_ref, s_ref, p_ref, al_ref, cm_ref, m_ref, acc_ref, *, tq, nk):
    cols = Q_PER_KV * tq
    ncb = cols // LANES
    m_ref[...] = jnp.full(m_ref.shape, -jnp.inf, F32)
    acc_ref[...] = jnp.zeros(acc_ref.shape, F32)

    def scores(j, slot):
        start = pl.multiple_of(j * KV_CHUNK, KV_CHUNK)
        k = k_ref[0, 0, pl.ds(start, KV_CHUNK), :]
        for h in range(Q_PER_KV):
            for q0 in range(0, tq, Q_SUB):
                c0 = h * tq + q0
                r = _dot(k, qt_ref[0, h, :, q0:q0 + Q_SUB])
                for i in range(Q_SUB // LANES):
                    s_ref[slot, c0 // LANES + i] = r[:, i * LANES:(i + 1) * LANES]
                cm_ref[slot, :, c0:c0 + Q_SUB] = jnp.max(r, axis=0, keepdims=True)

    def accumulate(j, slot):
        vt = vt_ref[0, 0, j]
        for i in range(cols // Q_SUB):
            cs = slice(i * Q_SUB, (i + 1) * Q_SUB)
            nb = Q_SUB // LANES
            p = jnp.concatenate([p_ref[slot, nb * i + t] for t in range(nb)], axis=1)
            acc_ref[:, cs] = al_ref[slot, :, cs] * acc_ref[:, cs] + _dot(vt, p)

    def softmax(slot):
        m_old = m_ref[...]
        m_new = jnp.maximum(m_old, cm_ref[slot])
        m_ref[...] = m_new
        al_ref[slot] = jnp.exp2(m_old - m_new)
        for c in range(ncb):
            p_ref[slot, c] = jnp.exp2(s_ref[slot, c] - m_new[:, c * LANES:(c + 1) * LANES]).astype(BF16)

    scores(0, 0)
    scores(1, 1)
    softmax(0)
    scores(2, 0)
    accumulate(0, 0)
    softmax(1)

    def step(jj, carry):
        j = 2 * jj
        scores(j + 1, 1)
        accumulate(j - 1, 1)
        softmax(0)
        scores(jnp.minimum(j + 2, nk - 1), 0)
        accumulate(j, 0)
        softmax(1)
        return carry

    lax.fori_loop(1, nk // 2, step, 0)
    accumulate(nk - 1, 1)
    o = acc_ref[0:HEAD_DIM, :] / acc_ref[HEAD_DIM:HEAD_DIM + 1, :]
    for h in range(Q_PER_KV):
        o_ref[0, h] = o[:, h * tq:(h + 1) * tq]


def _attention(qt, k, vt, B, L):
    tq = Q_TILE
    nk = L // KV_CHUNK
    assert nk % 2 == 0 and nk >= 4
    cols = Q_PER_KV * tq
    ncb = cols // LANES
    body = functools.partial(_attn_body, tq=tq, nk=nk)
    return pl.pallas_call(
        body,
        grid=(B, N_KV_HEADS, L // tq),
        in_specs=[
            pl.BlockSpec((1, Q_PER_KV, HEAD_DIM, tq), lambda b, g, i: (b, g, 0, i)),
            pl.BlockSpec((1, 1, L, HEAD_DIM), lambda b, g, i: (b, g, 0, 0)),
            pl.BlockSpec((1, 1, nk, V_ROWS, KV_CHUNK), lambda b, g, i: (b, g, 0, 0, 0)),
        ],
        out_specs=pl.BlockSpec((1, Q_PER_KV, HEAD_DIM, tq), lambda b, g, i: (b, g, 0, i)),
        out_shape=jax.ShapeDtypeStruct((B, N_Q_HEADS, HEAD_DIM, L), F32),
        scratch_shapes=[
            pltpu.VMEM((2, ncb, KV_CHUNK, LANES), F32),
            pltpu.VMEM((2, ncb, KV_CHUNK, LANES), BF16),
            pltpu.VMEM((2, 1, cols), F32),
            pltpu.VMEM((2, 1, cols), F32),
            pltpu.VMEM((1, cols), F32),
            pltpu.VMEM((V_ROWS, cols), F32),
        ],
        compiler_params=_cparams("parallel", "parallel", "parallel"),
        name="attention",
    )(qt, k, vt)


def _filt_body(z_ref, valid_ref, w1_ref, b1_ref, w2_ref, b2_ref, fr_ref, w3_ref, dec_ref,
               buf_ref, sum_ref):
    i = pl.program_id(0)
    z = z_ref[...]
    fr = fr_ref[...]
    h = jnp.sin(fr * (_dot_f32(z, w1_ref[...]) + b1_ref[...]))
    h = jnp.sin(fr * (_dot_f32(h, w2_ref[...]) + b2_ref[...]))
    h = _dot_f32(h, w3_ref[0])
    t = z[:, 0:1]
    win = jnp.exp(-t * jnp.abs(dec_ref[0])) + MOD_SHIFT
    out = h * win * valid_ref[...]
    for c in range(FILT_BLOCKS):
        buf_ref[c] = out[:, c * LANES:(c + 1) * LANES]

    @pl.when(i == 0)
    def _():
        sum_ref[...] = jnp.zeros_like(sum_ref)

    sum_ref[...] += jnp.sum(jnp.abs(out), axis=0, keepdims=True)


def _filters(ztab, valid, w1, b1, w2, b2, fr, w3d, decd, L, rb=512):
    n = 2 * L
    nb = n // rb
    half = nb // 2
    W = HYENA_ORDER * HYENA_W
    const = lambda i: (0, 0)
    return pl.pallas_call(
        _filt_body,
        grid=(nb,),
        in_specs=[
            pl.BlockSpec((rb, LANES), lambda i: (i, 0)),
            pl.BlockSpec((rb, 1), lambda i: (i, 0)),
            pl.BlockSpec((LANES, LANES), const),
            pl.BlockSpec((1, LANES), const),
            pl.BlockSpec((LANES, LANES), const),
            pl.BlockSpec((1, LANES), const),
            pl.BlockSpec((1, LANES), const),
            pl.BlockSpec((1, LANES, W), lambda i: (i // half, 0, 0)),
            pl.BlockSpec((1, 1, W), lambda i: (i // half, 0, 0)),
        ],
        out_specs=[
            pl.BlockSpec((FILT_BLOCKS, rb, LANES), lambda i: (0, i, 0)),
            pl.BlockSpec((1, W), const),
        ],
        out_shape=[
            jax.ShapeDtypeStruct((FILT_BLOCKS, n, LANES), F32),
            jax.ShapeDtypeStruct((1, W), F32),
        ],
        compiler_params=_cparams("arbitrary"),
        name="filters",
    )(ztab, valid, w1, b1, w2, b2, fr, w3d, decd)


def _strided_rows(ref, lead):
    n = len(lead)
    nblk, count, stride = ref.shape[n:n + 3]
    return ref.reshape(ref.shape[:n + 1] + (count * stride, LANES)), nblk, count, stride


def _gather_rows(ref, lead, r):
    flat, nblk, count, stride = _strided_rows(ref, lead)
    return jnp.concatenate([flat[lead + (c, pl.ds(r, count, stride=stride), slice(None))]
                            for c in range(nblk)], axis=1)


def _scatter_rows(ref, lead, r, val):
    flat, nblk, count, stride = _strided_rows(ref, lead)
    for c in range(nblk):
        flat[lead + (c, pl.ds(r, count, stride=stride), slice(None))] = val[:, c * LANES:(c + 1) * LANES]


def _pack_pairs(y):
    return pltpu.bitcast(y.astype(BF16), PACKED)


def _unpack_pairs(w):
    return pltpu.bitcast(w, BF16)


def _s1_body(a_ref, w_ref, o_ref):
    _, nb, n1h, rb, _ = a_ref.shape
    for r in range(rb):
        z = jnp.concatenate([_gather_rows(a_ref, (m,), r) for m in range(2)], axis=0)
        y = _pack_pairs(_dot(w_ref[...], z.astype(BF16)))
        for c in range(nb):
            o_ref[0, c, r] = y[:, c * LANES:(c + 1) * LANES]


def _s1(src, which, w1m, P, N1):
    N1h = N1 // 2
    return pl.pallas_call(
        _s1_body,
        grid=(P, FFT_N2 // R_BLOCK),
        in_specs=[
            pl.BlockSpec((None, 2, None, CH_BLOCKS, N1h, R_BLOCK, LANES), lambda p, j: (which, 0, p, 0, 0, j, 0)),
            pl.BlockSpec((2 * N1, N1), lambda p, j: (0, 0)),
        ],
        out_specs=pl.BlockSpec((1, CH_BLOCKS, R_BLOCK, N1, LANES), lambda p, j: (p, 0, j, 0, 0)),
        out_shape=jax.ShapeDtypeStruct((P, CH_BLOCKS, FFT_N2, N1, LANES), PACKED),
        compiler_params=_cparams("parallel", "parallel"),
        name="dft_n1",
    )(src, w1m)


def _s1f_body(b_ref, s_ref, w_ref, o_ref):
    nb, _, rb, _ = b_ref.shape
    for r in range(rb):
        z = (_gather_rows(b_ref, (), r) / s_ref[...]).astype(BF16)
        y = _pack_pairs(_dot(w_ref[...], z))
        for c in range(nb):
            o_ref[c, r] = y[:, c * LANES:(c + 1) * LANES]


def _s1_filter(buf4, ssum, w1f, N1):
    W = FILT_BLOCKS * LANES
    return pl.pallas_call(
        _s1f_body,
        grid=(FFT_N2 // R_BLOCK,),
        in_specs=[
            pl.BlockSpec((FILT_BLOCKS, N1, R_BLOCK, LANES), lambda j: (0, 0, j, 0)),
            pl.BlockSpec((1, W), lambda j: (0, 0)),
            pl.BlockSpec((2 * N1, N1), lambda j: (0, 0)),
        ],
        out_specs=pl.BlockSpec((FILT_BLOCKS, R_BLOCK, N1, LANES), lambda j: (0, j, 0, 0)),
        out_shape=jax.ShapeDtypeStruct((FILT_BLOCKS, FFT_N2, N1, LANES), PACKED),
        compiler_params=_cparams("parallel"),
        name="dft_n1_filter",
    )(buf4, ssum, w1f)


def _midf_body(x_ref, g_ref, h_ref):
    for k in range(x_ref.shape[2]):
        z = _unpack_pairs(_gather_rows(x_ref, (), k))
        h_ref[k] = _dot(g_ref[k], z)


def _mid_filter(hpre, g, N1):
    W = FILT_BLOCKS * LANES
    kb = K_BLOCK
    return pl.pallas_call(
        _midf_body,
        grid=(N1 // kb,),
        in_specs=[
            pl.BlockSpec((FILT_BLOCKS, FFT_N2, kb, LANES), lambda i: (0, 0, i, 0)),
            pl.BlockSpec((kb, 2 * FFT_N2, 2 * FFT_N2), lambda i: (i, 0, 0)),
        ],
        out_specs=pl.BlockSpec((kb, 2 * FFT_N2, W), lambda i: (i, 0, 0)),
        out_shape=jax.ShapeDtypeStruct((N1, 2 * FFT_N2, W), F32),
        compiler_params=_cparams("parallel"),
        name="dft_n2_filter",
    )(hpre, g)


def _mid_body(x_ref, g_ref, gi_ref, h_ref, o_ref):
    for k in range(x_ref.shape[3]):
        z = _unpack_pairs(_gather_rows(x_ref, (0,), k))
        d = _dot(g_ref[k], z)
        hh = h_ref[k]
        dr, di = d[:FFT_N2], d[FFT_N2:]
        hr, hi = hh[:FFT_N2], hh[FFT_N2:]
        y = jnp.concatenate([dr * hr - di * hi, dr * hi + di * hr], axis=0).astype(BF16)
        e = _dot(gi_ref[k], y)
        _scatter_rows(o_ref, (0,), k, _pack_pairs(e))


def _mid(x5, g, gi, hspec, order, P, N1):
    kb = K_BLOCK
    blk = pl.BlockSpec((1, CH_BLOCKS, FFT_N2, kb, LANES), lambda i, p: (p, 0, 0, i, 0))
    return pl.pallas_call(
        _mid_body,
        grid=(N1 // kb, P),
        in_specs=[
            blk,
            pl.BlockSpec((kb, 2 * FFT_N2, 2 * FFT_N2), lambda i, p: (i, 0, 0)),
            pl.BlockSpec((kb, 2 * FFT_N2, 2 * FFT_N2), lambda i, p: (i, 0, 0)),
            pl.BlockSpec((kb, 2 * FFT_N2, HYENA_W), lambda i, p: (i, 0, order)),
        ],
        out_specs=blk,
        out_shape=jax.ShapeDtypeStruct((P, CH_BLOCKS, FFT_N2, N1, LANES), PACKED),
        compiler_params=_cparams("parallel", "parallel"),
        name="dft_n2_conv",
    )(x5, g, gi, hspec)


def _s1inv_body(e_ref, w_ref, src_ref, mul_ref, bias_ref, o_ref, y_ref):
    _, nb, n1h, rb, _ = src_ref.shape
    for r in range(rb):
        e = _unpack_pairs(jnp.concatenate([e_ref[0, c, r] for c in range(nb)], axis=1))
        y = _dot(w_ref[...], e)
        for ro in range(2):
            _scatter_rows(y_ref, (ro,), r, y[ro * n1h:(ro + 1) * n1h])
    o_ref[...] = mul_ref[...] * (y_ref[...] + src_ref[...] * bias_ref[...][None, :, None])


def _s1inv(e5, w1i, src, src_which, mul, mul_which, bias3, P, N1):
    N1h = N1 // 2
    member = lambda which: pl.BlockSpec((None, 2, None, CH_BLOCKS, N1h, R_BLOCK, LANES),
                                        lambda p, j: (which, 0, p, 0, 0, j, 0))
    return pl.pallas_call(
        _s1inv_body,
        grid=(P, FFT_N2 // R_BLOCK),
        in_specs=[
            pl.BlockSpec((1, CH_BLOCKS, R_BLOCK, N1, LANES), lambda p, j: (p, 0, j, 0, 0)),
            pl.BlockSpec((N1, 2 * N1), lambda p, j: (0, 0)),
            member(src_which),
            member(mul_which),
            pl.BlockSpec((CH_BLOCKS, 1, LANES), lambda p, j: (0, 0, 0)),
        ],
        out_specs=member(0),
        out_shape=jax.ShapeDtypeStruct((1, 2, P, CH_BLOCKS, N1h, FFT_N2, LANES), F32),
        scratch_shapes=[pltpu.VMEM((2, CH_BLOCKS, N1h, R_BLOCK, LANES), F32)],
        compiler_params=_cparams("parallel", "parallel"),
        name="idft_n1",
    )(e5, w1i, src, mul, bias3)


def _outproj_body(a_ref, ga_ref, z_ref, gh_ref, x_ref, wa_ref, wh_ref, w_ref, fw_ref, o_ref, *, final):
    def norm_gate(y, w, g):
        ms = jnp.mean(y * y, axis=-1, keepdims=True)
        return (y * lax.rsqrt(ms + EPS) * w) * (g * (1.0 / (1.0 + jnp.exp(-g))))

    attn = a_ref[0].reshape(ATTN_W, a_ref.shape[-1]).T
    oa = norm_gate(attn, wa_ref[...], ga_ref[...]).astype(BF16)
    zz = jnp.concatenate([z_ref[0, c] for c in range(CH_BLOCKS)], axis=1)
    oh = norm_gate(zz, wh_ref[...], gh_ref[...]).astype(BF16)
    y = x_ref[...] + (_dot(oa, w_ref[0:ATTN_W, :]) + _dot(oh, w_ref[ATTN_W:MIX_W, :]))
    if final:
        ms = jnp.mean(y * y, axis=-1, keepdims=True)
        y = y * lax.rsqrt(ms + EPS) * fw_ref[...]
    o_ref[...] = y


def _outproj(attn_t, ga, zz4, gh, x2d, wa, wh, w_bf, fw, final, tm=512):
    T = x2d.shape[0]
    nlb = zz4.shape[2] // tm
    row = lambda i: (i, 0)
    const = lambda i: (0, 0)
    body = functools.partial(_outproj_body, final=final)
    return pl.pallas_call(
        body,
        grid=(T // tm,),
        in_specs=[
            pl.BlockSpec((1, N_Q_HEADS, HEAD_DIM, tm), lambda i: (i // nlb, 0, 0, i % nlb)),
            pl.BlockSpec((tm, ATTN_W), row),
            pl.BlockSpec((1, CH_BLOCKS, tm, LANES), lambda i: (i // nlb, 0, i % nlb, 0)),
            pl.BlockSpec((tm, HYENA_W), row),
            pl.BlockSpec((tm, D_MODEL), row),
            pl.BlockSpec((1, ATTN_W), const),
            pl.BlockSpec((1, HYENA_W), const),
            pl.BlockSpec((MIX_W, D_MODEL), const),
            pl.BlockSpec((1, D_MODEL), const),
        ],
        out_specs=pl.BlockSpec((tm, D_MODEL), row),
        out_shape=jax.ShapeDtypeStruct((T, D_MODEL), F32),
        compiler_params=_cparams("parallel"),
        name="outproj",
    )(attn_t, ga, zz4, gh, x2d, wa, wh, w_bf, fw)


def _rope_tables(L):
    t = jnp.arange(L, dtype=jnp.int32)
    pos = jnp.stack([t // GRID_W, t % GRID_W], axis=-1).astype(F32)
    freqs = ROPE_THETA ** (-jnp.arange(ROPE_FREQS, dtype=F32) / ROPE_FREQS)
    ang = pos[:, :, None] * freqs
    cos, sin = jnp.cos(ang), jnp.sin(ang)
    c_head = jnp.stack([cos, cos], axis=2).reshape(L, HEAD_DIM)
    s_head = jnp.stack([-sin, sin], axis=2).reshape(L, HEAD_DIM)
    reps = LANES // HEAD_DIM
    return jnp.tile(c_head, (1, reps)), jnp.tile(s_head, (1, reps))


def _head_mean_matrix():
    idx = jnp.arange(LANES) // HEAD_DIM
    return jnp.where(idx[:, None] == idx[None, :], 1.0 / HEAD_DIM, 0.0).astype(BF16)


def _dft_tables(L):
    N = 2 * L
    N1 = N // FFT_N2
    N1h = N1 // 2
    two_pi = 2.0 * math.pi
    k1 = jnp.arange(N1, dtype=jnp.int32)
    th = ((k1[:, None] * k1[None, :]) % N1).astype(F32) * (two_pi / N1)
    c, s = jnp.cos(th), jnp.sin(th)
    ch, sh = c[:, :N1h], s[:, :N1h]
    w1 = jnp.stack([jnp.concatenate([ch, sh], axis=1),
                    jnp.concatenate([-sh, ch], axis=1)], axis=1).reshape(2 * N1, N1)
    w1f = jnp.stack([c, -s], axis=1).reshape(2 * N1, N1)
    ct, st = ch.T, sh.T
    top = jnp.stack([ct, -st], axis=2).reshape(N1h, 2 * N1)
    bot = jnp.stack([st, ct], axis=2).reshape(N1h, 2 * N1)
    w1i = jnp.concatenate([top, bot], axis=0) * (1.0 / N)
    k2 = jnp.arange(FFT_N2, dtype=jnp.int32)
    freq = k1[:, None, None] + N1 * k2[None, :, None]
    ph = ((freq * k2[None, None, :]) % N).astype(F32) * (two_pi / N)
    cp, sp = jnp.cos(ph), jnp.sin(ph)
    g = jnp.concatenate([jnp.stack([cp, sp], axis=3).reshape(N1, FFT_N2, 2 * FFT_N2),
                         jnp.stack([-sp, cp], axis=3).reshape(N1, FFT_N2, 2 * FFT_N2)], axis=1)
    cpt, spt = jnp.swapaxes(cp, 1, 2), jnp.swapaxes(sp, 1, 2)
    gi = jnp.stack([jnp.concatenate([cpt, -spt], axis=2),
                    jnp.concatenate([spt, cpt], axis=2)], axis=2).reshape(N1, 2 * FFT_N2, 2 * FFT_N2)
    return w1.astype(BF16), w1f.astype(BF16), w1i.astype(BF16), g.astype(BF16), gi.astype(BF16)


def _filter_positions(L):
    m = jnp.arange(2 * L, dtype=jnp.int32)
    pos = jnp.where(m < L, m, 2 * L - m)
    valid = (m != L).astype(F32)[:, None]
    pos = jnp.where(m == L, 0, pos).astype(F32)
    t = pos / (L - 1)
    w = 2.0 * math.pi * pos / L
    bands = jnp.linspace(1e-4, FILT_BANDS - 1, FILT_BANDS, dtype=F32)
    ang = w[:, None] * bands[None, :]
    z = jnp.concatenate([t[:, None], jnp.cos(ang), -jnp.sin(ang)], axis=-1)
    z = jnp.pad(z, ((0, 0), (0, LANES - FILT_EMB)))
    return z, valid


def _pad_to(a, shape):
    return jnp.pad(a, [(0, s - d) for d, s in zip(a.shape, shape)])


def _trunk(x, norm_w, w_in, q_norm_w, k_norm_w, conv_w, conv_b, filt_w1, filt_b1,
           filt_w2, filt_b2, filt_w3, filt_freq, filt_decay, hyena_bias,
           attn_out_norm_w, hyena_out_norm_w, w_out, final_norm_w):
    B, L, _ = x.shape
    assert B % 2 == 0 and L % 1024 == 0
    P = B // 2
    N1 = 2 * L // FFT_N2
    N1h = N1 // 2
    T = B * L

    ctab, stab = _rope_tables(L)
    pmat = _head_mean_matrix()
    w1m, w1f, w1i, gmat, gimat = _dft_tables(L)
    ztab, valid = _filter_positions(L)
    reps = LANES // HEAD_DIM

    x2d = x.reshape(T, D_MODEL)
    for l in range(DEPTH):
        q, k, v, ga, uc, gh = _inproj(
            x2d, norm_w[l][None], w_in[l].astype(BF16),
            jnp.tile(q_norm_w[l], reps)[None], jnp.tile(k_norm_w[l], reps)[None],
            pmat, ctab, stab, conv_w[l], conv_b[l][None], B, L)
        attn = _attention(q, k, v, B, L)

        w3 = filt_w3[l].reshape(FILT_HID, HYENA_ORDER, 2, HYENA_W)
        w3d = _pad_to(jnp.transpose(w3, (2, 0, 1, 3)).reshape(2, FILT_HID, HYENA_ORDER * HYENA_W),
                      (2, LANES, HYENA_ORDER * HYENA_W))
        dec = filt_decay[l].reshape(HYENA_ORDER, 2, HYENA_W)
        decd = jnp.transpose(dec, (1, 0, 2)).reshape(2, 1, HYENA_ORDER * HYENA_W)
        buf, ssum = _filters(
            ztab, valid,
            _pad_to(filt_w1[l], (LANES, LANES)), _pad_to(filt_b1[l][None], (1, LANES)),
            _pad_to(filt_w2[l], (LANES, LANES)), _pad_to(filt_b2[l][None], (1, LANES)),
            _pad_to(filt_freq[l][None], (1, LANES)), w3d, decd, L)
        hpre = _s1_filter(buf.reshape(FILT_BLOCKS, N1, FFT_N2, LANES), ssum, w1f, N1)
        hspec = _mid_filter(hpre, gmat, N1)

        parts = uc.reshape(HYENA_ORDER + 1, 2, P, CH_BLOCKS, N1h, FFT_N2, LANES)
        zz = parts
        for o in range(HYENA_ORDER):
            o1 = _s1(zz, 0, w1m, P, N1)
            e = _mid(o1, gmat, gimat, hspec, o, P, N1)
            bias3 = hyena_bias[l, o].reshape(CH_BLOCKS, 1, LANES)
            zz = _s1inv(e, w1i, zz, 0, parts, o + 1, bias3, P, N1)
        zz4 = zz.reshape(B, CH_BLOCKS, L, LANES)

        x2d = _outproj(attn, ga, zz4, gh, x2d,
                       attn_out_norm_w[l][None], hyena_out_norm_w[l][None],
                       w_out[l].astype(BF16), final_norm_w[None], final=(l == DEPTH - 1))
    return x2d.reshape(B, L, D_MODEL)


def kernel(x_prompt, x_sample, norm_w, w_in, q_norm_w, k_norm_w, conv_w, conv_b, filt_w1, filt_b1, filt_w2, filt_b2, filt_w3, filt_freq, filt_decay, hyena_bias, attn_out_norm_w, hyena_out_norm_w, w_out, final_norm_w):
    weights = (norm_w, w_in, q_norm_w, k_norm_w, conv_w, conv_b, filt_w1, filt_b1, filt_w2, filt_b2,
               filt_w3, filt_freq, filt_decay, hyena_bias, attn_out_norm_w, hyena_out_norm_w, w_out,
               final_norm_w)
    return (_trunk(x_prompt, *weights), _trunk(x_sample, *weights))
```

```python
import functools
import math

import jax
import jax.numpy as jnp
from jax import lax
from jax.experimental import pallas as pl
from jax.experimental.pallas import tpu as pltpu

F32 = jnp.float32
BF16 = jnp.bfloat16
PACKED = jnp.uint32

D_MODEL = 1024
DEPTH = 4
GRID_W = 64
HEAD_DIM = 64
N_Q_HEADS = 8
N_KV_HEADS = 2
Q_PER_KV = N_Q_HEADS // N_KV_HEADS
ATTN_W = N_Q_HEADS * HEAD_DIM
KV_W = N_KV_HEADS * HEAD_DIM
HYENA_W = 512
HYENA_ORDER = 2
MIX_W = ATTN_W + HYENA_W
SHORT_CONV = 3
FILT_EMB = 33
FILT_BANDS = 16
FILT_HID = 64
N_FILT = HYENA_ORDER * 2 * HYENA_W
MOD_SHIFT = 0.05
ROPE_THETA = 10000.0
ROPE_FREQS = HEAD_DIM // 4
EPS = 1e-6
COL_K = ATTN_W
COL_V = COL_K + KV_W
COL_GA = COL_V + KV_W
COL_U = COL_GA + ATTN_W
COL_GH = COL_U + (HYENA_ORDER + 1) * HYENA_W
D_IN_PROJ = COL_GH + HYENA_W

QK_SCALE = math.log2(math.e) / math.sqrt(HEAD_DIM)
KV_CHUNK = 512
Q_TILE = 512
Q_SUB = 256
V_ROWS = HEAD_DIM + 16
LANES = 128
FFT_N2 = 128
CH_BLOCKS = HYENA_W // LANES
FILT_BLOCKS = HYENA_ORDER * CH_BLOCKS
R_BLOCK = 8
K_BLOCK = 8
VMEM_LIMIT_BYTES = 48 * 1024 * 1024
HIGHEST = lax.Precision.HIGHEST


def _cparams(*sem):
    return pltpu.CompilerParams(dimension_semantics=sem, vmem_limit_bytes=VMEM_LIMIT_BYTES)


def _dot(a, b):
    return jnp.dot(a, b, preferred_element_type=F32)


def _dot_f32(a, b):
    return jnp.dot(a, b, preferred_element_type=F32, precision=HIGHEST)


def _inproj_body(x_ref, xprev_ref, xnext_ref, nw_ref, w_ref, qnw_ref, knw_ref, p_ref, c_ref, s_ref,
                 cw_ref, cb_ref, q_ref, k_ref, v_ref, ga_ref, parts_ref, gh_ref, *, nlb):
    def normed(x):
        ms = jnp.mean(x * x, axis=-1, keepdims=True)
        return (x * lax.rsqrt(ms + EPS) * nw_ref[...]).astype(BF16)

    h = normed(x_ref[...])
    p = p_ref[...]
    c = c_ref[...]
    s = s_ref[...]
    lane = lax.broadcasted_iota(jnp.int32, c.shape, 1)
    first_half = (lane % (2 * ROPE_FREQS)) < ROPE_FREQS

    def norm_rope(y, nw):
        y2 = y * y
        hi = y2.astype(BF16)
        lo = (y2 - hi.astype(F32)).astype(BF16)
        msq = _dot(hi, p) + _dot(lo, p)
        yn = y * lax.rsqrt(msq + EPS) * nw
        partner = jnp.where(first_half, pltpu.roll(yn, LANES - ROPE_FREQS, 1),
                            pltpu.roll(yn, ROPE_FREQS, 1))
        return yn * c + partner * s

    q = _dot(h, w_ref[:, 0:COL_K])
    qnw = qnw_ref[...]
    for j in range(ATTN_W // LANES):
        r = norm_rope(q[:, j * LANES:(j + 1) * LANES], qnw) * QK_SCALE
        rt = r.T
        q_ref[0, 2 * j] = rt[:HEAD_DIM].astype(BF16)
        q_ref[0, 2 * j + 1] = rt[HEAD_DIM:].astype(BF16)
    kv = _dot(h, w_ref[:, COL_K:COL_GA])
    kr = norm_rope(kv[:, :KV_W], knw_ref[...])
    k_ref[0, 0] = kr[:, :HEAD_DIM].astype(BF16)
    k_ref[0, 1] = pltpu.roll(kr, HEAD_DIM, 1)[:, :HEAD_DIM].astype(BF16)
    vt = kv[:, KV_W:].T
    extra = lax.broadcasted_iota(jnp.int32, (V_ROWS - HEAD_DIM, KV_CHUNK), 0)
    ones_row = jnp.where(extra == 0, 1.0, 0.0).astype(BF16)
    for c in range(vt.shape[1] // KV_CHUNK):
        for g in range(N_KV_HEADS):
            v_ref[0, g, c, 0:HEAD_DIM, :] = vt[g * HEAD_DIM:(g + 1) * HEAD_DIM,
                                               c * KV_CHUNK:(c + 1) * KV_CHUNK].astype(BF16)
            v_ref[0, g, c, HEAD_DIM:V_ROWS, :] = ones_row
    ga_ref[...] = _dot(h, w_ref[:, COL_GA:COL_U]).astype(ga_ref.dtype)
    gh_ref[...] = _dot(h, w_ref[:, COL_GH:D_IN_PROJ]).astype(gh_ref.dtype)

    u = _dot(h, w_ref[:, COL_U:COL_GH])
    halo = xprev_ref.shape[0]
    hh = normed(jnp.concatenate([xprev_ref[...], xnext_ref[...]], axis=0))
    uh = _dot(hh, w_ref[:, COL_U:COL_GH])
    li = pl.program_id(0) % nlb
    prev_row = jnp.where(li == 0, 0.0, uh[halo - 1:halo])
    next_row = jnp.where(li == nlb - 1, 0.0, uh[halo:halo + 1])
    tm = u.shape[0]
    row = lax.broadcasted_iota(jnp.int32, u.shape, 0)
    um = jnp.where(row == 0, prev_row, pltpu.roll(u, 1, 0))
    up = jnp.where(row == tm - 1, next_row, pltpu.roll(u, tm - 1, 0))
    cw = cw_ref[...]
    res = um * cw[0:1] + u * cw[1:2] + up * cw[2:3] + cb_ref[...]
    for part in range(HYENA_ORDER + 1):
        for cblk in range(CH_BLOCKS):
            lo = part * HYENA_W + cblk * LANES
            parts_ref[part, 0, cblk] = res[:, lo:lo + LANES]


def _inproj(x2d, nw, w_bf, qnw, knw, pmat, ctab, stab, cw, cb, B, L, tm=512, halo=8):
    T = B * L
    nlb = L // tm
    rpb = tm // halo
    UW = (HYENA_ORDER + 1) * HYENA_W
    row = lambda i: (i, 0)
    const = lambda i: (0, 0)
    pos = lambda i: (i % nlb, 0)
    heads = lambda i: (i // nlb, 0, i % nlb, 0)
    return pl.pallas_call(
        functools.partial(_inproj_body, nlb=nlb),
        grid=(T // tm,),
        in_specs=[
            pl.BlockSpec((tm, D_MODEL), row),
            pl.BlockSpec((halo, D_MODEL), lambda i: (jnp.maximum(i * rpb - 1, 0), 0)),
            pl.BlockSpec((halo, D_MODEL), lambda i: (jnp.minimum((i + 1) * rpb, T // halo - 1), 0)),
            pl.BlockSpec((1, D_MODEL), const),
            pl.BlockSpec((D_MODEL, D_IN_PROJ), const),
            pl.BlockSpec((1, LANES), const),
            pl.BlockSpec((1, LANES), const),
            pl.BlockSpec((LANES, LANES), const),
            pl.BlockSpec((tm, LANES), pos),
            pl.BlockSpec((tm, LANES), pos),
            pl.BlockSpec((SHORT_CONV, UW), const),
            pl.BlockSpec((1, UW), const),
        ],
        out_specs=[
            pl.BlockSpec((1, N_Q_HEADS, HEAD_DIM, tm), lambda i: (i // nlb, 0, 0, i % nlb)),
            pl.BlockSpec((1, N_KV_HEADS, tm, HEAD_DIM), heads),
            pl.BlockSpec((1, N_KV_HEADS, tm // KV_CHUNK, V_ROWS, KV_CHUNK),
                         lambda i: (i // nlb, 0, i % nlb, 0, 0)),
            pl.BlockSpec((tm, ATTN_W), row),
            pl.BlockSpec((HYENA_ORDER + 1, 1, CH_BLOCKS, tm, LANES), lambda i: (0, i // nlb, 0, i % nlb, 0)),
            pl.BlockSpec((tm, HYENA_W), row),
        ],
        out_shape=[
            jax.ShapeDtypeStruct((B, N_Q_HEADS, HEAD_DIM, L), BF16),
            jax.ShapeDtypeStruct((B, N_KV_HEADS, L, HEAD_DIM), BF16),
            jax.ShapeDtypeStruct((B, N_KV_HEADS, L // KV_CHUNK, V_ROWS, KV_CHUNK), BF16),
            jax.ShapeDtypeStruct((T, ATTN_W), BF16),
            jax.ShapeDtypeStruct((HYENA_ORDER + 1, B, CH_BLOCKS, L, LANES), F32),
            jax.ShapeDtypeStruct((T, HYENA_W), BF16),
        ],
        compiler_params=_cparams("parallel"),
        name="inproj",
    )(x2d, x2d, x2d, nw, w_bf, qnw, knw, pmat, ctab, stab, cw, cb)


def _attn_body(qt_ref, k_ref, vt_ref, o_ref, s_ref, p_ref, al_ref, cm_ref, m_ref, acc_ref, *, tq, nk):
    cols = Q_PER_KV * tq
    ncb = cols // LANES
    m_ref[...] = jnp.full(m_ref.shape, -jnp.inf, F32)
    acc_ref[...] = jnp.zeros(acc_ref.shape, F32)

    def scores(j, slot):
        start = pl.multiple_of(j * KV_CHUNK, KV_CHUNK)
        k = k_ref[0, 0, pl.ds(start, KV_CHUNK), :]
        for h in range(Q_PER_KV):
            for q0 in range(0, tq, Q_SUB):
                c0 = h * tq + q0
                r = _dot(k, qt_ref[0, h, :, q0:q0 + Q_SUB])
                for i in range(Q_SUB // LANES):
                    s_ref[slot, c0 // LANES + i] = r[:, i * LANES:(i + 1) * LANES]
                cm_ref[slot, :, c0:c0 + Q_SUB] = jnp.max(r, axis=0, keepdims=True)

    def accumulate(j, slot):
        vt = vt_ref[0, 0, j]
        for i in range(cols // Q_SUB):
            cs = slice(i * Q_SUB, (i + 1) * Q_SUB)
            nb = Q_SUB // LANES
            p = jnp.concatenate([p_ref[slot, nb * i + t] for t in range(nb)], axis=1)
            acc_ref[:, cs] = al_ref[slot, :, cs] * acc_ref[:, cs] + _dot(vt, p)

    def softmax(slot):
        m_old = m_ref[...]
        m_new = jnp.maximum(m_old, cm_ref[slot])
        m_ref[...] = m_new
        al_ref[slot] = jnp.exp2(m_old - m_new)
        for c in range(ncb):
            p_ref[slot, c] = jnp.exp2(s_ref[slot, c] - m_new[:, c * LANES:(c + 1) * LANES]).astype(BF16)

    scores(0, 0)
    scores(1, 1)
    softmax(0)
    scores(2, 0)
    accumulate(0, 0)
    softmax(1)

    def step(jj, carry):
        j = 2 * jj
        scores(j + 1, 1)
        accumulate(j - 1, 1)
        softmax(0)
        scores(jnp.minimum(j + 2, nk - 1), 0)
        accumulate(j, 0)
        softmax(1)
        return carry

    lax.fori_loop(1, nk // 2, step, 0)
    accumulate(nk - 1, 1)
    o = acc_ref[0:HEAD_DIM, :] / acc_ref[HEAD_DIM:HEAD_DIM + 1, :]
    for h in range(Q_PER_KV):
        o_ref[0, h] = o[:, h * tq:(h + 1) * tq]


def _attention(qt, k, vt, B, L):
    tq = Q_TILE
    nk = L // KV_CHUNK
    assert nk % 2 == 0 and nk >= 4
    cols = Q_PER_KV * tq
    ncb = cols // LANES
    body = functools.partial(_attn_body, tq=tq, nk=nk)
    return pl.pallas_call(
        body,
        grid=(B, N_KV_HEADS, L // tq),
        in_specs=[
            pl.BlockSpec((1, Q_PER_KV, HEAD_DIM, tq), lambda b, g, i: (b, g, 0, i)),
            pl.BlockSpec((1, 1, L, HEAD_DIM), lambda b, g, i: (b, g, 0, 0)),
            pl.BlockSpec((1, 1, nk, V_ROWS, KV_CHUNK), lambda b, g, i: (b, g, 0, 0, 0)),
        ],
        out_specs=pl.BlockSpec((1, Q_PER_KV, HEAD_DIM, tq), lambda b, g, i: (b, g, 0, i)),
        out_shape=jax.ShapeDtypeStruct((B, N_Q_HEADS, HEAD_DIM, L), F32),
        scratch_shapes=[
            pltpu.VMEM((2, ncb, KV_CHUNK, LANES), F32),
            pltpu.VMEM((2, ncb, KV_CHUNK, LANES), BF16),
            pltpu.VMEM((2, 1, cols), F32),
            pltpu.VMEM((2, 1, cols), F32),
            pltpu.VMEM((1, cols), F32),
            pltpu.VMEM((V_ROWS, cols), F32),
        ],
        compiler_params=_cparams("parallel", "parallel", "parallel"),
        name="attention",
    )(qt, k, vt)


def _filt_body(z_ref, valid_ref, w1_ref, b1_ref, w2_ref, b2_ref, fr_ref, w3_ref, dec_ref,
               buf_ref, sum_ref):
    i = pl.program_id(0)
    z = z_ref[...]
    fr = fr_ref[...]
    h = jnp.sin(fr * (_dot_f32(z, w1_ref[...]) + b1_ref[...]))
    h = jnp.sin(fr * (_dot_f32(h, w2_ref[...]) + b2_ref[...]))
    h = _dot_f32(h, w3_ref[0])
    t = z[:, 0:1]
    win = jnp.exp(-t * jnp.abs(dec_ref[0])) + MOD_SHIFT
    out = h * win * valid_ref[...]
    for c in range(FILT_BLOCKS):
        buf_ref[c] = out[:, c * LANES:(c + 1) * LANES]

    @pl.when(i == 0)
    def _():
        sum_ref[...] = jnp.zeros_like(sum_ref)

    sum_ref[...] += jnp.sum(jnp.abs(out), axis=0, keepdims=True)


def _filters(ztab, valid, w1, b1, w2, b2, fr, w3d, decd, L, rb=512):
    n = 2 * L
    nb = n // rb
    half = nb // 2
    W = HYENA_ORDER * HYENA_W
    const = lambda i: (0, 0)
    return pl.pallas_call(
        _filt_body,
        grid=(nb,),
        in_specs=[
            pl.BlockSpec((rb, LANES), lambda i: (i, 0)),
            pl.BlockSpec((rb, 1), lambda i: (i, 0)),
            pl.BlockSpec((LANES, LANES), const),
            pl.BlockSpec((1, LANES), const),
            pl.BlockSpec((LANES, LANES), const),
            pl.BlockSpec((1, LANES), const),
            pl.BlockSpec((1, LANES), const),
            pl.BlockSpec((1, LANES, W), lambda i: (i // half, 0, 0)),
            pl.BlockSpec((1, 1, W), lambda i: (i // half, 0, 0)),
        ],
        out_specs=[
            pl.BlockSpec((FILT_BLOCKS, rb, LANES), lambda i: (0, i, 0)),
            pl.BlockSpec((1, W), const),
        ],
        out_shape=[
            jax.ShapeDtypeStruct((FILT_BLOCKS, n, LANES), F32),
            jax.ShapeDtypeStruct((1, W), F32),
        ],
        compiler_params=_cparams("arbitrary"),
        name="filters",
    )(ztab, valid, w1, b1, w2, b2, fr, w3d, decd)


def _strided_rows(ref, lead):
    n = len(lead)
    nblk, count, stride = ref.shape[n:n + 3]
    return ref.reshape(ref.shape[:n + 1] + (count * stride, LANES)), nblk, count, stride


def _gather_rows(ref, lead, r):
    flat, nblk, count, stride = _strided_rows(ref, lead)
    return jnp.concatenate([flat[lead + (c, pl.ds(r, count, stride=stride), slice(None))]
                            for c in range(nblk)], axis=1)


def _scatter_rows(ref, lead, r, val):
    flat, nblk, count, stride = _strided_rows(ref, lead)
    for c in range(nblk):
        flat[lead + (c, pl.ds(r, count, stride=stride), slice(None))] = val[:, c * LANES:(c + 1) * LANES]


def _pack_pairs(y):
    return pltpu.bitcast(y.astype(BF16), PACKED)


def _unpack_pairs(w):
    return pltpu.bitcast(w, BF16)


def _s1_body(a_ref, w_ref, o_ref):
    _, nb, n1h, rb, _ = a_ref.shape
    for r in range(rb):
        z = jnp.concatenate([_gather_rows(a_ref, (m,), r) for m in range(2)], axis=0)
        y = _pack_pairs(_dot(w_ref[...], z.astype(BF16)))
        for c in range(nb):
            o_ref[0, c, r] = y[:, c * LANES:(c + 1) * LANES]


def _s1(src, which, w1m, P, N1):
    N1h = N1 // 2
    return pl.pallas_call(
        _s1_body,
        grid=(P, FFT_N2 // R_BLOCK),
        in_specs=[
            pl.BlockSpec((None, 2, None, CH_BLOCKS, N1h, R_BLOCK, LANES), lambda p, j: (which, 0, p, 0, 0, j, 0)),
            pl.BlockSpec((2 * N1, N1), lambda p, j: (0, 0)),
        ],
        out_specs=pl.BlockSpec((1, CH_BLOCKS, R_BLOCK, N1, LANES), lambda p, j: (p, 0, j, 0, 0)),
        out_shape=jax.ShapeDtypeStruct((P, CH_BLOCKS, FFT_N2, N1, LANES), PACKED),
        compiler_params=_cparams("parallel", "parallel"),
        name="dft_n1",
    )(src, w1m)


def _s1f_body(b_ref, s_ref, w_ref, o_ref):
    nb, _, rb, _ = b_ref.shape
    for r in range(rb):
        z = (_gather_rows(b_ref, (), r) / s_ref[...]).astype(BF16)
        y = _pack_pairs(_dot(w_ref[...], z))
        for c in range(nb):
            o_ref[c, r] = y[:, c * LANES:(c + 1) * LANES]


def _s1_filter(buf4, ssum, w1f, N1):
    W = FILT_BLOCKS * LANES
    return pl.pallas_call(
        _s1f_body,
        grid=(FFT_N2 // R_BLOCK,),
        in_specs=[
            pl.BlockSpec((FILT_BLOCKS, N1, R_BLOCK, LANES), lambda j: (0, 0, j, 0)),
            pl.BlockSpec((1, W), lambda j: (0, 0)),
            pl.BlockSpec((2 * N1, N1), lambda j: (0, 0)),
        ],
        out_specs=pl.BlockSpec((FILT_BLOCKS, R_BLOCK, N1, LANES), lambda j: (0, j, 0, 0)),
        out_shape=jax.ShapeDtypeStruct((FILT_BLOCKS, FFT_N2, N1, LANES), PACKED),
        compiler_params=_cparams("parallel"),
        name="dft_n1_filter",
    )(buf4, ssum, w1f)


def _midf_body(x_ref, g_ref, h_ref):
    for k in range(x_ref.shape[2]):
        z = _unpack_pairs(_gather_rows(x_ref, (), k))
        h_ref[k] = _dot(g_ref[k], z)


def _mid_filter(hpre, g, N1):
    W = FILT_BLOCKS * LANES
    kb = K_BLOCK
    return pl.pallas_call(
        _midf_body,
        grid=(N1 // kb,),
        in_specs=[
            pl.BlockSpec((FILT_BLOCKS, FFT_N2, kb, LANES), lambda i: (0, 0, i, 0)),
            pl.BlockSpec((kb, 2 * FFT_N2, 2 * FFT_N2), lambda i: (i, 0, 0)),
        ],
        out_specs=pl.BlockSpec((kb, 2 * FFT_N2, W), lambda i: (i, 0, 0)),
        out_shape=jax.ShapeDtypeStruct((N1, 2 * FFT_N2, W), F32),
        compiler_params=_cparams("parallel"),
        name="dft_n2_filter",
    )(hpre, g)


def _mid_body(x_ref, g_ref, gi_ref, h_ref, o_ref, d_ref, y_ref):
    kb = x_ref.shape[3]

    def forward(k):
        z = _unpack_pairs(_gather_rows(x_ref, (0,), k))
        d_ref[k % 2] = _dot(g_ref[k], z)

    def product(k):
        hh = h_ref[k]
        dr, di = d_ref[k % 2, :FFT_N2], d_ref[k % 2, FFT_N2:]
        hr, hi = hh[:FFT_N2], hh[FFT_N2:]
        y_ref[k % 2, :FFT_N2] = (dr * hr - di * hi).astype(BF16)
        y_ref[k % 2, FFT_N2:] = (dr * hi + di * hr).astype(BF16)

    def inverse(k):
        e = _dot(gi_ref[k], y_ref[k % 2])
        _scatter_rows(o_ref, (0,), k, _pack_pairs(e))

    forward(0)
    for k in range(kb):
        if k + 1 < kb:
            forward(k + 1)
        if k > 0:
            inverse(k - 1)
        product(k)
    inverse(kb - 1)


def _mid(x5, g, gi, hspec, order, P, N1):
    kb = K_BLOCK
    blk = pl.BlockSpec((1, CH_BLOCKS, FFT_N2, kb, LANES), lambda i, p: (p, 0, 0, i, 0))
    return pl.pallas_call(
        _mid_body,
        grid=(N1 // kb, P),
        in_specs=[
            blk,
            pl.BlockSpec((kb, 2 * FFT_N2, 2 * FFT_N2), lambda i, p: (i, 0, 0)),
            pl.BlockSpec((kb, 2 * FFT_N2, 2 * FFT_N2), lambda i, p: (i, 0, 0)),
            pl.BlockSpec((kb, 2 * FFT_N2, HYENA_W), lambda i, p: (i, 0, order)),
        ],
        out_specs=blk,
        out_shape=jax.ShapeDtypeStruct((P, CH_BLOCKS, FFT_N2, N1, LANES), PACKED),
        scratch_shapes=[pltpu.VMEM((2, 2 * FFT_N2, HYENA_W), F32),
                        pltpu.VMEM((2, 2 * FFT_N2, HYENA_W), BF16)],
        compiler_params=_cparams("parallel", "parallel"),
        name="dft_n2_conv",
    )(x5, g, gi, hspec)


def _s1inv_body(e_ref, w_ref, src_ref, mul_ref, bias_ref, o_ref, y_ref):
    _, nb, n1h, rb, _ = src_ref.shape
    for r in range(rb):
        e = _unpack_pairs(jnp.concatenate([e_ref[0, c, r] for c in range(nb)], axis=1))
        y = _dot(w_ref[...], e)
        for ro in range(2):
            _scatter_rows(y_ref, (ro,), r, y[ro * n1h:(ro + 1) * n1h])
    o_ref[...] = mul_ref[...] * (y_ref[...] + src_ref[...] * bias_ref[...][None, :, None])


def _s1inv(e5, w1i, src, src_which, mul, mul_which, bias3, P, N1):
    N1h = N1 // 2
    member = lambda which: pl.BlockSpec((None, 2, None, CH_BLOCKS, N1h, R_BLOCK, LANES),
                                        lambda p, j: (which, 0, p, 0, 0, j, 0))
    return pl.pallas_call(
        _s1inv_body,
        grid=(P, FFT_N2 // R_BLOCK),
        in_specs=[
            pl.BlockSpec((1, CH_BLOCKS, R_BLOCK, N1, LANES), lambda p, j: (p, 0, j, 0, 0)),
            pl.BlockSpec((N1, 2 * N1), lambda p, j: (0, 0)),
            member(src_which),
            member(mul_which),
            pl.BlockSpec((CH_BLOCKS, 1, LANES), lambda p, j: (0, 0, 0)),
        ],
        out_specs=member(0),
        out_shape=jax.ShapeDtypeStruct((1, 2, P, CH_BLOCKS, N1h, FFT_N2, LANES), F32),
        scratch_shapes=[pltpu.VMEM((2, CH_BLOCKS, N1h, R_BLOCK, LANES), F32)],
        compiler_params=_cparams("parallel", "parallel"),
        name="idft_n1",
    )(e5, w1i, src, mul, bias3)


def _outproj_body(a_ref, ga_ref, z_ref, gh_ref, x_ref, wa_ref, wh_ref, w_ref, fw_ref, o_ref, *, final):
    def norm_gate(y, w, g):
        ms = jnp.mean(y * y, axis=-1, keepdims=True)
        g = g.astype(F32)
        return (y * lax.rsqrt(ms + EPS) * w) * (g * (1.0 / (1.0 + jnp.exp(-g))))

    attn = a_ref[0].reshape(ATTN_W, a_ref.shape[-1]).T
    oa = norm_gate(attn, wa_ref[...], ga_ref[...]).astype(BF16)
    zz = jnp.concatenate([z_ref[0, c] for c in range(CH_BLOCKS)], axis=1)
    oh = norm_gate(zz, wh_ref[...], gh_ref[...]).astype(BF16)
    y = x_ref[...] + (_dot(oa, w_ref[0:ATTN_W, :]) + _dot(oh, w_ref[ATTN_W:MIX_W, :]))
    if final:
        ms = jnp.mean(y * y, axis=-1, keepdims=True)
        y = y * lax.rsqrt(ms + EPS) * fw_ref[...]
    o_ref[...] = y


def _outproj(attn_t, ga, zz4, gh, x2d, wa, wh, w_bf, fw, final, tm=512):
    T = x2d.shape[0]
    nlb = zz4.shape[2] // tm
    row = lambda i: (i, 0)
    const = lambda i: (0, 0)
    body = functools.partial(_outproj_body, final=final)
    return pl.pallas_call(
        body,
        grid=(T // tm,),
        in_specs=[
            pl.BlockSpec((1, N_Q_HEADS, HEAD_DIM, tm), lambda i: (i // nlb, 0, 0, i % nlb)),
            pl.BlockSpec((tm, ATTN_W), row),
            pl.BlockSpec((1, CH_BLOCKS, tm, LANES), lambda i: (i // nlb, 0, i % nlb, 0)),
            pl.BlockSpec((tm, HYENA_W), row),
            pl.BlockSpec((tm, D_MODEL), row),
            pl.BlockSpec((1, ATTN_W), const),
            pl.BlockSpec((1, HYENA_W), const),
            pl.BlockSpec((MIX_W, D_MODEL), const),
            pl.BlockSpec((1, D_MODEL), const),
        ],
        out_specs=pl.BlockSpec((tm, D_MODEL), row),
        out_shape=jax.ShapeDtypeStruct((T, D_MODEL), F32),
        compiler_params=_cparams("parallel"),
        name="outproj",
    )(attn_t, ga, zz4, gh, x2d, wa, wh, w_bf, fw)


def _rope_tables(L):
    t = jnp.arange(L, dtype=jnp.int32)
    pos = jnp.stack([t // GRID_W, t % GRID_W], axis=-1).astype(F32)
    freqs = ROPE_THETA ** (-jnp.arange(ROPE_FREQS, dtype=F32) / ROPE_FREQS)
    ang = pos[:, :, None] * freqs
    cos, sin = jnp.cos(ang), jnp.sin(ang)
    c_head = jnp.stack([cos, cos], axis=2).reshape(L, HEAD_DIM)
    s_head = jnp.stack([-sin, sin], axis=2).reshape(L, HEAD_DIM)
    reps = LANES // HEAD_DIM
    return jnp.tile(c_head, (1, reps)), jnp.tile(s_head, (1, reps))


def _head_mean_matrix():
    idx = jnp.arange(LANES) // HEAD_DIM
    return jnp.where(idx[:, None] == idx[None, :], 1.0 / HEAD_DIM, 0.0).astype(BF16)


def _dft_tables(L):
    N = 2 * L
    N1 = N // FFT_N2
    N1h = N1 // 2
    two_pi = 2.0 * math.pi
    k1 = jnp.arange(N1, dtype=jnp.int32)
    th = ((k1[:, None] * k1[None, :]) % N1).astype(F32) * (two_pi / N1)
    c, s = jnp.cos(th), jnp.sin(th)
    ch, sh = c[:, :N1h], s[:, :N1h]
    w1 = jnp.stack([jnp.concatenate([ch, sh], axis=1),
                    jnp.concatenate([-sh, ch], axis=1)], axis=1).reshape(2 * N1, N1)
    w1f = jnp.stack([c, -s], axis=1).reshape(2 * N1, N1)
    ct, st = ch.T, sh.T
    top = jnp.stack([ct, -st], axis=2).reshape(N1h, 2 * N1)
    bot = jnp.stack([st, ct], axis=2).reshape(N1h, 2 * N1)
    w1i = jnp.concatenate([top, bot], axis=0) * (1.0 / N)
    k2 = jnp.arange(FFT_N2, dtype=jnp.int32)
    freq = k1[:, None, None] + N1 * k2[None, :, None]
    ph = ((freq * k2[None, None, :]) % N).astype(F32) * (two_pi / N)
    cp, sp = jnp.cos(ph), jnp.sin(ph)
    g = jnp.concatenate([jnp.stack([cp, sp], axis=3).reshape(N1, FFT_N2, 2 * FFT_N2),
                         jnp.stack([-sp, cp], axis=3).reshape(N1, FFT_N2, 2 * FFT_N2)], axis=1)
    cpt, spt = jnp.swapaxes(cp, 1, 2), jnp.swapaxes(sp, 1, 2)
    gi = jnp.stack([jnp.concatenate([cpt, -spt], axis=2),
                    jnp.concatenate([spt, cpt], axis=2)], axis=2).reshape(N1, 2 * FFT_N2, 2 * FFT_N2)
    return w1.astype(BF16), w1f.astype(BF16), w1i.astype(BF16), g.astype(BF16), gi.astype(BF16)


def _filter_positions(L):
    m = jnp.arange(2 * L, dtype=jnp.int32)
    pos = jnp.where(m < L, m, 2 * L - m)
    valid = (m != L).astype(F32)[:, None]
    pos = jnp.where(m == L, 0, pos).astype(F32)
    t = pos / (L - 1)
    w = 2.0 * math.pi * pos / L
    bands = jnp.linspace(1e-4, FILT_BANDS - 1, FILT_BANDS, dtype=F32)
    ang = w[:, None] * bands[None, :]
    z = jnp.concatenate([t[:, None], jnp.cos(ang), -jnp.sin(ang)], axis=-1)
    z = jnp.pad(z, ((0, 0), (0, LANES - FILT_EMB)))
    return z, valid


def _pad_to(a, shape):
    return jnp.pad(a, [(0, s - d) for d, s in zip(a.shape, shape)])


def _trunk(x, norm_w, w_in, q_norm_w, k_norm_w, conv_w, conv_b, filt_w1, filt_b1,
           filt_w2, filt_b2, filt_w3, filt_freq, filt_decay, hyena_bias,
           attn_out_norm_w, hyena_out_norm_w, w_out, final_norm_w):
    B, L, _ = x.shape
    assert B % 2 == 0 and L % 1024 == 0
    P = B // 2
    N1 = 2 * L // FFT_N2
    N1h = N1 // 2
    T = B * L

    ctab, stab = _rope_tables(L)
    pmat = _head_mean_matrix()
    w1m, w1f, w1i, gmat, gimat = _dft_tables(L)
    ztab, valid = _filter_positions(L)
    reps = LANES // HEAD_DIM

    x2d = x.reshape(T, D_MODEL)
    for l in range(DEPTH):
        q, k, v, ga, uc, gh = _inproj(
            x2d, norm_w[l][None], w_in[l].astype(BF16),
            jnp.tile(q_norm_w[l], reps)[None], jnp.tile(k_norm_w[l], reps)[None],
            pmat, ctab, stab, conv_w[l], conv_b[l][None], B, L)
        attn = _attention(q, k, v, B, L)

        w3 = filt_w3[l].reshape(FILT_HID, HYENA_ORDER, 2, HYENA_W)
        w3d = _pad_to(jnp.transpose(w3, (2, 0, 1, 3)).reshape(2, FILT_HID, HYENA_ORDER * HYENA_W),
                      (2, LANES, HYENA_ORDER * HYENA_W))
        dec = filt_decay[l].reshape(HYENA_ORDER, 2, HYENA_W)
        decd = jnp.transpose(dec, (1, 0, 2)).reshape(2, 1, HYENA_ORDER * HYENA_W)
        buf, ssum = _filters(
            ztab, valid,
            _pad_to(filt_w1[l], (LANES, LANES)), _pad_to(filt_b1[l][None], (1, LANES)),
            _pad_to(filt_w2[l], (LANES, LANES)), _pad_to(filt_b2[l][None], (1, LANES)),
            _pad_to(filt_freq[l][None], (1, LANES)), w3d, decd, L)
        hpre = _s1_filter(buf.reshape(FILT_BLOCKS, N1, FFT_N2, LANES), ssum, w1f, N1)
        hspec = _mid_filter(hpre, gmat, N1)

        parts = uc.reshape(HYENA_ORDER + 1, 2, P, CH_BLOCKS, N1h, FFT_N2, LANES)
        zz = parts
        for o in range(HYENA_ORDER):
            o1 = _s1(zz, 0, w1m, P, N1)
            e = _mid(o1, gmat, gimat, hspec, o, P, N1)
            bias3 = hyena_bias[l, o].reshape(CH_BLOCKS, 1, LANES)
            zz = _s1inv(e, w1i, zz, 0, parts, o + 1, bias3, P, N1)
        zz4 = zz.reshape(B, CH_BLOCKS, L, LANES)

        x2d = _outproj(attn, ga, zz4, gh, x2d,
                       attn_out_norm_w[l][None], hyena_out_norm_w[l][None],
                       w_out[l].astype(BF16), final_norm_w[None], final=(l == DEPTH - 1))
    return x2d.reshape(B, L, D_MODEL)


def kernel(x_prompt, x_sample, norm_w, w_in, q_norm_w, k_norm_w, conv_w, conv_b, filt_w1, filt_b1, filt_w2, filt_b2, filt_w3, filt_freq, filt_decay, hyena_bias, attn_out_norm_w, hyena_out_norm_w, w_out, final_norm_w):
    weights = (norm_w, w_in, q_norm_w, k_norm_w, conv_w, conv_b, filt_w1, filt_b1, filt_w2, filt_b2,
               filt_w3, filt_freq, filt_decay, hyena_bias, attn_out_norm_w, hyena_out_norm_w, w_out,
               final_norm_w)
    return (_trunk(x_prompt, *weights), _trunk(x_sample, *weights))
```

```python
import functools
import math

import jax
import jax.numpy as jnp
from jax import lax
from jax.experimental import pallas as pl
from jax.experimental.pallas import tpu as pltpu

F32 = jnp.float32
BF16 = jnp.bfloat16
PACKED = jnp.uint32

D_MODEL = 1024
DEPTH = 4
GRID_W = 64
HEAD_DIM = 64
N_Q_HEADS = 8
N_KV_HEADS = 2
Q_PER_KV = N_Q_HEADS // N_KV_HEADS
ATTN_W = N_Q_HEADS * HEAD_DIM
KV_W = N_KV_HEADS * HEAD_DIM
HYENA_W = 512
HYENA_ORDER = 2
MIX_W = ATTN_W + HYENA_W
SHORT_CONV = 3
FILT_EMB = 33
FILT_BANDS = 16
FILT_HID = 64
N_FILT = HYENA_ORDER * 2 * HYENA_W
MOD_SHIFT = 0.05
ROPE_THETA = 10000.0
ROPE_FREQS = HEAD_DIM // 4
EPS = 1e-6
COL_K = ATTN_W
COL_V = COL_K + KV_W
COL_GA = COL_V + KV_W
COL_U = COL_GA + ATTN_W
COL_GH = COL_U + (HYENA_ORDER + 1) * HYENA_W
D_IN_PROJ = COL_GH + HYENA_W

QK_SCALE = math.log2(math.e) / math.sqrt(HEAD_DIM)
KV_CHUNK = 512
Q_TILE = 512
Q_SUB = 256
V_ROWS = HEAD_DIM + 16
LANES = 128
FFT_N2 = 128
CH_BLOCKS = HYENA_W // LANES
FILT_BLOCKS = HYENA_ORDER * CH_BLOCKS
R_BLOCK = 8
K_BLOCK = 8
VMEM_LIMIT_BYTES = 48 * 1024 * 1024
HIGHEST = lax.Precision.HIGHEST


def _cparams(*sem):
    return pltpu.CompilerParams(dimension_semantics=sem, vmem_limit_bytes=VMEM_LIMIT_BYTES)


def _dot(a, b):
    return jnp.dot(a, b, preferred_element_type=F32)


def _dot_f32(a, b):
    return jnp.dot(a, b, preferred_element_type=F32, precision=HIGHEST)


def _inproj_body(x_ref, xprev_ref, xnext_ref, nw_ref, w_ref, qnw_ref, knw_ref, p_ref, c_ref, s_ref,
                 cw_ref, cb_ref, q_ref, k_ref, v_ref, ga_ref, parts_ref, gh_ref, *, nlb):
    def normed(x):
        ms = jnp.mean(x * x, axis=-1, keepdims=True)
        return (x * lax.rsqrt(ms + EPS) * nw_ref[...]).astype(BF16)

    h = normed(x_ref[...])
    p = p_ref[...]
    c = c_ref[...]
    s = s_ref[...]
    lane = lax.broadcasted_iota(jnp.int32, c.shape, 1)
    first_half = (lane % (2 * ROPE_FREQS)) < ROPE_FREQS

    def norm_rope(y, nw):
        y2 = y * y
        hi = y2.astype(BF16)
        lo = (y2 - hi.astype(F32)).astype(BF16)
        msq = _dot(hi, p) + _dot(lo, p)
        yn = y * lax.rsqrt(msq + EPS) * nw
        partner = jnp.where(first_half, pltpu.roll(yn, LANES - ROPE_FREQS, 1),
                            pltpu.roll(yn, ROPE_FREQS, 1))
        return yn * c + partner * s

    q = _dot(h, w_ref[:, 0:COL_K])
    qnw = qnw_ref[...]
    for j in range(ATTN_W // LANES):
        r = norm_rope(q[:, j * LANES:(j + 1) * LANES], qnw) * QK_SCALE
        rt = r.T
        q_ref[0, 2 * j] = rt[:HEAD_DIM].astype(BF16)
        q_ref[0, 2 * j + 1] = rt[HEAD_DIM:].astype(BF16)
    kv = _dot(h, w_ref[:, COL_K:COL_GA])
    kr = norm_rope(kv[:, :KV_W], knw_ref[...])
    k_ref[0, 0] = kr[:, :HEAD_DIM].astype(BF16)
    k_ref[0, 1] = pltpu.roll(kr, HEAD_DIM, 1)[:, :HEAD_DIM].astype(BF16)
    vt = kv[:, KV_W:].T
    extra = lax.broadcasted_iota(jnp.int32, (V_ROWS - HEAD_DIM, KV_CHUNK), 0)
    ones_row = jnp.where(extra == 0, 1.0, 0.0).astype(BF16)
    for c in range(vt.shape[1] // KV_CHUNK):
        for g in range(N_KV_HEADS):
            v_ref[0, g, c, 0:HEAD_DIM, :] = vt[g * HEAD_DIM:(g + 1) * HEAD_DIM,
                                               c * KV_CHUNK:(c + 1) * KV_CHUNK].astype(BF16)
            v_ref[0, g, c, HEAD_DIM:V_ROWS, :] = ones_row
    ga_ref[...] = _dot(h, w_ref[:, COL_GA:COL_U]).astype(ga_ref.dtype)
    gh_ref[...] = _dot(h, w_ref[:, COL_GH:D_IN_PROJ]).astype(gh_ref.dtype)

    u = _dot(h, w_ref[:, COL_U:COL_GH])
    halo = xprev_ref.shape[0]
    hh = normed(jnp.concatenate([xprev_ref[...], xnext_ref[...]], axis=0))
    uh = _dot(hh, w_ref[:, COL_U:COL_GH])
    li = pl.program_id(0) % nlb
    prev_row = jnp.where(li == 0, 0.0, uh[halo - 1:halo])
    next_row = jnp.where(li == nlb - 1, 0.0, uh[halo:halo + 1])
    tm = u.shape[0]
    row = lax.broadcasted_iota(jnp.int32, u.shape, 0)
    um = jnp.where(row == 0, prev_row, pltpu.roll(u, 1, 0))
    up = jnp.where(row == tm - 1, next_row, pltpu.roll(u, tm - 1, 0))
    cw = cw_ref[...]
    res = um * cw[0:1] + u * cw[1:2] + up * cw[2:3] + cb_ref[...]
    for part in range(HYENA_ORDER + 1):
        for cblk in range(CH_BLOCKS):
            lo = part * HYENA_W + cblk * LANES
            parts_ref[part, 0, cblk] = res[:, lo:lo + LANES]


def _inproj(x2d, nw, w_bf, qnw, knw, pmat, ctab, stab, cw, cb, B, L, tm=512, halo=8):
    T = B * L
    nlb = L // tm
    rpb = tm // halo
    UW = (HYENA_ORDER + 1) * HYENA_W
    row = lambda i: (i, 0)
    const = lambda i: (0, 0)
    pos = lambda i: (i % nlb, 0)
    heads = lambda i: (i // nlb, 0, i % nlb, 0)
    return pl.pallas_call(
        functools.partial(_inproj_body, nlb=nlb),
        grid=(T // tm,),
        in_specs=[
            pl.BlockSpec((tm, D_MODEL), row),
            pl.BlockSpec((halo, D_MODEL), lambda i: (jnp.maximum(i * rpb - 1, 0), 0)),
            pl.BlockSpec((halo, D_MODEL), lambda i: (jnp.minimum((i + 1) * rpb, T // halo - 1), 0)),
            pl.BlockSpec((1, D_MODEL), const),
            pl.BlockSpec((D_MODEL, D_IN_PROJ), const),
            pl.BlockSpec((1, LANES), const),
            pl.BlockSpec((1, LANES), const),
            pl.BlockSpec((LANES, LANES), const),
            pl.BlockSpec((tm, LANES), pos),
            pl.BlockSpec((tm, LANES), pos),
            pl.BlockSpec((SHORT_CONV, UW), const),
            pl.BlockSpec((1, UW), const),
        ],
        out_specs=[
            pl.BlockSpec((1, N_Q_HEADS, HEAD_DIM, tm), lambda i: (i // nlb, 0, 0, i % nlb)),
            pl.BlockSpec((1, N_KV_HEADS, tm, HEAD_DIM), heads),
            pl.BlockSpec((1, N_KV_HEADS, tm // KV_CHUNK, V_ROWS, KV_CHUNK),
                         lambda i: (i // nlb, 0, i % nlb, 0, 0)),
            pl.BlockSpec((tm, ATTN_W), row),
            pl.BlockSpec((HYENA_ORDER + 1, 1, CH_BLOCKS, tm, LANES), lambda i: (0, i // nlb, 0, i % nlb, 0)),
            pl.BlockSpec((tm, HYENA_W), row),
        ],
        out_shape=[
            jax.ShapeDtypeStruct((B, N_Q_HEADS, HEAD_DIM, L), BF16),
            jax.ShapeDtypeStruct((B, N_KV_HEADS, L, HEAD_DIM), BF16),
            jax.ShapeDtypeStruct((B, N_KV_HEADS, L // KV_CHUNK, V_ROWS, KV_CHUNK), BF16),
            jax.ShapeDtypeStruct((T, ATTN_W), BF16),
            jax.ShapeDtypeStruct((HYENA_ORDER + 1, B, CH_BLOCKS, L, LANES), F32),
            jax.ShapeDtypeStruct((T, HYENA_W), BF16),
        ],
        compiler_params=_cparams("parallel"),
        name="inproj",
    )(x2d, x2d, x2d, nw, w_bf, qnw, knw, pmat, ctab, stab, cw, cb)


def _attn_body(qt_ref, k_ref, vt_ref, o_ref, s_ref, p_ref, al_ref, cm_ref, m_ref, acc_ref, *, tq, nk):
    cols = Q_PER_KV * tq
    ncb = cols // LANES
    m_ref[...] = jnp.full(m_ref.shape, -jnp.inf, F32)
    acc_ref[...] = jnp.zeros(acc_ref.shape, F32)

    def scores(j, slot):
        start = pl.multiple_of(j * KV_CHUNK, KV_CHUNK)
        k = k_ref[0, 0, pl.ds(start, KV_CHUNK), :]
        for h in range(Q_PER_KV):
            for q0 in range(0, tq, Q_SUB):
                c0 = h * tq + q0
                r = _dot(k, qt_ref[0, h, :, q0:q0 + Q_SUB])
                for i in range(Q_SUB // LANES):
                    s_ref[slot, c0 // LANES + i] = r[:, i * LANES:(i + 1) * LANES]
                cm_ref[slot, :, c0:c0 + Q_SUB] = jnp.max(r, axis=0, keepdims=True)

    def accumulate(j, slot):
        vt = vt_ref[0, 0, j]
        for i in range(cols // Q_SUB):
            cs = slice(i * Q_SUB, (i + 1) * Q_SUB)
            nb = Q_SUB // LANES
            p = jnp.concatenate([p_ref[slot, nb * i + t] for t in range(nb)], axis=1)
            acc_ref[:, cs] = al_ref[slot, :, cs] * acc_ref[:, cs] + _dot(vt, p)

    def softmax(slot):
        m_old = m_ref[...]
        m_new = jnp.maximum(m_old, cm_ref[slot])
        m_ref[...] = m_new
        al_ref[slot] = jnp.exp2(m_old - m_new)
        for c in range(ncb):
            p_ref[slot, c] = jnp.exp2(s_ref[slot, c] - m_new[:, c * LANES:(c + 1) * LANES]).astype(BF16)

    scores(0, 0)
    scores(1, 1)
    softmax(0)
    scores(2, 0)
    accumulate(0, 0)
    softmax(1)

    def step(jj, carry):
        j = 2 * jj
        scores(j + 1, 1)
        accumulate(j - 1, 1)
        softmax(0)
        scores(jnp.minimum(j + 2, nk - 1), 0)
        accumulate(j, 0)
        softmax(1)
        return carry

    lax.fori_loop(1, nk // 2, step, 0)
    accumulate(nk - 1, 1)
    o = acc_ref[0:HEAD_DIM, :] / acc_ref[HEAD_DIM:HEAD_DIM + 1, :]
    for h in range(Q_PER_KV):
        o_ref[0, h] = o[:, h * tq:(h + 1) * tq]


def _attention(qt, k, vt, B, L):
    tq = Q_TILE
    nk = L // KV_CHUNK
    assert nk % 2 == 0 and nk >= 4
    cols = Q_PER_KV * tq
    ncb = cols // LANES
    body = functools.partial(_attn_body, tq=tq, nk=nk)
    return pl.pallas_call(
        body,
        grid=(B, N_KV_HEADS, L // tq),
        in_specs=[
            pl.BlockSpec((1, Q_PER_KV, HEAD_DIM, tq), lambda b, g, i: (b, g, 0, i)),
            pl.BlockSpec((1, 1, L, HEAD_DIM), lambda b, g, i: (b, g, 0, 0)),
            pl.BlockSpec((1, 1, nk, V_ROWS, KV_CHUNK), lambda b, g, i: (b, g, 0, 0, 0)),
        ],
        out_specs=pl.BlockSpec((1, Q_PER_KV, HEAD_DIM, tq), lambda b, g, i: (b, g, 0, i)),
        out_shape=jax.ShapeDtypeStruct((B, N_Q_HEADS, HEAD_DIM, L), F32),
        scratch_shapes=[
            pltpu.VMEM((2, ncb, KV_CHUNK, LANES), F32),
            pltpu.VMEM((2, ncb, KV_CHUNK, LANES), BF16),
            pltpu.VMEM((2, 1, cols), F32),
            pltpu.VMEM((2, 1, cols), F32),
            pltpu.VMEM((1, cols), F32),
            pltpu.VMEM((V_ROWS, cols), F32),
        ],
        compiler_params=_cparams("parallel", "parallel", "parallel"),
        name="attention",
    )(qt, k, vt)


def _filt_body(z_ref, valid_ref, w1_ref, b1_ref, w2_ref, b2_ref, fr_ref, w3_ref, dec_ref,
               buf_ref, sum_ref):
    i = pl.program_id(0)
    z = z_ref[...]
    fr = fr_ref[...]
    h = jnp.sin(fr * (_dot_f32(z, w1_ref[...]) + b1_ref[...]))
    h = jnp.sin(fr * (_dot_f32(h, w2_ref[...]) + b2_ref[...]))
    h = _dot_f32(h, w3_ref[0])
    t = z[:, 0:1]
    win = jnp.exp(-t * jnp.abs(dec_ref[0])) + MOD_SHIFT
    out = h * win * valid_ref[...]
    for c in range(FILT_BLOCKS):
        buf_ref[c] = out[:, c * LANES:(c + 1) * LANES]

    @pl.when(i == 0)
    def _():
        sum_ref[...] = jnp.zeros_like(sum_ref)

    sum_ref[...] += jnp.sum(jnp.abs(out), axis=0, keepdims=True)


def _filters(ztab, valid, w1, b1, w2, b2, fr, w3d, decd, L, rb=512):
    n = 2 * L
    nb = n // rb
    half = nb // 2
    W = HYENA_ORDER * HYENA_W
    const = lambda i: (0, 0)
    return pl.pallas_call(
        _filt_body,
        grid=(nb,),
        in_specs=[
            pl.BlockSpec((rb, LANES), lambda i: (i, 0)),
            pl.BlockSpec((rb, 1), lambda i: (i, 0)),
            pl.BlockSpec((LANES, LANES), const),
            pl.BlockSpec((1, LANES), const),
            pl.BlockSpec((LANES, LANES), const),
            pl.BlockSpec((1, LANES), const),
            pl.BlockSpec((1, LANES), const),
            pl.BlockSpec((1, LANES, W), lambda i: (i // half, 0, 0)),
            pl.BlockSpec((1, 1, W), lambda i: (i // half, 0, 0)),
        ],
        out_specs=[
            pl.BlockSpec((FILT_BLOCKS, rb, LANES), lambda i: (0, i, 0)),
            pl.BlockSpec((1, W), const),
        ],
        out_shape=[
            jax.ShapeDtypeStruct((FILT_BLOCKS, n, LANES), F32),
            jax.ShapeDtypeStruct((1, W), F32),
        ],
        compiler_params=_cparams("arbitrary"),
        name="filters",
    )(ztab, valid, w1, b1, w2, b2, fr, w3d, decd)


def _strided_rows(ref, lead):
    n = len(lead)
    nblk, count, stride = ref.shape[n:n + 3]
    return ref.reshape(ref.shape[:n + 1] + (count * stride, LANES)), nblk, count, stride


def _gather_rows(ref, lead, r):
    flat, nblk, count, stride = _strided_rows(ref, lead)
    return jnp.concatenate([flat[lead + (c, pl.ds(r, count, stride=stride), slice(None))]
                            for c in range(nblk)], axis=1)


def _scatter_rows(ref, lead, r, val):
    flat, nblk, count, stride = _strided_rows(ref, lead)
    for c in range(nblk):
        flat[lead + (c, pl.ds(r, count, stride=stride), slice(None))] = val[:, c * LANES:(c + 1) * LANES]


def _pack_pairs(y):
    return pltpu.bitcast(y.astype(BF16), PACKED)


def _unpack_pairs(w):
    return pltpu.bitcast(w, BF16)


def _s1_body(a_ref, w_ref, o_ref):
    _, nb, n1h, rb, _ = a_ref.shape
    for r in range(rb):
        z = jnp.concatenate([_gather_rows(a_ref, (m,), r) for m in range(2)], axis=0)
        y = _pack_pairs(_dot(w_ref[...], z.astype(BF16)))
        for c in range(nb):
            o_ref[0, c, r] = y[:, c * LANES:(c + 1) * LANES]


def _s1(src, which, w1m, P, N1):
    N1h = N1 // 2
    return pl.pallas_call(
        _s1_body,
        grid=(P, FFT_N2 // R_BLOCK),
        in_specs=[
            pl.BlockSpec((None, 2, None, CH_BLOCKS, N1h, R_BLOCK, LANES), lambda p, j: (which, 0, p, 0, 0, j, 0)),
            pl.BlockSpec((2 * N1, N1), lambda p, j: (0, 0)),
        ],
        out_specs=pl.BlockSpec((1, CH_BLOCKS, R_BLOCK, N1, LANES), lambda p, j: (p, 0, j, 0, 0)),
        out_shape=jax.ShapeDtypeStruct((P, CH_BLOCKS, FFT_N2, N1, LANES), PACKED),
        compiler_params=_cparams("parallel", "parallel"),
        name="dft_n1",
    )(src, w1m)


def _s1f_body(b_ref, s_ref, w_ref, o_ref):
    nb, _, rb, _ = b_ref.shape
    for r in range(rb):
        z = (_gather_rows(b_ref, (), r) / s_ref[...]).astype(BF16)
        y = _pack_pairs(_dot(w_ref[...], z))
        for c in range(nb):
            o_ref[c, r] = y[:, c * LANES:(c + 1) * LANES]


def _s1_filter(buf4, ssum, w1f, N1):
    W = FILT_BLOCKS * LANES
    return pl.pallas_call(
        _s1f_body,
        grid=(FFT_N2 // R_BLOCK,),
        in_specs=[
            pl.BlockSpec((FILT_BLOCKS, N1, R_BLOCK, LANES), lambda j: (0, 0, j, 0)),
            pl.BlockSpec((1, W), lambda j: (0, 0)),
            pl.BlockSpec((2 * N1, N1), lambda j: (0, 0)),
        ],
        out_specs=pl.BlockSpec((FILT_BLOCKS, R_BLOCK, N1, LANES), lambda j: (0, j, 0, 0)),
        out_shape=jax.ShapeDtypeStruct((FILT_BLOCKS, FFT_N2, N1, LANES), PACKED),
        compiler_params=_cparams("parallel"),
        name="dft_n1_filter",
    )(buf4, ssum, w1f)


def _midf_body(x_ref, g_ref, h_ref):
    for k in range(x_ref.shape[2]):
        z = _unpack_pairs(_gather_rows(x_ref, (), k))
        h_ref[k] = _dot(g_ref[k], z)


def _mid_filter(hpre, g, N1):
    W = FILT_BLOCKS * LANES
    kb = K_BLOCK
    return pl.pallas_call(
        _midf_body,
        grid=(N1 // kb,),
        in_specs=[
            pl.BlockSpec((FILT_BLOCKS, FFT_N2, kb, LANES), lambda i: (0, 0, i, 0)),
            pl.BlockSpec((kb, 2 * FFT_N2, 2 * FFT_N2), lambda i: (i, 0, 0)),
        ],
        out_specs=pl.BlockSpec((kb, 2 * FFT_N2, W), lambda i: (i, 0, 0)),
        out_shape=jax.ShapeDtypeStruct((N1, 2 * FFT_N2, W), F32),
        compiler_params=_cparams("parallel"),
        name="dft_n2_filter",
    )(hpre, g)


def _mid_body(x_ref, g_ref, gi_ref, h_ref, o_ref, d_ref, y_ref):
    kb = x_ref.shape[3]

    def forward(k):
        z = _unpack_pairs(_gather_rows(x_ref, (0,), k))
        d_ref[k % 2] = _dot(g_ref[k], z)

    def product(k):
        hh = h_ref[k]
        dr, di = d_ref[k % 2, :FFT_N2], d_ref[k % 2, FFT_N2:]
        hr, hi = hh[:FFT_N2], hh[FFT_N2:]
        y_ref[k % 2, :FFT_N2] = (dr * hr - di * hi).astype(BF16)
        y_ref[k % 2, FFT_N2:] = (dr * hi + di * hr).astype(BF16)

    def inverse(k):
        e = _dot(gi_ref[k], y_ref[k % 2])
        _scatter_rows(o_ref, (0,), k, _pack_pairs(e))

    forward(0)
    for k in range(kb):
        if k + 1 < kb:
            forward(k + 1)
        if k > 0:
            inverse(k - 1)
        product(k)
    inverse(kb - 1)


def _mid(x5, g, gi, hspec, order, P, N1):
    kb = K_BLOCK
    blk = pl.BlockSpec((1, CH_BLOCKS, FFT_N2, kb, LANES), lambda i, p: (p, 0, 0, i, 0))
    return pl.pallas_call(
        _mid_body,
        grid=(N1 // kb, P),
        in_specs=[
            blk,
            pl.BlockSpec((kb, 2 * FFT_N2, 2 * FFT_N2), lambda i, p: (i, 0, 0)),
            pl.BlockSpec((kb, 2 * FFT_N2, 2 * FFT_N2), lambda i, p: (i, 0, 0)),
            pl.BlockSpec((kb, 2 * FFT_N2, HYENA_W), lambda i, p: (i, 0, order)),
        ],
        out_specs=blk,
        out_shape=jax.ShapeDtypeStruct((P, CH_BLOCKS, FFT_N2, N1, LANES), PACKED),
        scratch_shapes=[pltpu.VMEM((2, 2 * FFT_N2, HYENA_W), F32),
                        pltpu.VMEM((2, 2 * FFT_N2, HYENA_W), BF16)],
        compiler_params=_cparams("parallel", "parallel"),
        name="dft_n2_conv",
    )(x5, g, gi, hspec)


def _s1inv_body(e_ref, w_ref, src_ref, mul_ref, bias_ref, *rest, chain):
    if chain:
        w1_ref, o_ref, o1_ref, y_ref = rest
    else:
        o_ref, y_ref = rest
    _, nb, n1h, rb, _ = src_ref.shape
    for r in range(rb):
        e = _unpack_pairs(jnp.concatenate([e_ref[0, c, r] for c in range(nb)], axis=1))
        y = _dot(w_ref[...], e)
        for ro in range(2):
            _scatter_rows(y_ref, (ro,), r, y[ro * n1h:(ro + 1) * n1h])
    zz = mul_ref[...] * (y_ref[...] + src_ref[...] * bias_ref[...][None, :, None])
    o_ref[...] = zz
    if chain:
        y_ref[...] = zz
        _s1_body(y_ref, w1_ref, o1_ref)


def _s1inv(e5, w1i, src, src_which, mul, mul_which, bias3, P, N1, w1m=None):
    N1h = N1 // 2
    chain = w1m is not None
    member = lambda which: pl.BlockSpec((None, 2, None, CH_BLOCKS, N1h, R_BLOCK, LANES),
                                        lambda p, j: (which, 0, p, 0, 0, j, 0))
    packed_blk = pl.BlockSpec((1, CH_BLOCKS, R_BLOCK, N1, LANES), lambda p, j: (p, 0, j, 0, 0))
    in_specs = [
        packed_blk,
        pl.BlockSpec((N1, 2 * N1), lambda p, j: (0, 0)),
        member(src_which),
        member(mul_which),
        pl.BlockSpec((CH_BLOCKS, 1, LANES), lambda p, j: (0, 0, 0)),
    ]
    out_specs = [member(0)]
    out_shape = [jax.ShapeDtypeStruct((1, 2, P, CH_BLOCKS, N1h, FFT_N2, LANES), F32)]
    args = [e5, w1i, src, mul, bias3]
    if chain:
        in_specs.append(pl.BlockSpec((2 * N1, N1), lambda p, j: (0, 0)))
        out_specs.append(packed_blk)
        out_shape.append(jax.ShapeDtypeStruct((P, CH_BLOCKS, FFT_N2, N1, LANES), PACKED))
        args.append(w1m)
    return pl.pallas_call(
        functools.partial(_s1inv_body, chain=chain),
        grid=(P, FFT_N2 // R_BLOCK),
        in_specs=in_specs,
        out_specs=out_specs,
        out_shape=out_shape,
        scratch_shapes=[pltpu.VMEM((2, CH_BLOCKS, N1h, R_BLOCK, LANES), F32)],
        compiler_params=_cparams("parallel", "parallel"),
        name="idft_n1_dft_n1" if chain else "idft_n1",
    )(*args)


def _outproj_body(a_ref, ga_ref, z_ref, gh_ref, x_ref, wa_ref, wh_ref, w_ref, fw_ref, o_ref, *, final):
    def norm_gate(y, w, g):
        ms = jnp.mean(y * y, axis=-1, keepdims=True)
        g = g.astype(F32)
        return (y * lax.rsqrt(ms + EPS) * w) * (g * (1.0 / (1.0 + jnp.exp(-g))))

    attn = a_ref[0].reshape(ATTN_W, a_ref.shape[-1]).T
    oa = norm_gate(attn, wa_ref[...], ga_ref[...]).astype(BF16)
    zz = jnp.concatenate([z_ref[0, c] for c in range(CH_BLOCKS)], axis=1)
    oh = norm_gate(zz, wh_ref[...], gh_ref[...]).astype(BF16)
    y = x_ref[...] + (_dot(oa, w_ref[0:ATTN_W, :]) + _dot(oh, w_ref[ATTN_W:MIX_W, :]))
    if final:
        ms = jnp.mean(y * y, axis=-1, keepdims=True)
        y = y * lax.rsqrt(ms + EPS) * fw_ref[...]
    o_ref[...] = y


def _outproj(attn_t, ga, zz4, gh, x2d, wa, wh, w_bf, fw, final, tm=512):
    T = x2d.shape[0]
    nlb = zz4.shape[2] // tm
    row = lambda i: (i, 0)
    const = lambda i: (0, 0)
    body = functools.partial(_outproj_body, final=final)
    return pl.pallas_call(
        body,
        grid=(T // tm,),
        in_specs=[
            pl.BlockSpec((1, N_Q_HEADS, HEAD_DIM, tm), lambda i: (i // nlb, 0, 0, i % nlb)),
            pl.BlockSpec((tm, ATTN_W), row),
            pl.BlockSpec((1, CH_BLOCKS, tm, LANES), lambda i: (i // nlb, 0, i % nlb, 0)),
            pl.BlockSpec((tm, HYENA_W), row),
            pl.BlockSpec((tm, D_MODEL), row),
            pl.BlockSpec((1, ATTN_W), const),
            pl.BlockSpec((1, HYENA_W), const),
            pl.BlockSpec((MIX_W, D_MODEL), const),
            pl.BlockSpec((1, D_MODEL), const),
        ],
        out_specs=pl.BlockSpec((tm, D_MODEL), row),
        out_shape=jax.ShapeDtypeStruct((T, D_MODEL), F32),
        compiler_params=_cparams("parallel"),
        name="outproj",
    )(attn_t, ga, zz4, gh, x2d, wa, wh, w_bf, fw)


def _rope_tables(L):
    t = jnp.arange(L, dtype=jnp.int32)
    pos = jnp.stack([t // GRID_W, t % GRID_W], axis=-1).astype(F32)
    freqs = ROPE_THETA ** (-jnp.arange(ROPE_FREQS, dtype=F32) / ROPE_FREQS)
    ang = pos[:, :, None] * freqs
    cos, sin = jnp.cos(ang), jnp.sin(ang)
    c_head = jnp.stack([cos, cos], axis=2).reshape(L, HEAD_DIM)
    s_head = jnp.stack([-sin, sin], axis=2).reshape(L, HEAD_DIM)
    reps = LANES // HEAD_DIM
    return jnp.tile(c_head, (1, reps)), jnp.tile(s_head, (1, reps))


def _head_mean_matrix():
    idx = jnp.arange(LANES) // HEAD_DIM
    return jnp.where(idx[:, None] == idx[None, :], 1.0 / HEAD_DIM, 0.0).astype(BF16)


def _dft_tables(L):
    N = 2 * L
    N1 = N // FFT_N2
    N1h = N1 // 2
    two_pi = 2.0 * math.pi
    k1 = jnp.arange(N1, dtype=jnp.int32)
    th = ((k1[:, None] * k1[None, :]) % N1).astype(F32) * (two_pi / N1)
    c, s = jnp.cos(th), jnp.sin(th)
    ch, sh = c[:, :N1h], s[:, :N1h]
    w1 = jnp.stack([jnp.concatenate([ch, sh], axis=1),
                    jnp.concatenate([-sh, ch], axis=1)], axis=1).reshape(2 * N1, N1)
    w1f = jnp.stack([c, -s], axis=1).reshape(2 * N1, N1)
    ct, st = ch.T, sh.T
    top = jnp.stack([ct, -st], axis=2).reshape(N1h, 2 * N1)
    bot = jnp.stack([st, ct], axis=2).reshape(N1h, 2 * N1)
    w1i = jnp.concatenate([top, bot], axis=0) * (1.0 / N)
    k2 = jnp.arange(FFT_N2, dtype=jnp.int32)
    freq = k1[:, None, None] + N1 * k2[None, :, None]
    ph = ((freq * k2[None, None, :]) % N).astype(F32) * (two_pi / N)
    cp, sp = jnp.cos(ph), jnp.sin(ph)
    g = jnp.concatenate([jnp.stack([cp, sp], axis=3).reshape(N1, FFT_N2, 2 * FFT_N2),
                         jnp.stack([-sp, cp], axis=3).reshape(N1, FFT_N2, 2 * FFT_N2)], axis=1)
    cpt, spt = jnp.swapaxes(cp, 1, 2), jnp.swapaxes(sp, 1, 2)
    gi = jnp.stack([jnp.concatenate([cpt, -spt], axis=2),
                    jnp.concatenate([spt, cpt], axis=2)], axis=2).reshape(N1, 2 * FFT_N2, 2 * FFT_N2)
    return w1.astype(BF16), w1f.astype(BF16), w1i.astype(BF16), g.astype(BF16), gi.astype(BF16)


def _filter_positions(L):
    m = jnp.arange(2 * L, dtype=jnp.int32)
    pos = jnp.where(m < L, m, 2 * L - m)
    valid = (m != L).astype(F32)[:, None]
    pos = jnp.where(m == L, 0, pos).astype(F32)
    t = pos / (L - 1)
    w = 2.0 * math.pi * pos / L
    bands = jnp.linspace(1e-4, FILT_BANDS - 1, FILT_BANDS, dtype=F32)
    ang = w[:, None] * bands[None, :]
    z = jnp.concatenate([t[:, None], jnp.cos(ang), -jnp.sin(ang)], axis=-1)
    z = jnp.pad(z, ((0, 0), (0, LANES - FILT_EMB)))
    return z, valid


def _pad_to(a, shape):
    return jnp.pad(a, [(0, s - d) for d, s in zip(a.shape, shape)])


def _trunk(x, norm_w, w_in, q_norm_w, k_norm_w, conv_w, conv_b, filt_w1, filt_b1,
           filt_w2, filt_b2, filt_w3, filt_freq, filt_decay, hyena_bias,
           attn_out_norm_w, hyena_out_norm_w, w_out, final_norm_w):
    B, L, _ = x.shape
    assert B % 2 == 0 and L % 1024 == 0
    P = B // 2
    N1 = 2 * L // FFT_N2
    N1h = N1 // 2
    T = B * L

    ctab, stab = _rope_tables(L)
    pmat = _head_mean_matrix()
    w1m, w1f, w1i, gmat, gimat = _dft_tables(L)
    ztab, valid = _filter_positions(L)
    reps = LANES // HEAD_DIM

    x2d = x.reshape(T, D_MODEL)
    for l in range(DEPTH):
        q, k, v, ga, uc, gh = _inproj(
            x2d, norm_w[l][None], w_in[l].astype(BF16),
            jnp.tile(q_norm_w[l], reps)[None], jnp.tile(k_norm_w[l], reps)[None],
            pmat, ctab, stab, conv_w[l], conv_b[l][None], B, L)
        attn = _attention(q, k, v, B, L)

        w3 = filt_w3[l].reshape(FILT_HID, HYENA_ORDER, 2, HYENA_W)
        w3d = _pad_to(jnp.transpose(w3, (2, 0, 1, 3)).reshape(2, FILT_HID, HYENA_ORDER * HYENA_W),
                      (2, LANES, HYENA_ORDER * HYENA_W))
        dec = filt_decay[l].reshape(HYENA_ORDER, 2, HYENA_W)
        decd = jnp.transpose(dec, (1, 0, 2)).reshape(2, 1, HYENA_ORDER * HYENA_W)
        buf, ssum = _filters(
            ztab, valid,
            _pad_to(filt_w1[l], (LANES, LANES)), _pad_to(filt_b1[l][None], (1, LANES)),
            _pad_to(filt_w2[l], (LANES, LANES)), _pad_to(filt_b2[l][None], (1, LANES)),
            _pad_to(filt_freq[l][None], (1, LANES)), w3d, decd, L)
        hpre = _s1_filter(buf.reshape(FILT_BLOCKS, N1, FFT_N2, LANES), ssum, w1f, N1)
        hspec = _mid_filter(hpre, gmat, N1)

        parts = uc.reshape(HYENA_ORDER + 1, 2, P, CH_BLOCKS, N1h, FFT_N2, LANES)
        zz = parts
        o1 = _s1(zz, 0, w1m, P, N1)
        for o in range(HYENA_ORDER):
            e = _mid(o1, gmat, gimat, hspec, o, P, N1)
            bias3 = hyena_bias[l, o].reshape(CH_BLOCKS, 1, LANES)
            if o + 1 < HYENA_ORDER:
                zz, o1 = _s1inv(e, w1i, zz, 0, parts, o + 1, bias3, P, N1, w1m=w1m)
            else:
                zz, = _s1inv(e, w1i, zz, 0, parts, o + 1, bias3, P, N1)
        zz4 = zz.reshape(B, CH_BLOCKS, L, LANES)

        x2d = _outproj(attn, ga, zz4, gh, x2d,
                       attn_out_norm_w[l][None], hyena_out_norm_w[l][None],
                       w_out[l].astype(BF16), final_norm_w[None], final=(l == DEPTH - 1))
    return x2d.reshape(B, L, D_MODEL)


def kernel(x_prompt, x_sample, norm_w, w_in, q_norm_w, k_norm_w, conv_w, conv_b, filt_w1, filt_b1, filt_w2, filt_b2, filt_w3, filt_freq, filt_decay, hyena_bias, attn_out_norm_w, hyena_out_norm_w, w_out, final_norm_w):
    weights = (norm_w, w_in, q_norm_w, k_norm_w, conv_w, conv_b, filt_w1, filt_b1, filt_w2, filt_b2,
               filt_w3, filt_freq, filt_decay, hyena_bias, attn_out_norm_w, hyena_out_norm_w, w_out,
               final_norm_w)
    return (_trunk(x_prompt, *weights), _trunk(x_sample, *weights))
```

```python
import functools
import math

import jax
import jax.numpy as jnp
from jax import lax
from jax.experimental import pallas as pl
from jax.experimental.pallas import tpu as pltpu

F32 = jnp.float32
BF16 = jnp.bfloat16
PACKED = jnp.uint32

D_MODEL = 1024
DEPTH = 4
GRID_W = 64
HEAD_DIM = 64
N_Q_HEADS = 8
N_KV_HEADS = 2
Q_PER_KV = N_Q_HEADS // N_KV_HEADS
ATTN_W = N_Q_HEADS * HEAD_DIM
KV_W = N_KV_HEADS * HEAD_DIM
HYENA_W = 512
HYENA_ORDER = 2
MIX_W = ATTN_W + HYENA_W
SHORT_CONV = 3
FILT_EMB = 33
FILT_BANDS = 16
FILT_HID = 64
N_FILT = HYENA_ORDER * 2 * HYENA_W
MOD_SHIFT = 0.05
ROPE_THETA = 10000.0
ROPE_FREQS = HEAD_DIM // 4
EPS = 1e-6
COL_K = ATTN_W
COL_V = COL_K + KV_W
COL_GA = COL_V + KV_W
COL_U = COL_GA + ATTN_W
COL_GH = COL_U + (HYENA_ORDER + 1) * HYENA_W
D_IN_PROJ = COL_GH + HYENA_W

QK_SCALE = math.log2(math.e) / math.sqrt(HEAD_DIM)
KV_CHUNK = 512
Q_TILE = 512
Q_SUB = 256
V_ROWS = HEAD_DIM + 16
LANES = 128
FFT_N2 = 128
CH_BLOCKS = HYENA_W // LANES
FILT_BLOCKS = HYENA_ORDER * CH_BLOCKS
R_BLOCK = 8
K_BLOCK = 8
VMEM_LIMIT_BYTES = 48 * 1024 * 1024
HIGHEST = lax.Precision.HIGHEST


def _cparams(*sem):
    return pltpu.CompilerParams(dimension_semantics=sem, vmem_limit_bytes=VMEM_LIMIT_BYTES)


def _dot(a, b):
    return jnp.dot(a, b, preferred_element_type=F32)


def _dot_f32(a, b):
    return jnp.dot(a, b, preferred_element_type=F32, precision=HIGHEST)


def _inproj_body(x_ref, xprev_ref, xnext_ref, nw_ref, w_ref, qnw_ref, knw_ref, p_ref, c_ref, s_ref,
                 cw_ref, cb_ref, q_ref, k_ref, v_ref, ga_ref, parts_ref, gh_ref, *, nlb):
    def normed(x):
        ms = jnp.mean(x * x, axis=-1, keepdims=True)
        return (x * lax.rsqrt(ms + EPS) * nw_ref[...]).astype(BF16)

    h = normed(x_ref[...])
    p = p_ref[...]
    c = c_ref[...]
    s = s_ref[...]
    lane = lax.broadcasted_iota(jnp.int32, c.shape, 1)
    first_half = (lane % (2 * ROPE_FREQS)) < ROPE_FREQS

    def norm_rope(y, nw):
        y2 = y * y
        hi = y2.astype(BF16)
        lo = (y2 - hi.astype(F32)).astype(BF16)
        msq = _dot(hi, p) + _dot(lo, p)
        yn = y * lax.rsqrt(msq + EPS) * nw
        partner = jnp.where(first_half, pltpu.roll(yn, LANES - ROPE_FREQS, 1),
                            pltpu.roll(yn, ROPE_FREQS, 1))
        return yn * c + partner * s

    q = _dot(h, w_ref[:, 0:COL_K])
    qnw = qnw_ref[...]
    for j in range(ATTN_W // LANES):
        r = norm_rope(q[:, j * LANES:(j + 1) * LANES], qnw) * QK_SCALE
        rt = r.T
        q_ref[0, 2 * j] = rt[:HEAD_DIM].astype(BF16)
        q_ref[0, 2 * j + 1] = rt[HEAD_DIM:].astype(BF16)
    kv = _dot(h, w_ref[:, COL_K:COL_GA])
    kr = norm_rope(kv[:, :KV_W], knw_ref[...])
    k_ref[0, 0] = kr[:, :HEAD_DIM].astype(BF16)
    k_ref[0, 1] = pltpu.roll(kr, HEAD_DIM, 1)[:, :HEAD_DIM].astype(BF16)
    vt = kv[:, KV_W:].T
    extra = lax.broadcasted_iota(jnp.int32, (V_ROWS - HEAD_DIM, KV_CHUNK), 0)
    ones_row = jnp.where(extra == 0, 1.0, 0.0).astype(BF16)
    for c in range(vt.shape[1] // KV_CHUNK):
        for g in range(N_KV_HEADS):
            v_ref[0, g, c, 0:HEAD_DIM, :] = vt[g * HEAD_DIM:(g + 1) * HEAD_DIM,
                                               c * KV_CHUNK:(c + 1) * KV_CHUNK].astype(BF16)
            v_ref[0, g, c, HEAD_DIM:V_ROWS, :] = ones_row
    ga_ref[...] = _dot(h, w_ref[:, COL_GA:COL_U]).astype(ga_ref.dtype)
    gh_ref[...] = _dot(h, w_ref[:, COL_GH:D_IN_PROJ]).astype(gh_ref.dtype)

    u = _dot(h, w_ref[:, COL_U:COL_GH])
    halo = xprev_ref.shape[0]
    hh = normed(jnp.concatenate([xprev_ref[...], xnext_ref[...]], axis=0))
    uh = _dot(hh, w_ref[:, COL_U:COL_GH])
    li = pl.program_id(0) % nlb
    prev_row = jnp.where(li == 0, 0.0, uh[halo - 1:halo])
    next_row = jnp.where(li == nlb - 1, 0.0, uh[halo:halo + 1])
    tm = u.shape[0]
    row = lax.broadcasted_iota(jnp.int32, u.shape, 0)
    um = jnp.where(row == 0, prev_row, pltpu.roll(u, 1, 0))
    up = jnp.where(row == tm - 1, next_row, pltpu.roll(u, tm - 1, 0))
    cw = cw_ref[...]
    res = um * cw[0:1] + u * cw[1:2] + up * cw[2:3] + cb_ref[...]
    for part in range(HYENA_ORDER + 1):
        for cblk in range(CH_BLOCKS):
            lo = part * HYENA_W + cblk * LANES
            parts_ref[part, 0, cblk] = res[:, lo:lo + LANES]


def _inproj(x2d, nw, w_bf, qnw, knw, pmat, ctab, stab, cw, cb, B, L, tm=512, halo=8):
    T = B * L
    nlb = L // tm
    rpb = tm // halo
    UW = (HYENA_ORDER + 1) * HYENA_W
    row = lambda i: (i, 0)
    const = lambda i: (0, 0)
    pos = lambda i: (i % nlb, 0)
    heads = lambda i: (i // nlb, 0, i % nlb, 0)
    return pl.pallas_call(
        functools.partial(_inproj_body, nlb=nlb),
        grid=(T // tm,),
        in_specs=[
            pl.BlockSpec((tm, D_MODEL), row),
            pl.BlockSpec((halo, D_MODEL), lambda i: (jnp.maximum(i * rpb - 1, 0), 0)),
            pl.BlockSpec((halo, D_MODEL), lambda i: (jnp.minimum((i + 1) * rpb, T // halo - 1), 0)),
            pl.BlockSpec((1, D_MODEL), const),
            pl.BlockSpec((D_MODEL, D_IN_PROJ), const),
            pl.BlockSpec((1, LANES), const),
            pl.BlockSpec((1, LANES), const),
            pl.BlockSpec((LANES, LANES), const),
            pl.BlockSpec((tm, LANES), pos),
            pl.BlockSpec((tm, LANES), pos),
            pl.BlockSpec((SHORT_CONV, UW), const),
            pl.BlockSpec((1, UW), const),
        ],
        out_specs=[
            pl.BlockSpec((1, N_Q_HEADS, HEAD_DIM, tm), lambda i: (i // nlb, 0, 0, i % nlb)),
            pl.BlockSpec((1, N_KV_HEADS, tm, HEAD_DIM), heads),
            pl.BlockSpec((1, N_KV_HEADS, tm // KV_CHUNK, V_ROWS, KV_CHUNK),
                         lambda i: (i // nlb, 0, i % nlb, 0, 0)),
            pl.BlockSpec((tm, ATTN_W), row),
            pl.BlockSpec((HYENA_ORDER + 1, 1, CH_BLOCKS, tm, LANES), lambda i: (0, i // nlb, 0, i % nlb, 0)),
            pl.BlockSpec((tm, HYENA_W), row),
        ],
        out_shape=[
            jax.ShapeDtypeStruct((B, N_Q_HEADS, HEAD_DIM, L), BF16),
            jax.ShapeDtypeStruct((B, N_KV_HEADS, L, HEAD_DIM), BF16),
            jax.ShapeDtypeStruct((B, N_KV_HEADS, L // KV_CHUNK, V_ROWS, KV_CHUNK), BF16),
            jax.ShapeDtypeStruct((T, ATTN_W), BF16),
            jax.ShapeDtypeStruct((HYENA_ORDER + 1, B, CH_BLOCKS, L, LANES), F32),
            jax.ShapeDtypeStruct((T, HYENA_W), BF16),
        ],
        compiler_params=_cparams("parallel"),
        name="inproj",
    )(x2d, x2d, x2d, nw, w_bf, qnw, knw, pmat, ctab, stab, cw, cb)


def _attn_body(qt_ref, k_ref, vt_ref, o_ref, s_ref, p_ref, al_ref, cm_ref, m_ref, acc_ref, *, tq, nk):
    cols = Q_PER_KV * tq
    ncb = cols // LANES
    m_ref[...] = jnp.full(m_ref.shape, -jnp.inf, F32)
    acc_ref[...] = jnp.zeros(acc_ref.shape, F32)

    def scores(j, slot):
        start = pl.multiple_of(j * KV_CHUNK, KV_CHUNK)
        k = k_ref[0, 0, pl.ds(start, KV_CHUNK), :]
        for h in range(Q_PER_KV):
            for q0 in range(0, tq, Q_SUB):
                c0 = h * tq + q0
                r = _dot(k, qt_ref[0, h, :, q0:q0 + Q_SUB])
                for i in range(Q_SUB // LANES):
                    s_ref[slot, c0 // LANES + i] = r[:, i * LANES:(i + 1) * LANES]
                cm_ref[slot, :, c0:c0 + Q_SUB] = jnp.max(r, axis=0, keepdims=True)

    def accumulate(j, slot):
        vt = vt_ref[0, 0, j]
        for i in range(cols // Q_SUB):
            cs = slice(i * Q_SUB, (i + 1) * Q_SUB)
            nb = Q_SUB // LANES
            p = jnp.concatenate([p_ref[slot, nb * i + t] for t in range(nb)], axis=1)
            acc_ref[:, cs] = al_ref[slot, :, cs] * acc_ref[:, cs] + _dot(vt, p)

    def softmax(slot):
        m_old = m_ref[...]
        m_new = jnp.maximum(m_old, cm_ref[slot])
        m_ref[...] = m_new
        al_ref[slot] = jnp.exp2(m_old - m_new)
        for c in range(ncb):
            p_ref[slot, c] = jnp.exp2(s_ref[slot, c] - m_new[:, c * LANES:(c + 1) * LANES]).astype(BF16)

    scores(0, 0)
    scores(1, 1)
    softmax(0)
    scores(2, 0)
    accumulate(0, 0)
    softmax(1)

    def step(jj, carry):
        j = 2 * jj
        scores(j + 1, 1)
        accumulate(j - 1, 1)
        softmax(0)
        scores(jnp.minimum(j + 2, nk - 1), 0)
        accumulate(j, 0)
        softmax(1)
        return carry

    lax.fori_loop(1, nk // 2, step, 0)
    accumulate(nk - 1, 1)
    o = acc_ref[0:HEAD_DIM, :] / acc_ref[HEAD_DIM:HEAD_DIM + 1, :]
    for h in range(Q_PER_KV):
        o_ref[0, h] = o[:, h * tq:(h + 1) * tq]


def _attention(qt, k, vt, B, L):
    tq = Q_TILE
    nk = L // KV_CHUNK
    assert nk % 2 == 0 and nk >= 4
    cols = Q_PER_KV * tq
    ncb = cols // LANES
    body = functools.partial(_attn_body, tq=tq, nk=nk)
    return pl.pallas_call(
        body,
        grid=(B, N_KV_HEADS, L // tq),
        in_specs=[
            pl.BlockSpec((1, Q_PER_KV, HEAD_DIM, tq), lambda b, g, i: (b, g, 0, i)),
            pl.BlockSpec((1, 1, L, HEAD_DIM), lambda b, g, i: (b, g, 0, 0)),
            pl.BlockSpec((1, 1, nk, V_ROWS, KV_CHUNK), lambda b, g, i: (b, g, 0, 0, 0)),
        ],
        out_specs=pl.BlockSpec((1, Q_PER_KV, HEAD_DIM, tq), lambda b, g, i: (b, g, 0, i)),
        out_shape=jax.ShapeDtypeStruct((B, N_Q_HEADS, HEAD_DIM, L), F32),
        scratch_shapes=[
            pltpu.VMEM((2, ncb, KV_CHUNK, LANES), F32),
            pltpu.VMEM((2, ncb, KV_CHUNK, LANES), BF16),
            pltpu.VMEM((2, 1, cols), F32),
            pltpu.VMEM((2, 1, cols), F32),
            pltpu.VMEM((1, cols), F32),
            pltpu.VMEM((V_ROWS, cols), F32),
        ],
        compiler_params=_cparams("parallel", "parallel", "parallel"),
        name="attention",
    )(qt, k, vt)


def _filt_body(z_ref, valid_ref, w1_ref, b1_ref, w2_ref, b2_ref, fr_ref, w3_ref, dec_ref,
               buf_ref, sum_ref):
    i = pl.program_id(0)
    z = z_ref[...]
    fr = fr_ref[...]
    h = jnp.sin(fr * (_dot_f32(z, w1_ref[...]) + b1_ref[...]))
    h = jnp.sin(fr * (_dot_f32(h, w2_ref[...]) + b2_ref[...]))
    h = _dot_f32(h, w3_ref[0])
    t = z[:, 0:1]
    win = jnp.exp(-t * jnp.abs(dec_ref[0])) + MOD_SHIFT
    out = h * win * valid_ref[...]
    for c in range(FILT_BLOCKS):
        buf_ref[c] = out[:, c * LANES:(c + 1) * LANES]

    @pl.when(i == 0)
    def _():
        sum_ref[...] = jnp.zeros_like(sum_ref)

    sum_ref[...] += jnp.sum(jnp.abs(out), axis=0, keepdims=True)


def _filters(ztab, valid, w1, b1, w2, b2, fr, w3d, decd, L, rb=512):
    n = 2 * L
    nb = n // rb
    half = nb // 2
    W = HYENA_ORDER * HYENA_W
    const = lambda i: (0, 0)
    return pl.pallas_call(
        _filt_body,
        grid=(nb,),
        in_specs=[
            pl.BlockSpec((rb, LANES), lambda i: (i, 0)),
            pl.BlockSpec((rb, 1), lambda i: (i, 0)),
            pl.BlockSpec((LANES, LANES), const),
            pl.BlockSpec((1, LANES), const),
            pl.BlockSpec((LANES, LANES), const),
            pl.BlockSpec((1, LANES), const),
            pl.BlockSpec((1, LANES), const),
            pl.BlockSpec((1, LANES, W), lambda i: (i // half, 0, 0)),
            pl.BlockSpec((1, 1, W), lambda i: (i // half, 0, 0)),
        ],
        out_specs=[
            pl.BlockSpec((FILT_BLOCKS, rb, LANES), lambda i: (0, i, 0)),
            pl.BlockSpec((1, W), const),
        ],
        out_shape=[
            jax.ShapeDtypeStruct((FILT_BLOCKS, n, LANES), F32),
            jax.ShapeDtypeStruct((1, W), F32),
        ],
        compiler_params=_cparams("arbitrary"),
        name="filters",
    )(ztab, valid, w1, b1, w2, b2, fr, w3d, decd)


def _strided_rows(ref, lead):
    n = len(lead)
    nblk, count, stride = ref.shape[n:n + 3]
    return ref.reshape(ref.shape[:n + 1] + (count * stride, LANES)), nblk, count, stride


def _gather_rows(ref, lead, r):
    flat, nblk, count, stride = _strided_rows(ref, lead)
    return jnp.concatenate([flat[lead + (c, pl.ds(r, count, stride=stride), slice(None))]
                            for c in range(nblk)], axis=1)


def _scatter_rows(ref, lead, r, val):
    flat, nblk, count, stride = _strided_rows(ref, lead)
    for c in range(nblk):
        flat[lead + (c, pl.ds(r, count, stride=stride), slice(None))] = val[:, c * LANES:(c + 1) * LANES]


def _pack_pairs(y):
    return pltpu.bitcast(y.astype(BF16), PACKED)


def _unpack_pairs(w):
    return pltpu.bitcast(w, BF16)


def _store_spectrum_rows(o_ref, lead, r, y):
    nkb, nblk, _, kb, _ = o_ref.shape[len(lead):]
    for i in range(nkb):
        for c in range(nblk):
            o_ref[lead + (i, c, r)] = y[i * kb:(i + 1) * kb, c * LANES:(c + 1) * LANES]


def _load_spectrum_rows(e_ref, lead, r):
    nkb, nblk = e_ref.shape[len(lead):len(lead) + 2]
    return jnp.concatenate(
        [jnp.concatenate([e_ref[lead + (i, c, r)] for i in range(nkb)], axis=0) for c in range(nblk)], axis=1)


def _s1_body(a_ref, w_ref, o_ref):
    rb = a_ref.shape[3]
    for r in range(rb):
        z = jnp.concatenate([_gather_rows(a_ref, (m,), r) for m in range(2)], axis=0)
        y = _pack_pairs(_dot(w_ref[...], z.astype(BF16)))
        _store_spectrum_rows(o_ref, (0,), r, y)


def _spectrum_spec(N1, nblk, lead_map):
    return pl.BlockSpec(lead_map[0] + (N1 // K_BLOCK, nblk, R_BLOCK, K_BLOCK, LANES), lead_map[1])


def _s1(src, which, w1m, P, N1):
    N1h = N1 // 2
    return pl.pallas_call(
        _s1_body,
        grid=(P, FFT_N2 // R_BLOCK),
        in_specs=[
            pl.BlockSpec((None, 2, None, CH_BLOCKS, N1h, R_BLOCK, LANES), lambda p, j: (which, 0, p, 0, 0, j, 0)),
            pl.BlockSpec((2 * N1, N1), lambda p, j: (0, 0)),
        ],
        out_specs=_spectrum_spec(N1, CH_BLOCKS, ((1,), lambda p, j: (p, 0, 0, j, 0, 0))),
        out_shape=jax.ShapeDtypeStruct((P, N1 // K_BLOCK, CH_BLOCKS, FFT_N2, K_BLOCK, LANES), PACKED),
        compiler_params=_cparams("parallel", "parallel"),
        name="dft_n1",
    )(src, w1m)


def _s1f_body(b_ref, s_ref, w_ref, o_ref):
    rb = b_ref.shape[2]
    for r in range(rb):
        z = (_gather_rows(b_ref, (), r) / s_ref[...]).astype(BF16)
        _store_spectrum_rows(o_ref, (), r, _pack_pairs(_dot(w_ref[...], z)))


def _s1_filter(buf4, ssum, w1f, N1):
    W = FILT_BLOCKS * LANES
    return pl.pallas_call(
        _s1f_body,
        grid=(FFT_N2 // R_BLOCK,),
        in_specs=[
            pl.BlockSpec((FILT_BLOCKS, N1, R_BLOCK, LANES), lambda j: (0, 0, j, 0)),
            pl.BlockSpec((1, W), lambda j: (0, 0)),
            pl.BlockSpec((2 * N1, N1), lambda j: (0, 0)),
        ],
        out_specs=_spectrum_spec(N1, FILT_BLOCKS, ((), lambda j: (0, 0, j, 0, 0))),
        out_shape=jax.ShapeDtypeStruct((N1 // K_BLOCK, FILT_BLOCKS, FFT_N2, K_BLOCK, LANES), PACKED),
        compiler_params=_cparams("parallel"),
        name="dft_n1_filter",
    )(buf4, ssum, w1f)


def _midf_body(x_ref, g_ref, h_ref):
    for k in range(x_ref.shape[3]):
        z = _unpack_pairs(_gather_rows(x_ref, (0,), k))
        h_ref[k] = _dot(g_ref[k], z)


def _mid_filter(hpre, g, N1):
    W = FILT_BLOCKS * LANES
    kb = K_BLOCK
    return pl.pallas_call(
        _midf_body,
        grid=(N1 // kb,),
        in_specs=[
            pl.BlockSpec((1, FILT_BLOCKS, FFT_N2, kb, LANES), lambda i: (i, 0, 0, 0, 0)),
            pl.BlockSpec((kb, 2 * FFT_N2, 2 * FFT_N2), lambda i: (i, 0, 0)),
        ],
        out_specs=pl.BlockSpec((kb, 2 * FFT_N2, W), lambda i: (i, 0, 0)),
        out_shape=jax.ShapeDtypeStruct((N1, 2 * FFT_N2, W), F32),
        compiler_params=_cparams("parallel"),
        name="dft_n2_filter",
    )(hpre, g)


def _mid_body(x_ref, g_ref, gi_ref, h_ref, o_ref, d_ref, y_ref):
    kb = x_ref.shape[4]

    def forward(k):
        z = _unpack_pairs(_gather_rows(x_ref, (0, 0), k))
        d_ref[k % 2] = _dot(g_ref[k], z)

    def product(k):
        hh = h_ref[k]
        dr, di = d_ref[k % 2, :FFT_N2], d_ref[k % 2, FFT_N2:]
        hr, hi = hh[:FFT_N2], hh[FFT_N2:]
        y_ref[k % 2, :FFT_N2] = (dr * hr - di * hi).astype(BF16)
        y_ref[k % 2, FFT_N2:] = (dr * hi + di * hr).astype(BF16)

    def inverse(k):
        e = _dot(gi_ref[k], y_ref[k % 2])
        _scatter_rows(o_ref, (0, 0), k, _pack_pairs(e))

    forward(0)
    for k in range(kb):
        if k + 1 < kb:
            forward(k + 1)
        if k > 0:
            inverse(k - 1)
        product(k)
    inverse(kb - 1)


def _mid(x5, g, gi, hspec, order, P, N1):
    kb = K_BLOCK
    blk = pl.BlockSpec((1, 1, CH_BLOCKS, FFT_N2, kb, LANES), lambda i, p: (p, i, 0, 0, 0, 0))
    return pl.pallas_call(
        _mid_body,
        grid=(N1 // kb, P),
        in_specs=[
            blk,
            pl.BlockSpec((kb, 2 * FFT_N2, 2 * FFT_N2), lambda i, p: (i, 0, 0)),
            pl.BlockSpec((kb, 2 * FFT_N2, 2 * FFT_N2), lambda i, p: (i, 0, 0)),
            pl.BlockSpec((kb, 2 * FFT_N2, HYENA_W), lambda i, p: (i, 0, order)),
        ],
        out_specs=blk,
        out_shape=jax.ShapeDtypeStruct((P, N1 // kb, CH_BLOCKS, FFT_N2, kb, LANES), PACKED),
        scratch_shapes=[pltpu.VMEM((2, 2 * FFT_N2, HYENA_W), F32),
                        pltpu.VMEM((2, 2 * FFT_N2, HYENA_W), BF16)],
        compiler_params=_cparams("parallel", "parallel"),
        name="dft_n2_conv",
    )(x5, g, gi, hspec)


def _s1inv_body(e_ref, w_ref, src_ref, mul_ref, bias_ref, *rest, chain):
    if chain:
        w1_ref, o_ref, o1_ref, y_ref = rest
    else:
        o_ref, y_ref = rest
    _, nb, n1h, rb, _ = src_ref.shape
    for r in range(rb):
        e = _unpack_pairs(_load_spectrum_rows(e_ref, (0,), r))
        y = _dot(w_ref[...], e)
        for ro in range(2):
            _scatter_rows(y_ref, (ro,), r, y[ro * n1h:(ro + 1) * n1h])
    zz = mul_ref[...] * (y_ref[...] + src_ref[...] * bias_ref[...][None, :, None])
    o_ref[...] = zz
    if chain:
        y_ref[...] = zz
        _s1_body(y_ref, w1_ref, o1_ref)


def _s1inv(e5, w1i, src, src_which, mul, mul_which, bias3, P, N1, w1m=None):
    N1h = N1 // 2
    chain = w1m is not None
    member = lambda which: pl.BlockSpec((None, 2, None, CH_BLOCKS, N1h, R_BLOCK, LANES),
                                        lambda p, j: (which, 0, p, 0, 0, j, 0))
    packed_blk = _spectrum_spec(N1, CH_BLOCKS, ((1,), lambda p, j: (p, 0, 0, j, 0, 0)))
    in_specs = [
        packed_blk,
        pl.BlockSpec((N1, 2 * N1), lambda p, j: (0, 0)),
        member(src_which),
        member(mul_which),
        pl.BlockSpec((CH_BLOCKS, 1, LANES), lambda p, j: (0, 0, 0)),
    ]
    out_specs = [member(0)]
    out_shape = [jax.ShapeDtypeStruct((1, 2, P, CH_BLOCKS, N1h, FFT_N2, LANES), F32)]
    args = [e5, w1i, src, mul, bias3]
    if chain:
        in_specs.append(pl.BlockSpec((2 * N1, N1), lambda p, j: (0, 0)))
        out_specs.append(packed_blk)
        out_shape.append(jax.ShapeDtypeStruct((P, N1 // K_BLOCK, CH_BLOCKS, FFT_N2, K_BLOCK, LANES), PACKED))
        args.append(w1m)
    return pl.pallas_call(
        functools.partial(_s1inv_body, chain=chain),
        grid=(P, FFT_N2 // R_BLOCK),
        in_specs=in_specs,
        out_specs=out_specs,
        out_shape=out_shape,
        scratch_shapes=[pltpu.VMEM((2, CH_BLOCKS, N1h, R_BLOCK, LANES), F32)],
        compiler_params=_cparams("parallel", "parallel"),
        name="idft_n1_dft_n1" if chain else "idft_n1",
    )(*args)


def _outproj_body(a_ref, ga_ref, z_ref, gh_ref, x_ref, wa_ref, wh_ref, w_ref, fw_ref, o_ref, *, final):
    def norm_gate(y, w, g):
        ms = jnp.mean(y * y, axis=-1, keepdims=True)
        g = g.astype(F32)
        return (y * lax.rsqrt(ms + EPS) * w) * (g * (1.0 / (1.0 + jnp.exp(-g))))

    attn = a_ref[0].reshape(ATTN_W, a_ref.shape[-1]).T
    oa = norm_gate(attn, wa_ref[...], ga_ref[...]).astype(BF16)
    zz = jnp.concatenate([z_ref[0, c] for c in range(CH_BLOCKS)], axis=1)
    oh = norm_gate(zz, wh_ref[...], gh_ref[...]).astype(BF16)
    y = x_ref[...] + (_dot(oa, w_ref[0:ATTN_W, :]) + _dot(oh, w_ref[ATTN_W:MIX_W, :]))
    if final:
        ms = jnp.mean(y * y, axis=-1, keepdims=True)
        y = y * lax.rsqrt(ms + EPS) * fw_ref[...]
    o_ref[...] = y


def _outproj(attn_t, ga, zz4, gh, x2d, wa, wh, w_bf, fw, final, tm=512):
    T = x2d.shape[0]
    nlb = zz4.shape[2] // tm
    row = lambda i: (i, 0)
    const = lambda i: (0, 0)
    body = functools.partial(_outproj_body, final=final)
    return pl.pallas_call(
        body,
        grid=(T // tm,),
        in_specs=[
            pl.BlockSpec((1, N_Q_HEADS, HEAD_DIM, tm), lambda i: (i // nlb, 0, 0, i % nlb)),
            pl.BlockSpec((tm, ATTN_W), row),
            pl.BlockSpec((1, CH_BLOCKS, tm, LANES), lambda i: (i // nlb, 0, i % nlb, 0)),
            pl.BlockSpec((tm, HYENA_W), row),
            pl.BlockSpec((tm, D_MODEL), row),
            pl.BlockSpec((1, ATTN_W), const),
            pl.BlockSpec((1, HYENA_W), const),
            pl.BlockSpec((MIX_W, D_MODEL), const),
            pl.BlockSpec((1, D_MODEL), const),
        ],
        out_specs=pl.BlockSpec((tm, D_MODEL), row),
        out_shape=jax.ShapeDtypeStruct((T, D_MODEL), F32),
        compiler_params=_cparams("parallel"),
        name="outproj",
    )(attn_t, ga, zz4, gh, x2d, wa, wh, w_bf, fw)


def _rope_tables(L):
    t = jnp.arange(L, dtype=jnp.int32)
    pos = jnp.stack([t // GRID_W, t % GRID_W], axis=-1).astype(F32)
    freqs = ROPE_THETA ** (-jnp.arange(ROPE_FREQS, dtype=F32) / ROPE_FREQS)
    ang = pos[:, :, None] * freqs
    cos, sin = jnp.cos(ang), jnp.sin(ang)
    c_head = jnp.stack([cos, cos], axis=2).reshape(L, HEAD_DIM)
    s_head = jnp.stack([-sin, sin], axis=2).reshape(L, HEAD_DIM)
    reps = LANES // HEAD_DIM
    return jnp.tile(c_head, (1, reps)), jnp.tile(s_head, (1, reps))


def _head_mean_matrix():
    idx = jnp.arange(LANES) // HEAD_DIM
    return jnp.where(idx[:, None] == idx[None, :], 1.0 / HEAD_DIM, 0.0).astype(BF16)


def _dft_tables(L):
    N = 2 * L
    N1 = N // FFT_N2
    N1h = N1 // 2
    two_pi = 2.0 * math.pi
    k1 = jnp.arange(N1, dtype=jnp.int32)
    th = ((k1[:, None] * k1[None, :]) % N1).astype(F32) * (two_pi / N1)
    c, s = jnp.cos(th), jnp.sin(th)
    ch, sh = c[:, :N1h], s[:, :N1h]
    w1 = jnp.stack([jnp.concatenate([ch, sh], axis=1),
                    jnp.concatenate([-sh, ch], axis=1)], axis=1).reshape(2 * N1, N1)
    w1f = jnp.stack([c, -s], axis=1).reshape(2 * N1, N1)
    ct, st = ch.T, sh.T
    top = jnp.stack([ct, -st], axis=2).reshape(N1h, 2 * N1)
    bot = jnp.stack([st, ct], axis=2).reshape(N1h, 2 * N1)
    w1i = jnp.concatenate([top, bot], axis=0) * (1.0 / N)
    k2 = jnp.arange(FFT_N2, dtype=jnp.int32)
    freq = k1[:, None, None] + N1 * k2[None, :, None]
    ph = ((freq * k2[None, None, :]) % N).astype(F32) * (two_pi / N)
    cp, sp = jnp.cos(ph), jnp.sin(ph)
    g = jnp.concatenate([jnp.stack([cp, sp], axis=3).reshape(N1, FFT_N2, 2 * FFT_N2),
                         jnp.stack([-sp, cp], axis=3).reshape(N1, FFT_N2, 2 * FFT_N2)], axis=1)
    cpt, spt = jnp.swapaxes(cp, 1, 2), jnp.swapaxes(sp, 1, 2)
    gi = jnp.stack([jnp.concatenate([cpt, -spt], axis=2),
                    jnp.concatenate([spt, cpt], axis=2)], axis=2).reshape(N1, 2 * FFT_N2, 2 * FFT_N2)
    return w1.astype(BF16), w1f.astype(BF16), w1i.astype(BF16), g.astype(BF16), gi.astype(BF16)


def _filter_positions(L):
    m = jnp.arange(2 * L, dtype=jnp.int32)
    pos = jnp.where(m < L, m, 2 * L - m)
    valid = (m != L).astype(F32)[:, None]
    pos = jnp.where(m == L, 0, pos).astype(F32)
    t = pos / (L - 1)
    w = 2.0 * math.pi * pos / L
    bands = jnp.linspace(1e-4, FILT_BANDS - 1, FILT_BANDS, dtype=F32)
    ang = w[:, None] * bands[None, :]
    z = jnp.concatenate([t[:, None], jnp.cos(ang), -jnp.sin(ang)], axis=-1)
    z = jnp.pad(z, ((0, 0), (0, LANES - FILT_EMB)))
    return z, valid


def _pad_to(a, shape):
    return jnp.pad(a, [(0, s - d) for d, s in zip(a.shape, shape)])


def _trunk(x, norm_w, w_in, q_norm_w, k_norm_w, conv_w, conv_b, filt_w1, filt_b1,
           filt_w2, filt_b2, filt_w3, filt_freq, filt_decay, hyena_bias,
           attn_out_norm_w, hyena_out_norm_w, w_out, final_norm_w):
    B, L, _ = x.shape
    assert B % 2 == 0 and L % 1024 == 0
    P = B // 2
    N1 = 2 * L // FFT_N2
    N1h = N1 // 2
    T = B * L

    ctab, stab = _rope_tables(L)
    pmat = _head_mean_matrix()
    w1m, w1f, w1i, gmat, gimat = _dft_tables(L)
    ztab, valid = _filter_positions(L)
    reps = LANES // HEAD_DIM

    x2d = x.reshape(T, D_MODEL)
    for l in range(DEPTH):
        q, k, v, ga, uc, gh = _inproj(
            x2d, norm_w[l][None], w_in[l].astype(BF16),
            jnp.tile(q_norm_w[l], reps)[None], jnp.tile(k_norm_w[l], reps)[None],
            pmat, ctab, stab, conv_w[l], conv_b[l][None], B, L)
        attn = _attention(q, k, v, B, L)

        w3 = filt_w3[l].reshape(FILT_HID, HYENA_ORDER, 2, HYENA_W)
        w3d = _pad_to(jnp.transpose(w3, (2, 0, 1, 3)).reshape(2, FILT_HID, HYENA_ORDER * HYENA_W),
                      (2, LANES, HYENA_ORDER * HYENA_W))
        dec = filt_decay[l].reshape(HYENA_ORDER, 2, HYENA_W)
        decd = jnp.transpose(dec, (1, 0, 2)).reshape(2, 1, HYENA_ORDER * HYENA_W)
        buf, ssum = _filters(
            ztab, valid,
            _pad_to(filt_w1[l], (LANES, LANES)), _pad_to(filt_b1[l][None], (1, LANES)),
            _pad_to(filt_w2[l], (LANES, LANES)), _pad_to(filt_b2[l][None], (1, LANES)),
            _pad_to(filt_freq[l][None], (1, LANES)), w3d, decd, L)
        hpre = _s1_filter(buf.reshape(FILT_BLOCKS, N1, FFT_N2, LANES), ssum, w1f, N1)
        hspec = _mid_filter(hpre, gmat, N1)

        parts = uc.reshape(HYENA_ORDER + 1, 2, P, CH_BLOCKS, N1h, FFT_N2, LANES)
        zz = parts
        o1 = _s1(zz, 0, w1m, P, N1)
        for o in range(HYENA_ORDER):
            e = _mid(o1, gmat, gimat, hspec, o, P, N1)
            bias3 = hyena_bias[l, o].reshape(CH_BLOCKS, 1, LANES)
            if o + 1 < HYENA_ORDER:
                zz, o1 = _s1inv(e, w1i, zz, 0, parts, o + 1, bias3, P, N1, w1m=w1m)
            else:
                zz, = _s1inv(e, w1i, zz, 0, parts, o + 1, bias3, P, N1)
        zz4 = zz.reshape(B, CH_BLOCKS, L, LANES)

        x2d = _outproj(attn, ga, zz4, gh, x2d,
                       attn_out_norm_w[l][None], hyena_out_norm_w[l][None],
                       w_out[l].astype(BF16), final_norm_w[None], final=(l == DEPTH - 1))
    return x2d.reshape(B, L, D_MODEL)


def kernel(x_prompt, x_sample, norm_w, w_in, q_norm_w, k_norm_w, conv_w, conv_b, filt_w1, filt_b1, filt_w2, filt_b2, filt_w3, filt_freq, filt_decay, hyena_bias, attn_out_norm_w, hyena_out_norm_w, w_out, final_norm_w):
    weights = (norm_w, w_in, q_norm_w, k_norm_w, conv_w, conv_b, filt_w1, filt_b1, filt_w2, filt_b2,
               filt_w3, filt_freq, filt_decay, hyena_bias, attn_out_norm_w, hyena_out_norm_w, w_out,
               final_norm_w)
    return (_trunk(x_prompt, *weights), _trunk(x_sample, *weights))
```

```python
import functools
import math

import jax
import jax.numpy as jnp
from jax import lax
from jax.experimental import pallas as pl
from jax.experimental.pallas import tpu as pltpu

F32 = jnp.float32
BF16 = jnp.bfloat16
PACKED = jnp.uint32

D_MODEL = 1024
DEPTH = 4
GRID_W = 64
HEAD_DIM = 64
N_Q_HEADS = 8
N_KV_HEADS = 2
Q_PER_KV = N_Q_HEADS // N_KV_HEADS
ATTN_W = N_Q_HEADS * HEAD_DIM
KV_W = N_KV_HEADS * HEAD_DIM
HYENA_W = 512
HYENA_ORDER = 2
MIX_W = ATTN_W + HYENA_W
SHORT_CONV = 3
FILT_EMB = 33
FILT_BANDS = 16
FILT_HID = 64
N_FILT = HYENA_ORDER * 2 * HYENA_W
MOD_SHIFT = 0.05
ROPE_THETA = 10000.0
ROPE_FREQS = HEAD_DIM // 4
EPS = 1e-6
COL_K = ATTN_W
COL_V = COL_K + KV_W
COL_GA = COL_V + KV_W
COL_U = COL_GA + ATTN_W
COL_GH = COL_U + (HYENA_ORDER + 1) * HYENA_W
D_IN_PROJ = COL_GH + HYENA_W

QK_SCALE = math.log2(math.e) / math.sqrt(HEAD_DIM)
KV_CHUNK = 512
Q_TILE = 512
Q_SUB = 256
V_ROWS = HEAD_DIM + 16
LANES = 128
FFT_N2 = 128
CH_BLOCKS = HYENA_W // LANES
FILT_BLOCKS = HYENA_ORDER * CH_BLOCKS
R_BLOCK = 8
K_BLOCK = 8
VMEM_LIMIT_BYTES = 48 * 1024 * 1024
HIGHEST = lax.Precision.HIGHEST


def _cparams(*sem):
    return pltpu.CompilerParams(dimension_semantics=sem, vmem_limit_bytes=VMEM_LIMIT_BYTES)


def _dot(a, b):
    return jnp.dot(a, b, preferred_element_type=F32)


def _dot_f32(a, b):
    return jnp.dot(a, b, preferred_element_type=F32, precision=HIGHEST)


def _inproj_body(x_ref, xprev_ref, xnext_ref, nw_ref, w_ref, qnw_ref, knw_ref, p_ref, c_ref, s_ref,
                 cw_ref, cb_ref, q_ref, k_ref, v_ref, ga_ref, parts_ref, gh_ref, *, nlb):
    def normed(x):
        ms = jnp.mean(x * x, axis=-1, keepdims=True)
        return (x * lax.rsqrt(ms + EPS) * nw_ref[...]).astype(BF16)

    h = normed(x_ref[...])
    p = p_ref[...]
    c = c_ref[...]
    s = s_ref[...]
    lane = lax.broadcasted_iota(jnp.int32, c.shape, 1)
    first_half = (lane % (2 * ROPE_FREQS)) < ROPE_FREQS

    def norm_rope(y, nw):
        y2 = y * y
        hi = y2.astype(BF16)
        lo = (y2 - hi.astype(F32)).astype(BF16)
        msq = _dot(hi, p) + _dot(lo, p)
        yn = y * lax.rsqrt(msq + EPS) * nw
        partner = jnp.where(first_half, pltpu.roll(yn, LANES - ROPE_FREQS, 1),
                            pltpu.roll(yn, ROPE_FREQS, 1))
        return yn * c + partner * s

    q = _dot(h, w_ref[:, 0:COL_K])
    qnw = qnw_ref[...]
    for j in range(ATTN_W // LANES):
        r = norm_rope(q[:, j * LANES:(j + 1) * LANES], qnw) * QK_SCALE
        rt = r.T
        q_ref[0, 2 * j] = rt[:HEAD_DIM].astype(BF16)
        q_ref[0, 2 * j + 1] = rt[HEAD_DIM:].astype(BF16)
    kv = _dot(h, w_ref[:, COL_K:COL_GA])
    kr = norm_rope(kv[:, :KV_W], knw_ref[...])
    k_ref[0, 0] = kr[:, :HEAD_DIM].astype(BF16)
    k_ref[0, 1] = pltpu.roll(kr, HEAD_DIM, 1)[:, :HEAD_DIM].astype(BF16)
    vt = kv[:, KV_W:].T
    extra = lax.broadcasted_iota(jnp.int32, (V_ROWS - HEAD_DIM, KV_CHUNK), 0)
    ones_row = jnp.where(extra == 0, 1.0, 0.0).astype(BF16)
    for c in range(vt.shape[1] // KV_CHUNK):
        for g in range(N_KV_HEADS):
            v_ref[0, g, c, 0:HEAD_DIM, :] = vt[g * HEAD_DIM:(g + 1) * HEAD_DIM,
                                               c * KV_CHUNK:(c + 1) * KV_CHUNK].astype(BF16)
            v_ref[0, g, c, HEAD_DIM:V_ROWS, :] = ones_row
    ga_ref[...] = _dot(h, w_ref[:, COL_GA:COL_U]).astype(ga_ref.dtype)
    gh_ref[...] = _dot(h, w_ref[:, COL_GH:D_IN_PROJ]).astype(gh_ref.dtype)

    u = _dot(h, w_ref[:, COL_U:COL_GH])
    halo = xprev_ref.shape[0]
    hh = normed(jnp.concatenate([xprev_ref[...], xnext_ref[...]], axis=0))
    uh = _dot(hh, w_ref[:, COL_U:COL_GH])
    li = pl.program_id(0) % nlb
    prev_row = jnp.where(li == 0, 0.0, uh[halo - 1:halo])
    next_row = jnp.where(li == nlb - 1, 0.0, uh[halo:halo + 1])
    tm = u.shape[0]
    row = lax.broadcasted_iota(jnp.int32, u.shape, 0)
    um = jnp.where(row == 0, prev_row, pltpu.roll(u, 1, 0))
    up = jnp.where(row == tm - 1, next_row, pltpu.roll(u, tm - 1, 0))
    cw = cw_ref[...]
    res = um * cw[0:1] + u * cw[1:2] + up * cw[2:3] + cb_ref[...]
    for part in range(HYENA_ORDER + 1):
        for cblk in range(CH_BLOCKS):
            lo = part * HYENA_W + cblk * LANES
            parts_ref[part, 0, cblk] = res[:, lo:lo + LANES]


def _inproj(x2d, nw, w_bf, qnw, knw, pmat, ctab, stab, cw, cb, B, L, tm=512, halo=8):
    T = B * L
    nlb = L // tm
    rpb = tm // halo
    UW = (HYENA_ORDER + 1) * HYENA_W
    row = lambda i: (i, 0)
    const = lambda i: (0, 0)
    pos = lambda i: (i % nlb, 0)
    heads = lambda i: (i // nlb, 0, i % nlb, 0)
    return pl.pallas_call(
        functools.partial(_inproj_body, nlb=nlb),
        grid=(T // tm,),
        in_specs=[
            pl.BlockSpec((tm, D_MODEL), row),
            pl.BlockSpec((halo, D_MODEL), lambda i: (jnp.maximum(i * rpb - 1, 0), 0)),
            pl.BlockSpec((halo, D_MODEL), lambda i: (jnp.minimum((i + 1) * rpb, T // halo - 1), 0)),
            pl.BlockSpec((1, D_MODEL), const),
            pl.BlockSpec((D_MODEL, D_IN_PROJ), const),
            pl.BlockSpec((1, LANES), const),
            pl.BlockSpec((1, LANES), const),
            pl.BlockSpec((LANES, LANES), const),
            pl.BlockSpec((tm, LANES), pos),
            pl.BlockSpec((tm, LANES), pos),
            pl.BlockSpec((SHORT_CONV, UW), const),
            pl.BlockSpec((1, UW), const),
        ],
        out_specs=[
            pl.BlockSpec((1, N_Q_HEADS, HEAD_DIM, tm), lambda i: (i // nlb, 0, 0, i % nlb)),
            pl.BlockSpec((1, N_KV_HEADS, tm, HEAD_DIM), heads),
            pl.BlockSpec((1, N_KV_HEADS, tm // KV_CHUNK, V_ROWS, KV_CHUNK),
                         lambda i: (i // nlb, 0, i % nlb, 0, 0)),
            pl.BlockSpec((tm, ATTN_W), row),
            pl.BlockSpec((HYENA_ORDER + 1, 1, CH_BLOCKS, tm, LANES), lambda i: (0, i // nlb, 0, i % nlb, 0)),
            pl.BlockSpec((tm, HYENA_W), row),
        ],
        out_shape=[
            jax.ShapeDtypeStruct((B, N_Q_HEADS, HEAD_DIM, L), BF16),
            jax.ShapeDtypeStruct((B, N_KV_HEADS, L, HEAD_DIM), BF16),
            jax.ShapeDtypeStruct((B, N_KV_HEADS, L // KV_CHUNK, V_ROWS, KV_CHUNK), BF16),
            jax.ShapeDtypeStruct((T, ATTN_W), BF16),
            jax.ShapeDtypeStruct((HYENA_ORDER + 1, B, CH_BLOCKS, L, LANES), F32),
            jax.ShapeDtypeStruct((T, HYENA_W), BF16),
        ],
        compiler_params=_cparams("parallel"),
        name="inproj",
    )(x2d, x2d, x2d, nw, w_bf, qnw, knw, pmat, ctab, stab, cw, cb)


def _attn_body(qt_ref, k_ref, vt_ref, o_ref, s_ref, p_ref, al_ref, cm_ref, m_ref, acc_ref, *, tq, nk):
    cols = Q_PER_KV * tq
    ncb = cols // LANES
    m_ref[...] = jnp.full(m_ref.shape, -jnp.inf, F32)
    acc_ref[...] = jnp.zeros(acc_ref.shape, F32)

    def scores(j, slot):
        start = pl.multiple_of(j * KV_CHUNK, KV_CHUNK)
        k = k_ref[0, 0, pl.ds(start, KV_CHUNK), :]
        for h in range(Q_PER_KV):
            for q0 in range(0, tq, Q_SUB):
                c0 = h * tq + q0
                r = _dot(k, qt_ref[0, h, :, q0:q0 + Q_SUB])
                for i in range(Q_SUB // LANES):
                    s_ref[slot, c0 // LANES + i] = r[:, i * LANES:(i + 1) * LANES]
                cm_ref[slot, :, c0:c0 + Q_SUB] = jnp.max(r, axis=0, keepdims=True)

    def accumulate(j, slot):
        vt = vt_ref[0, 0, j]
        for i in range(cols // Q_SUB):
            cs = slice(i * Q_SUB, (i + 1) * Q_SUB)
            nb = Q_SUB // LANES
            p = jnp.concatenate([p_ref[slot, nb * i + t] for t in range(nb)], axis=1)
            acc_ref[:, cs] = al_ref[slot, :, cs] * acc_ref[:, cs] + _dot(vt, p)

    def softmax(slot):
        m_old = m_ref[...]
        m_new = jnp.maximum(m_old, cm_ref[slot])
        m_ref[...] = m_new
        al_ref[slot] = jnp.exp2(m_old - m_new)
        for c in range(ncb):
            p_ref[slot, c] = jnp.exp2(s_ref[slot, c] - m_new[:, c * LANES:(c + 1) * LANES]).astype(BF16)

    scores(0, 0)
    scores(1, 1)
    softmax(0)
    scores(2, 0)
    accumulate(0, 0)
    softmax(1)

    def step(jj, carry):
        j = 2 * jj
        scores(j + 1, 1)
        accumulate(j - 1, 1)
        softmax(0)
        scores(jnp.minimum(j + 2, nk - 1), 0)
        accumulate(j, 0)
        softmax(1)
        return carry

    lax.fori_loop(1, nk // 2, step, 0)
    accumulate(nk - 1, 1)
    o = acc_ref[0:HEAD_DIM, :] / acc_ref[HEAD_DIM:HEAD_DIM + 1, :]
    for h in range(Q_PER_KV):
        o_ref[0, h] = o[:, h * tq:(h + 1) * tq]


def _attention(qt, k, vt, B, L):
    tq = Q_TILE
    nk = L // KV_CHUNK
    assert nk % 2 == 0 and nk >= 4
    cols = Q_PER_KV * tq
    ncb = cols // LANES
    body = functools.partial(_attn_body, tq=tq, nk=nk)
    return pl.pallas_call(
        body,
        grid=(B, N_KV_HEADS, L // tq),
        in_specs=[
            pl.BlockSpec((1, Q_PER_KV, HEAD_DIM, tq), lambda b, g, i: (b, g, 0, i)),
            pl.BlockSpec((1, 1, L, HEAD_DIM), lambda b, g, i: (b, g, 0, 0)),
            pl.BlockSpec((1, 1, nk, V_ROWS, KV_CHUNK), lambda b, g, i: (b, g, 0, 0, 0)),
        ],
        out_specs=pl.BlockSpec((1, Q_PER_KV, HEAD_DIM, tq), lambda b, g, i: (b, g, 0, i)),
        out_shape=jax.ShapeDtypeStruct((B, N_Q_HEADS, HEAD_DIM, L), F32),
        scratch_shapes=[
            pltpu.VMEM((2, ncb, KV_CHUNK, LANES), F32),
            pltpu.VMEM((2, ncb, KV_CHUNK, LANES), BF16),
            pltpu.VMEM((2, 1, cols), F32),
            pltpu.VMEM((2, 1, cols), F32),
            pltpu.VMEM((1, cols), F32),
            pltpu.VMEM((V_ROWS, cols), F32),
        ],
        compiler_params=_cparams("parallel", "parallel", "parallel"),
        name="attention",
    )(qt, k, vt)


def _filt_body(z_ref, valid_ref, w1_ref, b1_ref, w2_ref, b2_ref, fr_ref, w3_ref, dec_ref,
               buf_ref, sum_ref):
    i = pl.program_id(0)
    z = z_ref[...]
    fr = fr_ref[...]
    h = jnp.sin(fr * (_dot_f32(z, w1_ref[...]) + b1_ref[...]))
    h = jnp.sin(fr * (_dot_f32(h, w2_ref[...]) + b2_ref[...]))
    h = _dot_f32(h, w3_ref[0])
    t = z[:, 0:1]
    win = jnp.exp(-t * jnp.abs(dec_ref[0])) + MOD_SHIFT
    out = h * win * valid_ref[...]
    for c in range(FILT_BLOCKS):
        buf_ref[c] = out[:, c * LANES:(c + 1) * LANES]

    @pl.when(i == 0)
    def _():
        sum_ref[...] = jnp.zeros_like(sum_ref)

    sum_ref[...] += jnp.sum(jnp.abs(out), axis=0, keepdims=True)


def _filters(ztab, valid, w1, b1, w2, b2, fr, w3d, decd, L, rb=512):
    n = 2 * L
    nb = n // rb
    half = nb // 2
    W = HYENA_ORDER * HYENA_W
    const = lambda i: (0, 0)
    return pl.pallas_call(
        _filt_body,
        grid=(nb,),
        in_specs=[
            pl.BlockSpec((rb, LANES), lambda i: (i, 0)),
            pl.BlockSpec((rb, 1), lambda i: (i, 0)),
            pl.BlockSpec((LANES, LANES), const),
            pl.BlockSpec((1, LANES), const),
            pl.BlockSpec((LANES, LANES), const),
            pl.BlockSpec((1, LANES), const),
            pl.BlockSpec((1, LANES), const),
            pl.BlockSpec((1, LANES, W), lambda i: (i // half, 0, 0)),
            pl.BlockSpec((1, 1, W), lambda i: (i // half, 0, 0)),
        ],
        out_specs=[
            pl.BlockSpec((FILT_BLOCKS, rb, LANES), lambda i: (0, i, 0)),
            pl.BlockSpec((1, W), const),
        ],
        out_shape=[
            jax.ShapeDtypeStruct((FILT_BLOCKS, n, LANES), F32),
            jax.ShapeDtypeStruct((1, W), F32),
        ],
        compiler_params=_cparams("arbitrary"),
        name="filters",
    )(ztab, valid, w1, b1, w2, b2, fr, w3d, decd)


def _strided_rows(ref, lead):
    n = len(lead)
    nblk, count, stride = ref.shape[n:n + 3]
    return ref.reshape(ref.shape[:n + 1] + (count * stride, LANES)), nblk, count, stride


def _gather_rows(ref, lead, r):
    flat, nblk, count, stride = _strided_rows(ref, lead)
    return jnp.concatenate([flat[lead + (c, pl.ds(r, count, stride=stride), slice(None))]
                            for c in range(nblk)], axis=1)


def _scatter_rows(ref, lead, r, val):
    flat, nblk, count, stride = _strided_rows(ref, lead)
    for c in range(nblk):
        flat[lead + (c, pl.ds(r, count, stride=stride), slice(None))] = val[:, c * LANES:(c + 1) * LANES]


def _pack_pairs(y):
    return pltpu.bitcast(y.astype(BF16), PACKED)


def _unpack_pairs(w):
    return pltpu.bitcast(w, BF16)


def _tile_row(ref, lead, k):
    return jnp.concatenate([ref[lead + (c, k)] for c in range(ref.shape[len(lead)])], axis=1)


def _s1_body(a_ref, w_ref, o_ref):
    rb = a_ref.shape[3]
    for r in range(rb):
        z = jnp.concatenate([_gather_rows(a_ref, (m,), r) for m in range(2)], axis=0)
        y = _pack_pairs(_dot(w_ref[...], z.astype(BF16)))
        _scatter_rows(o_ref, (0,), r, y)


def _s1(src, which, w1m, P, N1):
    N1h = N1 // 2
    return pl.pallas_call(
        _s1_body,
        grid=(P, FFT_N2 // R_BLOCK),
        in_specs=[
            pl.BlockSpec((None, 2, None, CH_BLOCKS, N1h, R_BLOCK, LANES), lambda p, j: (which, 0, p, 0, 0, j, 0)),
            pl.BlockSpec((2 * N1, N1), lambda p, j: (0, 0)),
        ],
        out_specs=pl.BlockSpec((1, CH_BLOCKS, N1, R_BLOCK, LANES), lambda p, j: (p, 0, 0, j, 0)),
        out_shape=jax.ShapeDtypeStruct((P, CH_BLOCKS, N1, FFT_N2, LANES), PACKED),
        compiler_params=_cparams("parallel", "parallel"),
        name="dft_n1",
    )(src, w1m)


def _s1f_body(b_ref, s_ref, w_ref, o_ref):
    rb = b_ref.shape[2]
    for r in range(rb):
        z = (_gather_rows(b_ref, (), r) / s_ref[...]).astype(BF16)
        _scatter_rows(o_ref, (), r, _pack_pairs(_dot(w_ref[...], z)))


def _s1_filter(buf4, ssum, w1f, N1):
    W = FILT_BLOCKS * LANES
    return pl.pallas_call(
        _s1f_body,
        grid=(FFT_N2 // R_BLOCK,),
        in_specs=[
            pl.BlockSpec((FILT_BLOCKS, N1, R_BLOCK, LANES), lambda j: (0, 0, j, 0)),
            pl.BlockSpec((1, W), lambda j: (0, 0)),
            pl.BlockSpec((2 * N1, N1), lambda j: (0, 0)),
        ],
        out_specs=pl.BlockSpec((FILT_BLOCKS, N1, R_BLOCK, LANES), lambda j: (0, 0, j, 0)),
        out_shape=jax.ShapeDtypeStruct((FILT_BLOCKS, N1, FFT_N2, LANES), PACKED),
        compiler_params=_cparams("parallel"),
        name="dft_n1_filter",
    )(buf4, ssum, w1f)


def _midf_body(x_ref, g_ref, h_ref):
    for k in range(x_ref.shape[1]):
        z = _unpack_pairs(_tile_row(x_ref, (), k))
        h_ref[k] = _dot(g_ref[k], z)


def _mid_filter(hpre, g, N1):
    W = FILT_BLOCKS * LANES
    kb = K_BLOCK
    return pl.pallas_call(
        _midf_body,
        grid=(N1 // kb,),
        in_specs=[
            pl.BlockSpec((FILT_BLOCKS, kb, FFT_N2, LANES), lambda i: (0, i, 0, 0)),
            pl.BlockSpec((kb, 2 * FFT_N2, 2 * FFT_N2), lambda i: (i, 0, 0)),
        ],
        out_specs=pl.BlockSpec((kb, 2 * FFT_N2, W), lambda i: (i, 0, 0)),
        out_shape=jax.ShapeDtypeStruct((N1, 2 * FFT_N2, W), F32),
        compiler_params=_cparams("parallel"),
        name="dft_n2_filter",
    )(hpre, g)


def _mid_body(x_ref, g_ref, gi_ref, h_ref, o_ref, d_ref, y_ref):
    kb = x_ref.shape[2]

    def forward(k):
        z = _unpack_pairs(_tile_row(x_ref, (0,), k))
        d_ref[k % 2] = _dot(g_ref[k], z)

    def product(k):
        hh = h_ref[k]
        dr, di = d_ref[k % 2, :FFT_N2], d_ref[k % 2, FFT_N2:]
        hr, hi = hh[:FFT_N2], hh[FFT_N2:]
        y_ref[k % 2, :FFT_N2] = (dr * hr - di * hi).astype(BF16)
        y_ref[k % 2, FFT_N2:] = (dr * hi + di * hr).astype(BF16)

    def inverse(k):
        e = _dot(gi_ref[k], y_ref[k % 2])
        ew = _pack_pairs(e)
        for c in range(o_ref.shape[1]):
            o_ref[0, c, k] = ew[:, c * LANES:(c + 1) * LANES]

    forward(0)
    for k in range(kb):
        if k + 1 < kb:
            forward(k + 1)
        if k > 0:
            inverse(k - 1)
        product(k)
    inverse(kb - 1)


def _mid(x5, g, gi, hspec, order, P, N1):
    kb = K_BLOCK
    blk = pl.BlockSpec((1, CH_BLOCKS, kb, FFT_N2, LANES), lambda i, p: (p, 0, i, 0, 0))
    return pl.pallas_call(
        _mid_body,
        grid=(N1 // kb, P),
        in_specs=[
            blk,
            pl.BlockSpec((kb, 2 * FFT_N2, 2 * FFT_N2), lambda i, p: (i, 0, 0)),
            pl.BlockSpec((kb, 2 * FFT_N2, 2 * FFT_N2), lambda i, p: (i, 0, 0)),
            pl.BlockSpec((kb, 2 * FFT_N2, HYENA_W), lambda i, p: (i, 0, order)),
        ],
        out_specs=blk,
        out_shape=jax.ShapeDtypeStruct((P, CH_BLOCKS, N1, FFT_N2, LANES), PACKED),
        scratch_shapes=[pltpu.VMEM((2, 2 * FFT_N2, HYENA_W), F32),
                        pltpu.VMEM((2, 2 * FFT_N2, HYENA_W), BF16)],
        compiler_params=_cparams("parallel", "parallel"),
        name="dft_n2_conv",
    )(x5, g, gi, hspec)


def _s1inv_body(e_ref, w_ref, src_ref, mul_ref, bias_ref, *rest, chain):
    if chain:
        w1_ref, o_ref, o1_ref, y_ref = rest
    else:
        o_ref, y_ref = rest
    _, nb, n1h, rb, _ = src_ref.shape
    for r in range(rb):
        e = _unpack_pairs(_gather_rows(e_ref, (0,), r))
        y = _dot(w_ref[...], e)
        for ro in range(2):
            _scatter_rows(y_ref, (ro,), r, y[ro * n1h:(ro + 1) * n1h])
    zz = mul_ref[...] * (y_ref[...] + src_ref[...] * bias_ref[...][None, :, None])
    o_ref[...] = zz
    if chain:
        y_ref[...] = zz
        _s1_body(y_ref, w1_ref, o1_ref)


def _s1inv(e5, w1i, src, src_which, mul, mul_which, bias3, P, N1, w1m=None):
    N1h = N1 // 2
    chain = w1m is not None
    member = lambda which: pl.BlockSpec((None, 2, None, CH_BLOCKS, N1h, R_BLOCK, LANES),
                                        lambda p, j: (which, 0, p, 0, 0, j, 0))
    packed_blk = pl.BlockSpec((1, CH_BLOCKS, N1, R_BLOCK, LANES), lambda p, j: (p, 0, 0, j, 0))
    in_specs = [
        packed_blk,
        pl.BlockSpec((N1, 2 * N1), lambda p, j: (0, 0)),
        member(src_which),
        member(mul_which),
        pl.BlockSpec((CH_BLOCKS, 1, LANES), lambda p, j: (0, 0, 0)),
    ]
    out_specs = [member(0)]
    out_shape = [jax.ShapeDtypeStruct((1, 2, P, CH_BLOCKS, N1h, FFT_N2, LANES), F32)]
    args = [e5, w1i, src, mul, bias3]
    if chain:
        in_specs.append(pl.BlockSpec((2 * N1, N1), lambda p, j: (0, 0)))
        out_specs.append(packed_blk)
        out_shape.append(jax.ShapeDtypeStruct((P, CH_BLOCKS, N1, FFT_N2, LANES), PACKED))
        args.append(w1m)
    return pl.pallas_call(
        functools.partial(_s1inv_body, chain=chain),
        grid=(P, FFT_N2 // R_BLOCK),
        in_specs=in_specs,
        out_specs=out_specs,
        out_shape=out_shape,
        scratch_shapes=[pltpu.VMEM((2, CH_BLOCKS, N1h, R_BLOCK, LANES), F32)],
        compiler_params=_cparams("parallel", "parallel"),
        name="idft_n1_dft_n1" if chain else "idft_n1",
    )(*args)


def _outproj_body(a_ref, ga_ref, z_ref, gh_ref, x_ref, wa_ref, wh_ref, w_ref, fw_ref, o_ref, *, final):
    def norm_gate(y, w, g):
        ms = jnp.mean(y * y, axis=-1, keepdims=True)
        g = g.astype(F32)
        return (y * lax.rsqrt(ms + EPS) * w) * (g * (1.0 / (1.0 + jnp.exp(-g))))

    attn = a_ref[0].reshape(ATTN_W, a_ref.shape[-1]).T
    oa = norm_gate(attn, wa_ref[...], ga_ref[...]).astype(BF16)
    zz = jnp.concatenate([z_ref[0, c] for c in range(CH_BLOCKS)], axis=1)
    oh = norm_gate(zz, wh_ref[...], gh_ref[...]).astype(BF16)
    y = x_ref[...] + (_dot(oa, w_ref[0:ATTN_W, :]) + _dot(oh, w_ref[ATTN_W:MIX_W, :]))
    if final:
        ms = jnp.mean(y * y, axis=-1, keepdims=True)
        y = y * lax.rsqrt(ms + EPS) * fw_ref[...]
    o_ref[...] = y


def _outproj(attn_t, ga, zz4, gh, x2d, wa, wh, w_bf, fw, final, tm=512):
    T = x2d.shape[0]
    nlb = zz4.shape[2] // tm
    row = lambda i: (i, 0)
    const = lambda i: (0, 0)
    body = functools.partial(_outproj_body, final=final)
    return pl.pallas_call(
        body,
        grid=(T // tm,),
        in_specs=[
            pl.BlockSpec((1, N_Q_HEADS, HEAD_DIM, tm), lambda i: (i // nlb, 0, 0, i % nlb)),
            pl.BlockSpec((tm, ATTN_W), row),
            pl.BlockSpec((1, CH_BLOCKS, tm, LANES), lambda i: (i // nlb, 0, i % nlb, 0)),
            pl.BlockSpec((tm, HYENA_W), row),
            pl.BlockSpec((tm, D_MODEL), row),
            pl.BlockSpec((1, ATTN_W), const),
            pl.BlockSpec((1, HYENA_W), const),
            pl.BlockSpec((MIX_W, D_MODEL), const),
            pl.BlockSpec((1, D_MODEL), const),
        ],
        out_specs=pl.BlockSpec((tm, D_MODEL), row),
        out_shape=jax.ShapeDtypeStruct((T, D_MODEL), F32),
        compiler_params=_cparams("parallel"),
        name="outproj",
    )(attn_t, ga, zz4, gh, x2d, wa, wh, w_bf, fw)


def _rope_tables(L):
    t = jnp.arange(L, dtype=jnp.int32)
    pos = jnp.stack([t // GRID_W, t % GRID_W], axis=-1).astype(F32)
    freqs = ROPE_THETA ** (-jnp.arange(ROPE_FREQS, dtype=F32) / ROPE_FREQS)
    ang = pos[:, :, None] * freqs
    cos, sin = jnp.cos(ang), jnp.sin(ang)
    c_head = jnp.stack([cos, cos], axis=2).reshape(L, HEAD_DIM)
    s_head = jnp.stack([-sin, sin], axis=2).reshape(L, HEAD_DIM)
    reps = LANES // HEAD_DIM
    return jnp.tile(c_head, (1, reps)), jnp.tile(s_head, (1, reps))


def _head_mean_matrix():
    idx = jnp.arange(LANES) // HEAD_DIM
    return jnp.where(idx[:, None] == idx[None, :], 1.0 / HEAD_DIM, 0.0).astype(BF16)


def _dft_tables(L):
    N = 2 * L
    N1 = N // FFT_N2
    N1h = N1 // 2
    two_pi = 2.0 * math.pi
    k1 = jnp.arange(N1, dtype=jnp.int32)
    th = ((k1[:, None] * k1[None, :]) % N1).astype(F32) * (two_pi / N1)
    c, s = jnp.cos(th), jnp.sin(th)
    ch, sh = c[:, :N1h], s[:, :N1h]
    w1 = jnp.stack([jnp.concatenate([ch, sh], axis=1),
                    jnp.concatenate([-sh, ch], axis=1)], axis=1).reshape(2 * N1, N1)
    w1f = jnp.stack([c, -s], axis=1).reshape(2 * N1, N1)
    ct, st = ch.T, sh.T
    top = jnp.stack([ct, -st], axis=2).reshape(N1h, 2 * N1)
    bot = jnp.stack([st, ct], axis=2).reshape(N1h, 2 * N1)
    w1i = jnp.concatenate([top, bot], axis=0) * (1.0 / N)
    k2 = jnp.arange(FFT_N2, dtype=jnp.int32)
    freq = k1[:, None, None] + N1 * k2[None, :, None]
    ph = ((freq * k2[None, None, :]) % N).astype(F32) * (two_pi / N)
    cp, sp = jnp.cos(ph), jnp.sin(ph)
    g = jnp.concatenate([jnp.stack([cp, sp], axis=3).reshape(N1, FFT_N2, 2 * FFT_N2),
                         jnp.stack([-sp, cp], axis=3).reshape(N1, FFT_N2, 2 * FFT_N2)], axis=1)
    cpt, spt = jnp.swapaxes(cp, 1, 2), jnp.swapaxes(sp, 1, 2)
    gi = jnp.stack([jnp.concatenate([cpt, -spt], axis=2),
                    jnp.concatenate([spt, cpt], axis=2)], axis=2).reshape(N1, 2 * FFT_N2, 2 * FFT_N2)
    return w1.astype(BF16), w1f.astype(BF16), w1i.astype(BF16), g.astype(BF16), gi.astype(BF16)


def _filter_positions(L):
    m = jnp.arange(2 * L, dtype=jnp.int32)
    pos = jnp.where(m < L, m, 2 * L - m)
    valid = (m != L).astype(F32)[:, None]
    pos = jnp.where(m == L, 0, pos).astype(F32)
    t = pos / (L - 1)
    w = 2.0 * math.pi * pos / L
    bands = jnp.linspace(1e-4, FILT_BANDS - 1, FILT_BANDS, dtype=F32)
    ang = w[:, None] * bands[None, :]
    z = jnp.concatenate([t[:, None], jnp.cos(ang), -jnp.sin(ang)], axis=-1)
    z = jnp.pad(z, ((0, 0), (0, LANES - FILT_EMB)))
    return z, valid


def _pad_to(a, shape):
    return jnp.pad(a, [(0, s - d) for d, s in zip(a.shape, shape)])


def _trunk(x, norm_w, w_in, q_norm_w, k_norm_w, conv_w, conv_b, filt_w1, filt_b1,
           filt_w2, filt_b2, filt_w3, filt_freq, filt_decay, hyena_bias,
           attn_out_norm_w, hyena_out_norm_w, w_out, final_norm_w):
    B, L, _ = x.shape
    assert B % 2 == 0 and L % 1024 == 0
    P = B // 2
    N1 = 2 * L // FFT_N2
    N1h = N1 // 2
    T = B * L

    ctab, stab = _rope_tables(L)
    pmat = _head_mean_matrix()
    w1m, w1f, w1i, gmat, gimat = _dft_tables(L)
    ztab, valid = _filter_positions(L)
    reps = LANES // HEAD_DIM

    x2d = x.reshape(T, D_MODEL)
    for l in range(DEPTH):
        q, k, v, ga, uc, gh = _inproj(
            x2d, norm_w[l][None], w_in[l].astype(BF16),
            jnp.tile(q_norm_w[l], reps)[None], jnp.tile(k_norm_w[l], reps)[None],
            pmat, ctab, stab, conv_w[l], conv_b[l][None], B, L)
        attn = _attention(q, k, v, B, L)

        w3 = filt_w3[l].reshape(FILT_HID, HYENA_ORDER, 2, HYENA_W)
        w3d = _pad_to(jnp.transpose(w3, (2, 0, 1, 3)).reshape(2, FILT_HID, HYENA_ORDER * HYENA_W),
                      (2, LANES, HYENA_ORDER * HYENA_W))
        dec = filt_decay[l].reshape(HYENA_ORDER, 2, HYENA_W)
        decd = jnp.transpose(dec, (1, 0, 2)).reshape(2, 1, HYENA_ORDER * HYENA_W)
        buf, ssum = _filters(
            ztab, valid,
            _pad_to(filt_w1[l], (LANES, LANES)), _pad_to(filt_b1[l][None], (1, LANES)),
            _pad_to(filt_w2[l], (LANES, LANES)), _pad_to(filt_b2[l][None], (1, LANES)),
            _pad_to(filt_freq[l][None], (1, LANES)), w3d, decd, L)
        hpre = _s1_filter(buf.reshape(FILT_BLOCKS, N1, FFT_N2, LANES), ssum, w1f, N1)
        hspec = _mid_filter(hpre, gmat, N1)

        parts = uc.reshape(HYENA_ORDER + 1, 2, P, CH_BLOCKS, N1h, FFT_N2, LANES)
        zz = parts
        o1 = _s1(zz, 0, w1m, P, N1)
        for o in range(HYENA_ORDER):
            e = _mid(o1, gmat, gimat, hspec, o, P, N1)
            bias3 = hyena_bias[l, o].reshape(CH_BLOCKS, 1, LANES)
            if o + 1 < HYENA_ORDER:
                zz, o1 = _s1inv(e, w1i, zz, 0, parts, o + 1, bias3, P, N1, w1m=w1m)
            else:
                zz, = _s1inv(e, w1i, zz, 0, parts, o + 1, bias3, P, N1)
        zz4 = zz.reshape(B, CH_BLOCKS, L, LANES)

        x2d = _outproj(attn, ga, zz4, gh, x2d,
                       attn_out_norm_w[l][None], hyena_out_norm_w[l][None],
                       w_out[l].astype(BF16), final_norm_w[None], final=(l == DEPTH - 1))
    return x2d.reshape(B, L, D_MODEL)


def kernel(x_prompt, x_sample, norm_w, w_in, q_norm_w, k_norm_w, conv_w, conv_b, filt_w1, filt_b1, filt_w2, filt_b2, filt_w3, filt_freq, filt_decay, hyena_bias, attn_out_norm_w, hyena_out_norm_w, w_out, final_norm_w):
    weights = (norm_w, w_in, q_norm_w, k_norm_w, conv_w, conv_b, filt_w1, filt_b1, filt_w2, filt_b2,
               filt_w3, filt_freq, filt_decay, hyena_bias, attn_out_norm_w, hyena_out_norm_w, w_out,
               final_norm_w)
    return (_trunk(x_prompt, *weights), _trunk(x_sample, *weights))
```

```python
import functools
import math

import jax
import jax.numpy as jnp
from jax import lax
from jax.experimental import pallas as pl
from jax.experimental.pallas import tpu as pltpu

F32 = jnp.float32
BF16 = jnp.bfloat16
PACKED = jnp.uint32

D_MODEL = 1024
DEPTH = 4
GRID_W = 64
HEAD_DIM = 64
N_Q_HEADS = 8
N_KV_HEADS = 2
Q_PER_KV = N_Q_HEADS // N_KV_HEADS
ATTN_W = N_Q_HEADS * HEAD_DIM
KV_W = N_KV_HEADS * HEAD_DIM
HYENA_W = 512
HYENA_ORDER = 2
MIX_W = ATTN_W + HYENA_W
SHORT_CONV = 3
FILT_EMB = 33
FILT_BANDS = 16
FILT_HID = 64
N_FILT = HYENA_ORDER * 2 * HYENA_W
MOD_SHIFT = 0.05
ROPE_THETA = 10000.0
ROPE_FREQS = HEAD_DIM // 4
EPS = 1e-6
COL_K = ATTN_W
COL_V = COL_K + KV_W
COL_GA = COL_V + KV_W
COL_U = COL_GA + ATTN_W
COL_GH = COL_U + (HYENA_ORDER + 1) * HYENA_W
D_IN_PROJ = COL_GH + HYENA_W

ROW_TILE = 512
HALO_ROWS = 8
QK_SCALE = math.log2(math.e) / math.sqrt(HEAD_DIM)
KV_CHUNK = 512
Q_TILE = 512
Q_SUB = 256
V_ROWS = HEAD_DIM + 16
LANES = 128
FFT_N2 = 128
CH_BLOCKS = HYENA_W // LANES
FILT_BLOCKS = HYENA_ORDER * CH_BLOCKS
R_BLOCK = 8
K_BLOCK = 8
VMEM_LIMIT_BYTES = 48 * 1024 * 1024
HIGHEST = lax.Precision.HIGHEST


def _cparams(*sem):
    return pltpu.CompilerParams(dimension_semantics=sem, vmem_limit_bytes=VMEM_LIMIT_BYTES)


def _dot(a, b):
    return jnp.dot(a, b, preferred_element_type=F32)


def _dot_f32(a, b):
    return jnp.dot(a, b, preferred_element_type=F32, precision=HIGHEST)


def _inproj_body(x_ref, xprev_ref, xnext_ref, nw_ref, w_ref, qnw_ref, knw_ref, p_ref, c_ref, s_ref,
                 cw_ref, cb_ref, q_ref, k_ref, v_ref, ga_ref, parts_ref, gh_ref, *, nlb):
    def normed(x):
        ms = jnp.mean(x * x, axis=-1, keepdims=True)
        return (x * lax.rsqrt(ms + EPS) * nw_ref[...]).astype(BF16)

    h = normed(x_ref[...])
    p = p_ref[...]
    c = c_ref[...]
    s = s_ref[...]
    lane = lax.broadcasted_iota(jnp.int32, c.shape, 1)
    first_half = (lane % (2 * ROPE_FREQS)) < ROPE_FREQS

    def norm_rope(y, nw):
        y2 = y * y
        hi = y2.astype(BF16)
        lo = (y2 - hi.astype(F32)).astype(BF16)
        msq = _dot(hi, p) + _dot(lo, p)
        yn = y * lax.rsqrt(msq + EPS) * nw
        partner = jnp.where(first_half, pltpu.roll(yn, LANES - ROPE_FREQS, 1),
                            pltpu.roll(yn, ROPE_FREQS, 1))
        return yn * c + partner * s

    q = _dot(h, w_ref[:, 0:COL_K])
    qnw = qnw_ref[...]
    for j in range(ATTN_W // LANES):
        r = norm_rope(q[:, j * LANES:(j + 1) * LANES], qnw) * QK_SCALE
        rt = r.T
        q_ref[0, 2 * j] = rt[:HEAD_DIM].astype(BF16)
        q_ref[0, 2 * j + 1] = rt[HEAD_DIM:].astype(BF16)
    kv = _dot(h, w_ref[:, COL_K:COL_GA])
    kr = norm_rope(kv[:, :KV_W], knw_ref[...])
    k_ref[0, 0] = kr[:, :HEAD_DIM].astype(BF16)
    k_ref[0, 1] = pltpu.roll(kr, HEAD_DIM, 1)[:, :HEAD_DIM].astype(BF16)
    vt = kv[:, KV_W:].T
    extra = lax.broadcasted_iota(jnp.int32, (V_ROWS - HEAD_DIM, KV_CHUNK), 0)
    ones_row = jnp.where(extra == 0, 1.0, 0.0).astype(BF16)
    for c in range(vt.shape[1] // KV_CHUNK):
        for g in range(N_KV_HEADS):
            v_ref[0, g, c, 0:HEAD_DIM, :] = vt[g * HEAD_DIM:(g + 1) * HEAD_DIM,
                                               c * KV_CHUNK:(c + 1) * KV_CHUNK].astype(BF16)
            v_ref[0, g, c, HEAD_DIM:V_ROWS, :] = ones_row
    ga_ref[...] = _dot(h, w_ref[:, COL_GA:COL_U]).astype(ga_ref.dtype)
    gh_ref[...] = _dot(h, w_ref[:, COL_GH:D_IN_PROJ]).astype(gh_ref.dtype)

    u = _dot(h, w_ref[:, COL_U:COL_GH])
    halo = xprev_ref.shape[0]
    hh = normed(jnp.concatenate([xprev_ref[...], xnext_ref[...]], axis=0))
    uh = _dot(hh, w_ref[:, COL_U:COL_GH])
    li = pl.program_id(0) % nlb
    prev_row = jnp.where(li == 0, 0.0, uh[halo - 1:halo])
    next_row = jnp.where(li == nlb - 1, 0.0, uh[halo:halo + 1])
    tm = u.shape[0]
    row = lax.broadcasted_iota(jnp.int32, u.shape, 0)
    um = jnp.where(row == 0, prev_row, pltpu.roll(u, 1, 0))
    up = jnp.where(row == tm - 1, next_row, pltpu.roll(u, tm - 1, 0))
    cw = cw_ref[...]
    res = um * cw[0:1] + u * cw[1:2] + up * cw[2:3] + cb_ref[...]
    for part in range(HYENA_ORDER + 1):
        for cblk in range(CH_BLOCKS):
            lo = part * HYENA_W + cblk * LANES
            parts_ref[part, 0, cblk] = res[:, lo:lo + LANES]


def _inproj(x2d, nw, w_bf, qnw, knw, pmat, ctab, stab, cw, cb, B, L):
    tm, halo = ROW_TILE, HALO_ROWS
    assert tm % KV_CHUNK == 0 and L % tm == 0
    T = B * L
    nlb = L // tm
    rpb = tm // halo
    UW = (HYENA_ORDER + 1) * HYENA_W
    row = lambda i: (i, 0)
    const = lambda i: (0, 0)
    pos = lambda i: (i % nlb, 0)
    heads = lambda i: (i // nlb, 0, i % nlb, 0)
    return pl.pallas_call(
        functools.partial(_inproj_body, nlb=nlb),
        grid=(T // tm,),
        in_specs=[
            pl.BlockSpec((tm, D_MODEL), row),
            pl.BlockSpec((halo, D_MODEL), lambda i: (jnp.maximum(i * rpb - 1, 0), 0)),
            pl.BlockSpec((halo, D_MODEL), lambda i: (jnp.minimum((i + 1) * rpb, T // halo - 1), 0)),
            pl.BlockSpec((1, D_MODEL), const),
            pl.BlockSpec((D_MODEL, D_IN_PROJ), const),
            pl.BlockSpec((1, LANES), const),
            pl.BlockSpec((1, LANES), const),
            pl.BlockSpec((LANES, LANES), const),
            pl.BlockSpec((tm, LANES), pos),
            pl.BlockSpec((tm, LANES), pos),
            pl.BlockSpec((SHORT_CONV, UW), const),
            pl.BlockSpec((1, UW), const),
        ],
        out_specs=[
            pl.BlockSpec((1, N_Q_HEADS, HEAD_DIM, tm), lambda i: (i // nlb, 0, 0, i % nlb)),
            pl.BlockSpec((1, N_KV_HEADS, tm, HEAD_DIM), heads),
            pl.BlockSpec((1, N_KV_HEADS, tm // KV_CHUNK, V_ROWS, KV_CHUNK),
                         lambda i: (i // nlb, 0, i % nlb, 0, 0)),
            pl.BlockSpec((tm, ATTN_W), row),
            pl.BlockSpec((HYENA_ORDER + 1, 1, CH_BLOCKS, tm, LANES), lambda i: (0, i // nlb, 0, i % nlb, 0)),
            pl.BlockSpec((tm, HYENA_W), row),
        ],
        out_shape=[
            jax.ShapeDtypeStruct((B, N_Q_HEADS, HEAD_DIM, L), BF16),
            jax.ShapeDtypeStruct((B, N_KV_HEADS, L, HEAD_DIM), BF16),
            jax.ShapeDtypeStruct((B, N_KV_HEADS, L // KV_CHUNK, V_ROWS, KV_CHUNK), BF16),
            jax.ShapeDtypeStruct((T, ATTN_W), BF16),
            jax.ShapeDtypeStruct((HYENA_ORDER + 1, B, CH_BLOCKS, L, LANES), F32),
            jax.ShapeDtypeStruct((T, HYENA_W), BF16),
        ],
        compiler_params=_cparams("parallel"),
        name="inproj",
    )(x2d, x2d, x2d, nw, w_bf, qnw, knw, pmat, ctab, stab, cw, cb)


def _attn_body(qt_ref, k_ref, vt_ref, o_ref, s_ref, p_ref, al_ref, cm_ref, m_ref, acc_ref, *, tq, nk):
    cols = Q_PER_KV * tq
    ncb = cols // LANES
    m_ref[...] = jnp.full(m_ref.shape, -jnp.inf, F32)
    acc_ref[...] = jnp.zeros(acc_ref.shape, F32)

    def scores(j, slot):
        start = pl.multiple_of(j * KV_CHUNK, KV_CHUNK)
        k = k_ref[0, 0, pl.ds(start, KV_CHUNK), :]
        for h in range(Q_PER_KV):
            for q0 in range(0, tq, Q_SUB):
                c0 = h * tq + q0
                r = _dot(k, qt_ref[0, h, :, q0:q0 + Q_SUB])
                for i in range(Q_SUB // LANES):
                    s_ref[slot, c0 // LANES + i] = r[:, i * LANES:(i + 1) * LANES]
                cm_ref[slot, :, c0:c0 + Q_SUB] = jnp.max(r, axis=0, keepdims=True)

    def accumulate(j, slot):
        vt = vt_ref[0, 0, j]
        for i in range(cols // Q_SUB):
            cs = slice(i * Q_SUB, (i + 1) * Q_SUB)
            nb = Q_SUB // LANES
            p = jnp.concatenate([p_ref[slot, nb * i + t] for t in range(nb)], axis=1)
            acc_ref[:, cs] = al_ref[slot, :, cs] * acc_ref[:, cs] + _dot(vt, p)

    def softmax(slot):
        m_old = m_ref[...]
        m_new = jnp.maximum(m_old, cm_ref[slot])
        m_ref[...] = m_new
        al_ref[slot] = jnp.exp2(m_old - m_new)
        for c in range(ncb):
            p_ref[slot, c] = jnp.exp2(s_ref[slot, c] - m_new[:, c * LANES:(c + 1) * LANES]).astype(BF16)

    scores(0, 0)
    scores(1, 1)
    softmax(0)
    scores(2, 0)
    accumulate(0, 0)
    softmax(1)

    def step(jj, carry):
        j = 2 * jj
        scores(j + 1, 1)
        accumulate(j - 1, 1)
        softmax(0)
        scores(jnp.minimum(j + 2, nk - 1), 0)
        accumulate(j, 0)
        softmax(1)
        return carry

    lax.fori_loop(1, nk // 2, step, 0)
    accumulate(nk - 1, 1)
    o = acc_ref[0:HEAD_DIM, :] / acc_ref[HEAD_DIM:HEAD_DIM + 1, :]
    for h in range(Q_PER_KV):
        o_ref[0, h] = o[:, h * tq:(h + 1) * tq]


def _attention(qt, k, vt, B, L):
    tq = Q_TILE
    nk = L // KV_CHUNK
    assert nk % 2 == 0 and nk >= 4
    cols = Q_PER_KV * tq
    ncb = cols // LANES
    body = functools.partial(_attn_body, tq=tq, nk=nk)
    return pl.pallas_call(
        body,
        grid=(B, N_KV_HEADS, L // tq),
        in_specs=[
            pl.BlockSpec((1, Q_PER_KV, HEAD_DIM, tq), lambda b, g, i: (b, g, 0, i)),
            pl.BlockSpec((1, 1, L, HEAD_DIM), lambda b, g, i: (b, g, 0, 0)),
            pl.BlockSpec((1, 1, nk, V_ROWS, KV_CHUNK), lambda b, g, i: (b, g, 0, 0, 0)),
        ],
        out_specs=pl.BlockSpec((1, Q_PER_KV, HEAD_DIM, tq), lambda b, g, i: (b, g, 0, i)),
        out_shape=jax.ShapeDtypeStruct((B, N_Q_HEADS, HEAD_DIM, L), F32),
        scratch_shapes=[
            pltpu.VMEM((2, ncb, KV_CHUNK, LANES), F32),
            pltpu.VMEM((2, ncb, KV_CHUNK, LANES), BF16),
            pltpu.VMEM((2, 1, cols), F32),
            pltpu.VMEM((2, 1, cols), F32),
            pltpu.VMEM((1, cols), F32),
            pltpu.VMEM((V_ROWS, cols), F32),
        ],
        compiler_params=_cparams("parallel", "parallel", "parallel"),
        name="attention",
    )(qt, k, vt)


def _filt_body(z_ref, valid_ref, w1_ref, b1_ref, w2_ref, b2_ref, fr_ref, w3_ref, dec_ref,
               buf_ref, sum_ref):
    i = pl.program_id(0)
    z = z_ref[...]
    fr = fr_ref[...]
    h = jnp.sin(fr * (_dot_f32(z, w1_ref[...]) + b1_ref[...]))
    h = jnp.sin(fr * (_dot_f32(h, w2_ref[...]) + b2_ref[...]))
    h = jnp.concatenate([_dot_f32(h, w3_ref[0, 0]), _dot_f32(h, w3_ref[0, 1])], axis=0)
    t = jnp.concatenate([z[:, 0:1], z[:, FILT_HID:FILT_HID + 1]], axis=0)
    win = jnp.exp(-t * jnp.abs(dec_ref[0])) + MOD_SHIFT
    out = h * win * valid_ref[...]
    for c in range(FILT_BLOCKS):
        buf_ref[c] = out[:, c * LANES:(c + 1) * LANES]

    @pl.when(i == 0)
    def _():
        sum_ref[...] = jnp.zeros_like(sum_ref)

    sum_ref[...] += jnp.sum(jnp.abs(out), axis=0, keepdims=True)


def _filters(ztab, valid, w1, b1, w2, b2, fr, w3d, decd, L):
    rb = ROW_TILE
    n = 2 * L
    nb = n // rb
    half = nb // 2
    W = HYENA_ORDER * HYENA_W
    const = lambda i: (0, 0)
    return pl.pallas_call(
        _filt_body,
        grid=(nb,),
        in_specs=[
            pl.BlockSpec((rb // 2, LANES), lambda i: (i, 0)),
            pl.BlockSpec((rb, 1), lambda i: (i, 0)),
            pl.BlockSpec((LANES, LANES), const),
            pl.BlockSpec((1, LANES), const),
            pl.BlockSpec((LANES, LANES), const),
            pl.BlockSpec((1, LANES), const),
            pl.BlockSpec((1, LANES), const),
            pl.BlockSpec((1, 2, LANES, W), lambda i: (i // half, 0, 0, 0)),
            pl.BlockSpec((1, 1, W), lambda i: (i // half, 0, 0)),
        ],
        out_specs=[
            pl.BlockSpec((FILT_BLOCKS, rb, LANES), lambda i: (0, i, 0)),
            pl.BlockSpec((1, W), const),
        ],
        out_shape=[
            jax.ShapeDtypeStruct((FILT_BLOCKS, n, LANES), F32),
            jax.ShapeDtypeStruct((1, W), F32),
        ],
        compiler_params=_cparams("arbitrary"),
        name="filters",
    )(ztab, valid, w1, b1, w2, b2, fr, w3d, decd)


def _strided_rows(ref, lead):
    n = len(lead)
    nblk, count, stride = ref.shape[n:n + 3]
    return ref.reshape(ref.shape[:n + 1] + (count * stride, LANES)), nblk, count, stride


def _gather_rows(ref, lead, r):
    flat, nblk, count, stride = _strided_rows(ref, lead)
    return jnp.concatenate([flat[lead + (c, pl.ds(r, count, stride=stride), slice(None))]
                            for c in range(nblk)], axis=1)


def _scatter_rows(ref, lead, r, val):
    flat, nblk, count, stride = _strided_rows(ref, lead)
    for c in range(nblk):
        flat[lead + (c, pl.ds(r, count, stride=stride), slice(None))] = val[:, c * LANES:(c + 1) * LANES]


def _pack_pairs(y):
    return pltpu.bitcast(y.astype(BF16), PACKED)


def _unpack_pairs(w):
    return pltpu.bitcast(w, BF16)


def _tile_row(ref, lead, k):
    return jnp.concatenate([ref[lead + (c, k)] for c in range(ref.shape[len(lead)])], axis=1)


def _s1_body(a_ref, w_ref, o_ref):
    rb = a_ref.shape[3]
    for r in range(rb):
        z = jnp.concatenate([_gather_rows(a_ref, (m,), r) for m in range(2)], axis=0)
        y = _pack_pairs(_dot(w_ref[...], z.astype(BF16)))
        _scatter_rows(o_ref, (0,), r, y)


def _s1(src, which, w1m, P, N1):
    N1h = N1 // 2
    return pl.pallas_call(
        _s1_body,
        grid=(P, FFT_N2 // R_BLOCK),
        in_specs=[
            pl.BlockSpec((None, 2, None, CH_BLOCKS, N1h, R_BLOCK, LANES), lambda p, j: (which, 0, p, 0, 0, j, 0)),
            pl.BlockSpec((2 * N1, N1), lambda p, j: (0, 0)),
        ],
        out_specs=pl.BlockSpec((1, CH_BLOCKS, N1, R_BLOCK, LANES), lambda p, j: (p, 0, 0, j, 0)),
        out_shape=jax.ShapeDtypeStruct((P, CH_BLOCKS, N1, FFT_N2, LANES), PACKED),
        compiler_params=_cparams("parallel", "parallel"),
        name="dft_n1",
    )(src, w1m)


def _s1f_body(b_ref, s_ref, w_ref, o_ref):
    rb = b_ref.shape[2]
    for r in range(rb):
        z = (_gather_rows(b_ref, (), r) / s_ref[...]).astype(BF16)
        _scatter_rows(o_ref, (), r, _pack_pairs(_dot(w_ref[...], z)))


def _s1_filter(buf4, ssum, w1f, N1):
    W = FILT_BLOCKS * LANES
    return pl.pallas_call(
        _s1f_body,
        grid=(FFT_N2 // R_BLOCK,),
        in_specs=[
            pl.BlockSpec((FILT_BLOCKS, N1, R_BLOCK, LANES), lambda j: (0, 0, j, 0)),
            pl.BlockSpec((1, W), lambda j: (0, 0)),
            pl.BlockSpec((2 * N1, N1), lambda j: (0, 0)),
        ],
        out_specs=pl.BlockSpec((FILT_BLOCKS, N1, R_BLOCK, LANES), lambda j: (0, 0, j, 0)),
        out_shape=jax.ShapeDtypeStruct((FILT_BLOCKS, N1, FFT_N2, LANES), PACKED),
        compiler_params=_cparams("parallel"),
        name="dft_n1_filter",
    )(buf4, ssum, w1f)


def _midf_body(x_ref, g_ref, h_ref):
    for k in range(x_ref.shape[1]):
        z = _unpack_pairs(_tile_row(x_ref, (), k))
        h_ref[k] = _dot(g_ref[k], z)


def _mid_filter(hpre, g, N1):
    W = FILT_BLOCKS * LANES
    kb = K_BLOCK
    return pl.pallas_call(
        _midf_body,
        grid=(N1 // kb,),
        in_specs=[
            pl.BlockSpec((FILT_BLOCKS, kb, FFT_N2, LANES), lambda i: (0, i, 0, 0)),
            pl.BlockSpec((kb, 2 * FFT_N2, 2 * FFT_N2), lambda i: (i, 0, 0)),
        ],
        out_specs=pl.BlockSpec((kb, 2 * FFT_N2, W), lambda i: (i, 0, 0)),
        out_shape=jax.ShapeDtypeStruct((N1, 2 * FFT_N2, W), F32),
        compiler_params=_cparams("parallel"),
        name="dft_n2_filter",
    )(hpre, g)


def _mid_body(x_ref, g_ref, gi_ref, h_ref, o_ref, d_ref, y_ref):
    kb = x_ref.shape[2]

    def forward(k):
        z = _unpack_pairs(_tile_row(x_ref, (0,), k))
        d_ref[k % 2] = _dot(g_ref[k], z)

    def product(k):
        hh = h_ref[k]
        dr, di = d_ref[k % 2, :FFT_N2], d_ref[k % 2, FFT_N2:]
        hr, hi = hh[:FFT_N2], hh[FFT_N2:]
        y_ref[k % 2, :FFT_N2] = (dr * hr - di * hi).astype(BF16)
        y_ref[k % 2, FFT_N2:] = (dr * hi + di * hr).astype(BF16)

    def inverse(k):
        e = _dot(gi_ref[k], y_ref[k % 2])
        ew = _pack_pairs(e)
        for c in range(o_ref.shape[1]):
            o_ref[0, c, k] = ew[:, c * LANES:(c + 1) * LANES]

    forward(0)
    for k in range(kb):
        if k + 1 < kb:
            forward(k + 1)
        if k > 0:
            inverse(k - 1)
        product(k)
    inverse(kb - 1)


def _mid(x5, g, gi, hspec, order, P, N1):
    kb = K_BLOCK
    blk = pl.BlockSpec((1, CH_BLOCKS, kb, FFT_N2, LANES), lambda i, p: (p, 0, i, 0, 0))
    return pl.pallas_call(
        _mid_body,
        grid=(N1 // kb, P),
        in_specs=[
            blk,
            pl.BlockSpec((kb, 2 * FFT_N2, 2 * FFT_N2), lambda i, p: (i, 0, 0)),
            pl.BlockSpec((kb, 2 * FFT_N2, 2 * FFT_N2), lambda i, p: (i, 0, 0)),
            pl.BlockSpec((kb, 2 * FFT_N2, HYENA_W), lambda i, p: (i, 0, order)),
        ],
        out_specs=blk,
        out_shape=jax.ShapeDtypeStruct((P, CH_BLOCKS, N1, FFT_N2, LANES), PACKED),
        scratch_shapes=[pltpu.VMEM((2, 2 * FFT_N2, HYENA_W), F32),
                        pltpu.VMEM((2, 2 * FFT_N2, HYENA_W), BF16)],
        compiler_params=_cparams("parallel", "parallel"),
        name="dft_n2_conv",
    )(x5, g, gi, hspec)


def _s1inv_body(e_ref, w_ref, src_ref, mul_ref, bias_ref, *rest, chain):
    if chain:
        w1_ref, o_ref, o1_ref, y_ref = rest
    else:
        o_ref, y_ref = rest
    n1h, rb = src_ref.shape[2:4]
    for r in range(rb):
        e = _unpack_pairs(_gather_rows(e_ref, (0,), r))
        y = _dot(w_ref[...], e)
        for ro in range(2):
            _scatter_rows(y_ref, (ro,), r, y[ro * n1h:(ro + 1) * n1h])
    zz = mul_ref[...] * (y_ref[...] + src_ref[...] * bias_ref[...][None, :, None])
    o_ref[...] = zz
    if chain:
        y_ref[...] = zz
        _s1_body(y_ref, w1_ref, o1_ref)


def _s1inv(e5, w1i, src, src_which, mul, mul_which, bias3, P, N1, w1m=None):
    N1h = N1 // 2
    chain = w1m is not None
    member = lambda which: pl.BlockSpec((None, 2, None, CH_BLOCKS, N1h, R_BLOCK, LANES),
                                        lambda p, j: (which, 0, p, 0, 0, j, 0))
    packed_blk = pl.BlockSpec((1, CH_BLOCKS, N1, R_BLOCK, LANES), lambda p, j: (p, 0, 0, j, 0))
    in_specs = [
        packed_blk,
        pl.BlockSpec((N1, 2 * N1), lambda p, j: (0, 0)),
        member(src_which),
        member(mul_which),
        pl.BlockSpec((CH_BLOCKS, 1, LANES), lambda p, j: (0, 0, 0)),
    ]
    out_specs = [member(0)]
    out_shape = [jax.ShapeDtypeStruct((1, 2, P, CH_BLOCKS, N1h, FFT_N2, LANES), F32)]
    args = [e5, w1i, src, mul, bias3]
    if chain:
        in_specs.append(pl.BlockSpec((2 * N1, N1), lambda p, j: (0, 0)))
        out_specs.append(packed_blk)
        out_shape.append(jax.ShapeDtypeStruct((P, CH_BLOCKS, N1, FFT_N2, LANES), PACKED))
        args.append(w1m)
    return pl.pallas_call(
        functools.partial(_s1inv_body, chain=chain),
        grid=(P, FFT_N2 // R_BLOCK),
        in_specs=in_specs,
        out_specs=out_specs,
        out_shape=out_shape,
        scratch_shapes=[pltpu.VMEM((2, CH_BLOCKS, N1h, R_BLOCK, LANES), F32)],
        compiler_params=_cparams("parallel", "parallel"),
        name="idft_n1_dft_n1" if chain else "idft_n1",
    )(*args)


def _outproj_body(a_ref, ga_ref, z_ref, gh_ref, x_ref, wa_ref, wh_ref, w_ref, fw_ref, o_ref, *, final):
    def norm_gate(y, w, g):
        ms = jnp.mean(y * y, axis=-1, keepdims=True)
        g = g.astype(F32)
        return (y * lax.rsqrt(ms + EPS) * w) * (g * (1.0 / (1.0 + jnp.exp(-g))))

    attn = a_ref[0].reshape(ATTN_W, a_ref.shape[-1]).T
    oa = norm_gate(attn, wa_ref[...], ga_ref[...]).astype(BF16)
    zz = jnp.concatenate([z_ref[0, c] for c in range(CH_BLOCKS)], axis=1)
    oh = norm_gate(zz, wh_ref[...], gh_ref[...]).astype(BF16)
    y = x_ref[...] + (_dot(oa, w_ref[0:ATTN_W, :]) + _dot(oh, w_ref[ATTN_W:MIX_W, :]))
    if final:
        ms = jnp.mean(y * y, axis=-1, keepdims=True)
        y = y * lax.rsqrt(ms + EPS) * fw_ref[...]
    o_ref[...] = y


def _outproj(attn_t, ga, zz4, gh, x2d, wa, wh, w_bf, fw, final):
    tm = ROW_TILE
    T = x2d.shape[0]
    nlb = zz4.shape[2] // tm
    row = lambda i: (i, 0)
    const = lambda i: (0, 0)
    body = functools.partial(_outproj_body, final=final)
    return pl.pallas_call(
        body,
        grid=(T // tm,),
        in_specs=[
            pl.BlockSpec((1, N_Q_HEADS, HEAD_DIM, tm), lambda i: (i // nlb, 0, 0, i % nlb)),
            pl.BlockSpec((tm, ATTN_W), row),
            pl.BlockSpec((1, CH_BLOCKS, tm, LANES), lambda i: (i // nlb, 0, i % nlb, 0)),
            pl.BlockSpec((tm, HYENA_W), row),
            pl.BlockSpec((tm, D_MODEL), row),
            pl.BlockSpec((1, ATTN_W), const),
            pl.BlockSpec((1, HYENA_W), const),
            pl.BlockSpec((MIX_W, D_MODEL), const),
            pl.BlockSpec((1, D_MODEL), const),
        ],
        out_specs=pl.BlockSpec((tm, D_MODEL), row),
        out_shape=jax.ShapeDtypeStruct((T, D_MODEL), F32),
        compiler_params=_cparams("parallel"),
        name="outproj",
    )(attn_t, ga, zz4, gh, x2d, wa, wh, w_bf, fw)


def _rope_tables(L):
    t = jnp.arange(L, dtype=jnp.int32)
    pos = jnp.stack([t // GRID_W, t % GRID_W], axis=-1).astype(F32)
    freqs = ROPE_THETA ** (-jnp.arange(ROPE_FREQS, dtype=F32) / ROPE_FREQS)
    ang = pos[:, :, None] * freqs
    cos, sin = jnp.cos(ang), jnp.sin(ang)
    c_head = jnp.stack([cos, cos], axis=2).reshape(L, HEAD_DIM)
    s_head = jnp.stack([-sin, sin], axis=2).reshape(L, HEAD_DIM)
    reps = LANES // HEAD_DIM
    return jnp.tile(c_head, (1, reps)), jnp.tile(s_head, (1, reps))


def _head_mean_matrix():
    idx = jnp.arange(LANES) // HEAD_DIM
    return jnp.where(idx[:, None] == idx[None, :], 1.0 / HEAD_DIM, 0.0).astype(BF16)


def _dft_tables(L):
    N = 2 * L
    N1 = N // FFT_N2
    N1h = N1 // 2
    two_pi = 2.0 * math.pi
    k1 = jnp.arange(N1, dtype=jnp.int32)
    th = ((k1[:, None] * k1[None, :]) % N1).astype(F32) * (two_pi / N1)
    c, s = jnp.cos(th), jnp.sin(th)
    ch, sh = c[:, :N1h], s[:, :N1h]
    w1 = jnp.stack([jnp.concatenate([ch, sh], axis=1),
                    jnp.concatenate([-sh, ch], axis=1)], axis=1).reshape(2 * N1, N1)
    w1f = jnp.stack([c, -s], axis=1).reshape(2 * N1, N1)
    ct, st = ch.T, sh.T
    top = jnp.stack([ct, -st], axis=2).reshape(N1h, 2 * N1)
    bot = jnp.stack([st, ct], axis=2).reshape(N1h, 2 * N1)
    w1i = jnp.concatenate([top, bot], axis=0) * (1.0 / N)
    k2 = jnp.arange(FFT_N2, dtype=jnp.int32)
    freq = k1[:, None, None] + N1 * k2[None, :, None]
    ph = ((freq * k2[None, None, :]) % N).astype(F32) * (two_pi / N)
    cp, sp = jnp.cos(ph), jnp.sin(ph)
    g = jnp.concatenate([jnp.stack([cp, sp], axis=3).reshape(N1, FFT_N2, 2 * FFT_N2),
                         jnp.stack([-sp, cp], axis=3).reshape(N1, FFT_N2, 2 * FFT_N2)], axis=1)
    cpt, spt = jnp.swapaxes(cp, 1, 2), jnp.swapaxes(sp, 1, 2)
    gi = jnp.stack([jnp.concatenate([cpt, -spt], axis=2),
                    jnp.concatenate([spt, cpt], axis=2)], axis=2).reshape(N1, 2 * FFT_N2, 2 * FFT_N2)
    return w1.astype(BF16), w1f.astype(BF16), w1i.astype(BF16), g.astype(BF16), gi.astype(BF16)


def _filter_positions(L):
    m = jnp.arange(2 * L, dtype=jnp.int32)
    pos = jnp.where(m < L, m, 2 * L - m)
    valid = (m != L).astype(F32)[:, None]
    pos = jnp.where(m == L, 0, pos).astype(F32)
    t = pos / (L - 1)
    w = 2.0 * math.pi * pos / L
    bands = jnp.linspace(1e-4, FILT_BANDS - 1, FILT_BANDS, dtype=F32)
    ang = w[:, None] * bands[None, :]
    z = jnp.concatenate([t[:, None], jnp.cos(ang), -jnp.sin(ang)], axis=-1)
    z = jnp.pad(z, ((0, 0), (0, FILT_HID - FILT_EMB)))
    hb = ROW_TILE // 2
    z = z.reshape(2 * L // ROW_TILE, 2, hb, FILT_HID)
    z = jnp.transpose(z, (0, 2, 1, 3)).reshape(L, 2 * FILT_HID)
    return z, valid


def _two_row_weights(w):
    zero = jnp.zeros_like(w)
    return jnp.concatenate([jnp.concatenate([w, zero], axis=1), jnp.concatenate([zero, w], axis=1)], axis=0)


def _pad_to(a, shape):
    return jnp.pad(a, [(0, s - d) for d, s in zip(a.shape, shape)])


def _trunk(x, norm_w, w_in, q_norm_w, k_norm_w, conv_w, conv_b, filt_w1, filt_b1,
           filt_w2, filt_b2, filt_w3, filt_freq, filt_decay, hyena_bias,
           attn_out_norm_w, hyena_out_norm_w, w_out, final_norm_w):
    B, L, _ = x.shape
    assert B % 2 == 0 and L % 1024 == 0
    P = B // 2
    N1 = 2 * L // FFT_N2
    N1h = N1 // 2
    T = B * L

    ctab, stab = _rope_tables(L)
    pmat = _head_mean_matrix()
    w1m, w1f, w1i, gmat, gimat = _dft_tables(L)
    ztab, valid = _filter_positions(L)
    reps = LANES // HEAD_DIM

    x2d = x.reshape(T, D_MODEL)
    for l in range(DEPTH):
        q, k, v, ga, uc, gh = _inproj(
            x2d, norm_w[l][None], w_in[l].astype(BF16),
            jnp.tile(q_norm_w[l], reps)[None], jnp.tile(k_norm_w[l], reps)[None],
            pmat, ctab, stab, conv_w[l], conv_b[l][None], B, L)
        attn = _attention(q, k, v, B, L)

        W = HYENA_ORDER * HYENA_W
        w3 = filt_w3[l].reshape(FILT_HID, HYENA_ORDER, 2, HYENA_W)
        w3 = jnp.transpose(w3, (2, 0, 1, 3)).reshape(2, FILT_HID, W)
        zero = jnp.zeros_like(w3)
        w3d = jnp.stack([jnp.concatenate([w3, zero], axis=1), jnp.concatenate([zero, w3], axis=1)], axis=1)
        dec = filt_decay[l].reshape(HYENA_ORDER, 2, HYENA_W)
        decd = jnp.transpose(dec, (1, 0, 2)).reshape(2, 1, W)
        hid = (FILT_HID, FILT_HID)
        buf, ssum = _filters(
            ztab, valid,
            _two_row_weights(_pad_to(filt_w1[l], hid)), jnp.tile(filt_b1[l], 2)[None],
            _two_row_weights(filt_w2[l]), jnp.tile(filt_b2[l], 2)[None],
            jnp.tile(filt_freq[l], 2)[None], w3d, decd, L)
        hpre = _s1_filter(buf.reshape(FILT_BLOCKS, N1, FFT_N2, LANES), ssum, w1f, N1)
        hspec = _mid_filter(hpre, gmat, N1)

        parts = uc.reshape(HYENA_ORDER + 1, 2, P, CH_BLOCKS, N1h, FFT_N2, LANES)
        zz = parts
        o1 = _s1(zz, 0, w1m, P, N1)
        for o in range(HYENA_ORDER):
            e = _mid(o1, gmat, gimat, hspec, o, P, N1)
            bias3 = hyena_bias[l, o].reshape(CH_BLOCKS, 1, LANES)
            if o + 1 < HYENA_ORDER:
                zz, o1 = _s1inv(e, w1i, zz, 0, parts, o + 1, bias3, P, N1, w1m=w1m)
            else:
                zz, = _s1inv(e, w1i, zz, 0, parts, o + 1, bias3, P, N1)
        zz4 = zz.reshape(B, CH_BLOCKS, L, LANES)

        x2d = _outproj(attn, ga, zz4, gh, x2d,
                       attn_out_norm_w[l][None], hyena_out_norm_w[l][None],
                       w_out[l].astype(BF16), final_norm_w[None], final=(l == DEPTH - 1))
    return x2d.reshape(B, L, D_MODEL)


def kernel(x_prompt, x_sample, norm_w, w_in, q_norm_w, k_norm_w, conv_w, conv_b, filt_w1, filt_b1, filt_w2, filt_b2, filt_w3, filt_freq, filt_decay, hyena_bias, attn_out_norm_w, hyena_out_norm_w, w_out, final_norm_w):
    weights = (norm_w, w_in, q_norm_w, k_norm_w, conv_w, conv_b, filt_w1, filt_b1, filt_w2, filt_b2,
               filt_w3, filt_freq, filt_decay, hyena_bias, attn_out_norm_w, hyena_out_norm_w, w_out,
               final_norm_w)
    return (_trunk(x_prompt, *weights), _trunk(x_sample, *weights))
```

```python
import functools
import math

import jax
import jax.numpy as jnp
from jax import lax
from jax.experimental import pallas as pl
from jax.experimental.pallas import tpu as pltpu

F32 = jnp.float32
BF16 = jnp.bfloat16
PACKED = jnp.uint32

D_MODEL = 1024
DEPTH = 4
GRID_W = 64
HEAD_DIM = 64
N_Q_HEADS = 8
N_KV_HEADS = 2
Q_PER_KV = N_Q_HEADS // N_KV_HEADS
ATTN_W = N_Q_HEADS * HEAD_DIM
KV_W = N_KV_HEADS * HEAD_DIM
HYENA_W = 512
HYENA_ORDER = 2
MIX_W = ATTN_W + HYENA_W
SHORT_CONV = 3
FILT_EMB = 33
FILT_BANDS = 16
FILT_HID = 64
N_FILT = HYENA_ORDER * 2 * HYENA_W
MOD_SHIFT = 0.05
ROPE_THETA = 10000.0
ROPE_FREQS = HEAD_DIM // 4
EPS = 1e-6
COL_K = ATTN_W
COL_V = COL_K + KV_W
COL_GA = COL_V + KV_W
COL_U = COL_GA + ATTN_W
COL_GH = COL_U + (HYENA_ORDER + 1) * HYENA_W
D_IN_PROJ = COL_GH + HYENA_W

ROW_TILE = 512
HALO_ROWS = 8
QK_SCALE = math.log2(math.e) / math.sqrt(HEAD_DIM)
KV_CHUNK = 512
Q_TILE = 512
Q_SUB = 256
V_ROWS = HEAD_DIM + 16
LANES = 128
FFT_N2 = 128
CH_BLOCKS = HYENA_W // LANES
FILT_BLOCKS = HYENA_ORDER * CH_BLOCKS
R_BLOCK = 8
K_BLOCK = 8
VMEM_LIMIT_BYTES = 48 * 1024 * 1024
HIGHEST = lax.Precision.HIGHEST


def _cparams(*sem):
    return pltpu.CompilerParams(dimension_semantics=sem, vmem_limit_bytes=VMEM_LIMIT_BYTES)


def _dot(a, b):
    return jnp.dot(a, b, preferred_element_type=F32)


def _dot_f32(a, b):
    return jnp.dot(a, b, preferred_element_type=F32, precision=HIGHEST)


def _inproj_body(x_ref, xprev_ref, xnext_ref, nw_ref, w_ref, qnw_ref, knw_ref, p_ref, c_ref, s_ref,
                 cw_ref, cb_ref, q_ref, k_ref, v_ref, ga_ref, parts_ref, gh_ref, *, nlb):
    def normed(x):
        ms = jnp.mean(x * x, axis=-1, keepdims=True)
        return (x * lax.rsqrt(ms + EPS) * nw_ref[...]).astype(BF16)

    h = normed(x_ref[...])
    p = p_ref[...]
    c = c_ref[...]
    s = s_ref[...]
    lane = lax.broadcasted_iota(jnp.int32, c.shape, 1)
    first_half = (lane % (2 * ROPE_FREQS)) < ROPE_FREQS

    def norm_rope(y, nw):
        y2 = y * y
        hi = y2.astype(BF16)
        lo = (y2 - hi.astype(F32)).astype(BF16)
        msq = _dot(hi, p) + _dot(lo, p)
        yn = y * lax.rsqrt(msq + EPS) * nw
        partner = jnp.where(first_half, pltpu.roll(yn, LANES - ROPE_FREQS, 1),
                            pltpu.roll(yn, ROPE_FREQS, 1))
        return yn * c + partner * s

    q = _dot(h, w_ref[:, 0:COL_K])
    qnw = qnw_ref[...]
    for j in range(ATTN_W // LANES):
        r = norm_rope(q[:, j * LANES:(j + 1) * LANES], qnw) * QK_SCALE
        rt = r.T
        q_ref[0, 2 * j] = rt[:HEAD_DIM].astype(BF16)
        q_ref[0, 2 * j + 1] = rt[HEAD_DIM:].astype(BF16)
    kv = _dot(h, w_ref[:, COL_K:COL_GA])
    kr = norm_rope(kv[:, :KV_W], knw_ref[...])
    k_ref[0, 0] = kr[:, :HEAD_DIM].astype(BF16)
    k_ref[0, 1] = pltpu.roll(kr, HEAD_DIM, 1)[:, :HEAD_DIM].astype(BF16)
    vt = kv[:, KV_W:].T
    extra = lax.broadcasted_iota(jnp.int32, (V_ROWS - HEAD_DIM, KV_CHUNK), 0)
    ones_row = jnp.where(extra == 0, 1.0, 0.0).astype(BF16)
    for c in range(vt.shape[1] // KV_CHUNK):
        for g in range(N_KV_HEADS):
            v_ref[0, g, c, 0:HEAD_DIM, :] = vt[g * HEAD_DIM:(g + 1) * HEAD_DIM,
                                               c * KV_CHUNK:(c + 1) * KV_CHUNK].astype(BF16)
            v_ref[0, g, c, HEAD_DIM:V_ROWS, :] = ones_row
    ga_ref[...] = _dot(h, w_ref[:, COL_GA:COL_U]).astype(ga_ref.dtype)
    gh_ref[...] = _dot(h, w_ref[:, COL_GH:D_IN_PROJ]).astype(gh_ref.dtype)

    halo = xprev_ref.shape[0]
    hh = normed(jnp.concatenate([xprev_ref[...], xnext_ref[...]], axis=0))
    u_all = _dot(jnp.concatenate([h, hh], axis=0), w_ref[:, COL_U:COL_GH])
    tm = h.shape[0]
    u = u_all[:tm]
    uh = u_all[tm:]
    li = pl.program_id(0) % nlb
    prev_row = jnp.where(li == 0, 0.0, uh[halo - 1:halo])
    next_row = jnp.where(li == nlb - 1, 0.0, uh[halo:halo + 1])
    row = lax.broadcasted_iota(jnp.int32, u.shape, 0)
    um = jnp.where(row == 0, prev_row, pltpu.roll(u, 1, 0))
    up = jnp.where(row == tm - 1, next_row, pltpu.roll(u, tm - 1, 0))
    cw = cw_ref[...]
    res = um * cw[0:1] + u * cw[1:2] + up * cw[2:3] + cb_ref[...]
    for part in range(HYENA_ORDER + 1):
        for cblk in range(CH_BLOCKS):
            lo = part * HYENA_W + cblk * LANES
            parts_ref[part, 0, cblk] = res[:, lo:lo + LANES]


def _inproj(x2d, nw, w_bf, qnw, knw, pmat, ctab, stab, cw, cb, B, L):
    tm, halo = ROW_TILE, HALO_ROWS
    assert tm % KV_CHUNK == 0 and L % tm == 0
    T = B * L
    nlb = L // tm
    rpb = tm // halo
    UW = (HYENA_ORDER + 1) * HYENA_W
    row = lambda i: (i, 0)
    const = lambda i: (0, 0)
    pos = lambda i: (i % nlb, 0)
    heads = lambda i: (i // nlb, 0, i % nlb, 0)
    return pl.pallas_call(
        functools.partial(_inproj_body, nlb=nlb),
        grid=(T // tm,),
        in_specs=[
            pl.BlockSpec((tm, D_MODEL), row),
            pl.BlockSpec((halo, D_MODEL), lambda i: (jnp.maximum(i * rpb - 1, 0), 0)),
            pl.BlockSpec((halo, D_MODEL), lambda i: (jnp.minimum((i + 1) * rpb, T // halo - 1), 0)),
            pl.BlockSpec((1, D_MODEL), const),
            pl.BlockSpec((D_MODEL, D_IN_PROJ), const),
            pl.BlockSpec((1, LANES), const),
            pl.BlockSpec((1, LANES), const),
            pl.BlockSpec((LANES, LANES), const),
            pl.BlockSpec((tm, LANES), pos),
            pl.BlockSpec((tm, LANES), pos),
            pl.BlockSpec((SHORT_CONV, UW), const),
            pl.BlockSpec((1, UW), const),
        ],
        out_specs=[
            pl.BlockSpec((1, N_Q_HEADS, HEAD_DIM, tm), lambda i: (i // nlb, 0, 0, i % nlb)),
            pl.BlockSpec((1, N_KV_HEADS, tm, HEAD_DIM), heads),
            pl.BlockSpec((1, N_KV_HEADS, tm // KV_CHUNK, V_ROWS, KV_CHUNK),
                         lambda i: (i // nlb, 0, i % nlb, 0, 0)),
            pl.BlockSpec((tm, ATTN_W), row),
            pl.BlockSpec((HYENA_ORDER + 1, 1, CH_BLOCKS, tm, LANES), lambda i: (0, i // nlb, 0, i % nlb, 0)),
            pl.BlockSpec((tm, HYENA_W), row),
        ],
        out_shape=[
            jax.ShapeDtypeStruct((B, N_Q_HEADS, HEAD_DIM, L), BF16),
            jax.ShapeDtypeStruct((B, N_KV_HEADS, L, HEAD_DIM), BF16),
            jax.ShapeDtypeStruct((B, N_KV_HEADS, L // KV_CHUNK, V_ROWS, KV_CHUNK), BF16),
            jax.ShapeDtypeStruct((T, ATTN_W), BF16),
            jax.ShapeDtypeStruct((HYENA_ORDER + 1, B, CH_BLOCKS, L, LANES), F32),
            jax.ShapeDtypeStruct((T, HYENA_W), BF16),
        ],
        compiler_params=_cparams("parallel"),
        name="inproj",
    )(x2d, x2d, x2d, nw, w_bf, qnw, knw, pmat, ctab, stab, cw, cb)


def _attn_body(qt_ref, k_ref, vt_ref, o_ref, s_ref, p_ref, al_ref, cm_ref, m_ref, acc_ref, *, tq, nk):
    cols = Q_PER_KV * tq
    ncb = cols // LANES
    m_ref[...] = jnp.full(m_ref.shape, -jnp.inf, F32)
    acc_ref[...] = jnp.zeros(acc_ref.shape, F32)

    def scores(j, slot):
        start = pl.multiple_of(j * KV_CHUNK, KV_CHUNK)
        k = k_ref[0, 0, pl.ds(start, KV_CHUNK), :]
        for h in range(Q_PER_KV):
            for q0 in range(0, tq, Q_SUB):
                c0 = h * tq + q0
                r = _dot(k, qt_ref[0, h, :, q0:q0 + Q_SUB])
                for i in range(Q_SUB // LANES):
                    s_ref[slot, c0 // LANES + i] = r[:, i * LANES:(i + 1) * LANES]
                cm_ref[slot, :, c0:c0 + Q_SUB] = jnp.max(r, axis=0, keepdims=True)

    def accumulate(j, slot):
        vt = vt_ref[0, 0, j]
        for i in range(cols // Q_SUB):
            cs = slice(i * Q_SUB, (i + 1) * Q_SUB)
            nb = Q_SUB // LANES
            p = jnp.concatenate([p_ref[slot, nb * i + t] for t in range(nb)], axis=1)
            acc_ref[:, cs] = al_ref[slot, :, cs] * acc_ref[:, cs] + _dot(vt, p)

    def softmax(slot):
        m_old = m_ref[...]
        m_new = jnp.maximum(m_old, cm_ref[slot])
        m_ref[...] = m_new
        al_ref[slot] = jnp.exp2(m_old - m_new)
        for c in range(ncb):
            p_ref[slot, c] = jnp.exp2(s_ref[slot, c] - m_new[:, c * LANES:(c + 1) * LANES]).astype(BF16)

    scores(0, 0)
    scores(1, 1)
    softmax(0)
    scores(2, 0)
    accumulate(0, 0)
    softmax(1)

    def step(jj, carry):
        j = 2 * jj
        scores(j + 1, 1)
        accumulate(j - 1, 1)
        softmax(0)
        scores(jnp.minimum(j + 2, nk - 1), 0)
        accumulate(j, 0)
        softmax(1)
        return carry

    lax.fori_loop(1, nk // 2, step, 0)
    accumulate(nk - 1, 1)
    o = acc_ref[0:HEAD_DIM, :] / acc_ref[HEAD_DIM:HEAD_DIM + 1, :]
    for h in range(Q_PER_KV):
        o_ref[0, h] = o[:, h * tq:(h + 1) * tq]


def _attention(qt, k, vt, B, L):
    tq = Q_TILE
    nk = L // KV_CHUNK
    assert nk % 2 == 0 and nk >= 4
    cols = Q_PER_KV * tq
    ncb = cols // LANES
    body = functools.partial(_attn_body, tq=tq, nk=nk)
    return pl.pallas_call(
        body,
        grid=(B, N_KV_HEADS, L // tq),
        in_specs=[
            pl.BlockSpec((1, Q_PER_KV, HEAD_DIM, tq), lambda b, g, i: (b, g, 0, i)),
            pl.BlockSpec((1, 1, L, HEAD_DIM), lambda b, g, i: (b, g, 0, 0)),
            pl.BlockSpec((1, 1, nk, V_ROWS, KV_CHUNK), lambda b, g, i: (b, g, 0, 0, 0)),
        ],
        out_specs=pl.BlockSpec((1, Q_PER_KV, HEAD_DIM, tq), lambda b, g, i: (b, g, 0, i)),
        out_shape=jax.ShapeDtypeStruct((B, N_Q_HEADS, HEAD_DIM, L), F32),
        scratch_shapes=[
            pltpu.VMEM((2, ncb, KV_CHUNK, LANES), F32),
            pltpu.VMEM((2, ncb, KV_CHUNK, LANES), BF16),
            pltpu.VMEM((2, 1, cols), F32),
            pltpu.VMEM((2, 1, cols), F32),
            pltpu.VMEM((1, cols), F32),
            pltpu.VMEM((V_ROWS, cols), F32),
        ],
        compiler_params=_cparams("parallel", "parallel", "parallel"),
        name="attention",
    )(qt, k, vt)


def _filt_body(z_ref, valid_ref, w1_ref, b1_ref, w2_ref, b2_ref, fr_ref, w3_ref, dec_ref,
               buf_ref, sum_ref):
    i = pl.program_id(0)
    z = z_ref[...]
    fr = fr_ref[...]
    h = jnp.sin(fr * (_dot_f32(z, w1_ref[...]) + b1_ref[...]))
    h = jnp.sin(fr * (_dot_f32(h, w2_ref[...]) + b2_ref[...]))
    h = jnp.concatenate([_dot_f32(h, w3_ref[0, 0]), _dot_f32(h, w3_ref[0, 1])], axis=0)
    t = jnp.concatenate([z[:, 0:1], z[:, FILT_HID:FILT_HID + 1]], axis=0)
    win = jnp.exp(-t * jnp.abs(dec_ref[0])) + MOD_SHIFT
    out = h * win * valid_ref[...]
    for c in range(FILT_BLOCKS):
        buf_ref[c] = out[:, c * LANES:(c + 1) * LANES]

    @pl.when(i == 0)
    def _():
        sum_ref[...] = jnp.zeros_like(sum_ref)

    sum_ref[...] += jnp.sum(jnp.abs(out), axis=0, keepdims=True)


def _filters(ztab, valid, w1, b1, w2, b2, fr, w3d, decd, L):
    rb = ROW_TILE
    n = 2 * L
    nb = n // rb
    half = nb // 2
    W = HYENA_ORDER * HYENA_W
    const = lambda i: (0, 0)
    return pl.pallas_call(
        _filt_body,
        grid=(nb,),
        in_specs=[
            pl.BlockSpec((rb // 2, LANES), lambda i: (i, 0)),
            pl.BlockSpec((rb, 1), lambda i: (i, 0)),
            pl.BlockSpec((LANES, LANES), const),
            pl.BlockSpec((1, LANES), const),
            pl.BlockSpec((LANES, LANES), const),
            pl.BlockSpec((1, LANES), const),
            pl.BlockSpec((1, LANES), const),
            pl.BlockSpec((1, 2, LANES, W), lambda i: (i // half, 0, 0, 0)),
            pl.BlockSpec((1, 1, W), lambda i: (i // half, 0, 0)),
        ],
        out_specs=[
            pl.BlockSpec((FILT_BLOCKS, rb, LANES), lambda i: (0, i, 0)),
            pl.BlockSpec((1, W), const),
        ],
        out_shape=[
            jax.ShapeDtypeStruct((FILT_BLOCKS, n, LANES), F32),
            jax.ShapeDtypeStruct((1, W), F32),
        ],
        compiler_params=_cparams("arbitrary"),
        name="filters",
    )(ztab, valid, w1, b1, w2, b2, fr, w3d, decd)


def _strided_rows(ref, lead):
    n = len(lead)
    nblk, count, stride = ref.shape[n:n + 3]
    return ref.reshape(ref.shape[:n + 1] + (count * stride, LANES)), nblk, count, stride


def _gather_rows(ref, lead, r):
    flat, nblk, count, stride = _strided_rows(ref, lead)
    return jnp.concatenate([flat[lead + (c, pl.ds(r, count, stride=stride), slice(None))]
                            for c in range(nblk)], axis=1)


def _scatter_rows(ref, lead, r, val):
    flat, nblk, count, stride = _strided_rows(ref, lead)
    for c in range(nblk):
        flat[lead + (c, pl.ds(r, count, stride=stride), slice(None))] = val[:, c * LANES:(c + 1) * LANES]


def _pack_pairs(y):
    return pltpu.bitcast(y.astype(BF16), PACKED)


def _unpack_pairs(w):
    return pltpu.bitcast(w, BF16)


def _tile_row(ref, lead, k):
    return jnp.concatenate([ref[lead + (c, k)] for c in range(ref.shape[len(lead)])], axis=1)


def _s1_body(a_ref, w_ref, o_ref):
    rb = a_ref.shape[3]
    for r in range(rb):
        z = jnp.concatenate([_gather_rows(a_ref, (m,), r) for m in range(2)], axis=0)
        y = _pack_pairs(_dot(w_ref[...], z.astype(BF16)))
        _scatter_rows(o_ref, (0,), r, y)


def _s1(src, which, w1m, P, N1):
    N1h = N1 // 2
    return pl.pallas_call(
        _s1_body,
        grid=(P, FFT_N2 // R_BLOCK),
        in_specs=[
            pl.BlockSpec((None, 2, None, CH_BLOCKS, N1h, R_BLOCK, LANES), lambda p, j: (which, 0, p, 0, 0, j, 0)),
            pl.BlockSpec((2 * N1, N1), lambda p, j: (0, 0)),
        ],
        out_specs=pl.BlockSpec((1, CH_BLOCKS, N1, R_BLOCK, LANES), lambda p, j: (p, 0, 0, j, 0)),
        out_shape=jax.ShapeDtypeStruct((P, CH_BLOCKS, N1, FFT_N2, LANES), PACKED),
        compiler_params=_cparams("parallel", "parallel"),
        name="dft_n1",
    )(src, w1m)


def _s1f_body(b_ref, s_ref, w_ref, o_ref):
    rb = b_ref.shape[2]
    for r in range(rb):
        z = (_gather_rows(b_ref, (), r) / s_ref[...]).astype(BF16)
        _scatter_rows(o_ref, (), r, _pack_pairs(_dot(w_ref[...], z)))


def _s1_filter(buf4, ssum, w1f, N1):
    W = FILT_BLOCKS * LANES
    return pl.pallas_call(
        _s1f_body,
        grid=(FFT_N2 // R_BLOCK,),
        in_specs=[
            pl.BlockSpec((FILT_BLOCKS, N1, R_BLOCK, LANES), lambda j: (0, 0, j, 0)),
            pl.BlockSpec((1, W), lambda j: (0, 0)),
            pl.BlockSpec((2 * N1, N1), lambda j: (0, 0)),
        ],
        out_specs=pl.BlockSpec((FILT_BLOCKS, N1, R_BLOCK, LANES), lambda j: (0, 0, j, 0)),
        out_shape=jax.ShapeDtypeStruct((FILT_BLOCKS, N1, FFT_N2, LANES), PACKED),
        compiler_params=_cparams("parallel"),
        name="dft_n1_filter",
    )(buf4, ssum, w1f)


def _midf_body(x_ref, g_ref, h_ref):
    for k in range(x_ref.shape[1]):
        z = _unpack_pairs(_tile_row(x_ref, (), k))
        h_ref[k] = _dot(g_ref[k], z)


def _mid_filter(hpre, g, N1):
    W = FILT_BLOCKS * LANES
    kb = K_BLOCK
    return pl.pallas_call(
        _midf_body,
        grid=(N1 // kb,),
        in_specs=[
            pl.BlockSpec((FILT_BLOCKS, kb, FFT_N2, LANES), lambda i: (0, i, 0, 0)),
            pl.BlockSpec((kb, 2 * FFT_N2, 2 * FFT_N2), lambda i: (i, 0, 0)),
        ],
        out_specs=pl.BlockSpec((kb, 2 * FFT_N2, W), lambda i: (i, 0, 0)),
        out_shape=jax.ShapeDtypeStruct((N1, 2 * FFT_N2, W), F32),
        compiler_params=_cparams("parallel"),
        name="dft_n2_filter",
    )(hpre, g)


def _mid_body(x_ref, g_ref, gi_ref, h_ref, o_ref, d_ref, y_ref):
    kb = x_ref.shape[2]

    def forward(k):
        z = _unpack_pairs(_tile_row(x_ref, (0,), k))
        d_ref[k % 2] = _dot(g_ref[k], z)

    def product(k):
        hh = h_ref[k]
        dr, di = d_ref[k % 2, :FFT_N2], d_ref[k % 2, FFT_N2:]
        hr, hi = hh[:FFT_N2], hh[FFT_N2:]
        y_ref[k % 2, :FFT_N2] = (dr * hr - di * hi).astype(BF16)
        y_ref[k % 2, FFT_N2:] = (dr * hi + di * hr).astype(BF16)

    def inverse(k):
        e = _dot(gi_ref[k], y_ref[k % 2])
        ew = _pack_pairs(e)
        for c in range(o_ref.shape[1]):
            o_ref[0, c, k] = ew[:, c * LANES:(c + 1) * LANES]

    forward(0)
    for k in range(kb):
        if k + 1 < kb:
            forward(k + 1)
        if k > 0:
            inverse(k - 1)
        product(k)
    inverse(kb - 1)


def _mid(x5, g, gi, hspec, order, P, N1):
    kb = K_BLOCK
    blk = pl.BlockSpec((1, CH_BLOCKS, kb, FFT_N2, LANES), lambda i, p: (p, 0, i, 0, 0))
    return pl.pallas_call(
        _mid_body,
        grid=(N1 // kb, P),
        in_specs=[
            blk,
            pl.BlockSpec((kb, 2 * FFT_N2, 2 * FFT_N2), lambda i, p: (i, 0, 0)),
            pl.BlockSpec((kb, 2 * FFT_N2, 2 * FFT_N2), lambda i, p: (i, 0, 0)),
            pl.BlockSpec((kb, 2 * FFT_N2, HYENA_W), lambda i, p: (i, 0, order)),
        ],
        out_specs=blk,
        out_shape=jax.ShapeDtypeStruct((P, CH_BLOCKS, N1, FFT_N2, LANES), PACKED),
        scratch_shapes=[pltpu.VMEM((2, 2 * FFT_N2, HYENA_W), F32),
                        pltpu.VMEM((2, 2 * FFT_N2, HYENA_W), BF16)],
        compiler_params=_cparams("parallel", "parallel"),
        name="dft_n2_conv",
    )(x5, g, gi, hspec)


def _s1inv_body(e_ref, w_ref, src_ref, mul_ref, bias_ref, *rest, chain):
    if chain:
        w1_ref, o_ref, o1_ref, y_ref = rest
    else:
        o_ref, y_ref = rest
    n1h, rb = src_ref.shape[2:4]
    for r in range(rb):
        e = _unpack_pairs(_gather_rows(e_ref, (0,), r))
        y = _dot(w_ref[...], e)
        for ro in range(2):
            _scatter_rows(y_ref, (ro,), r, y[ro * n1h:(ro + 1) * n1h])
    zz = mul_ref[...] * (y_ref[...] + src_ref[...] * bias_ref[...][None, :, None])
    o_ref[...] = zz
    if chain:
        y_ref[...] = zz
        _s1_body(y_ref, w1_ref, o1_ref)


def _s1inv(e5, w1i, src, src_which, mul, mul_which, bias3, P, N1, w1m=None):
    N1h = N1 // 2
    chain = w1m is not None
    member = lambda which: pl.BlockSpec((None, 2, None, CH_BLOCKS, N1h, R_BLOCK, LANES),
                                        lambda p, j: (which, 0, p, 0, 0, j, 0))
    packed_blk = pl.BlockSpec((1, CH_BLOCKS, N1, R_BLOCK, LANES), lambda p, j: (p, 0, 0, j, 0))
    in_specs = [
        packed_blk,
        pl.BlockSpec((N1, 2 * N1), lambda p, j: (0, 0)),
        member(src_which),
        member(mul_which),
        pl.BlockSpec((CH_BLOCKS, 1, LANES), lambda p, j: (0, 0, 0)),
    ]
    out_specs = [member(0)]
    out_shape = [jax.ShapeDtypeStruct((1, 2, P, CH_BLOCKS, N1h, FFT_N2, LANES), F32)]
    args = [e5, w1i, src, mul, bias3]
    if chain:
        in_specs.append(pl.BlockSpec((2 * N1, N1), lambda p, j: (0, 0)))
        out_specs.append(packed_blk)
        out_shape.append(jax.ShapeDtypeStruct((P, CH_BLOCKS, N1, FFT_N2, LANES), PACKED))
        args.append(w1m)
    return pl.pallas_call(
        functools.partial(_s1inv_body, chain=chain),
        grid=(P, FFT_N2 // R_BLOCK),
        in_specs=in_specs,
        out_specs=out_specs,
        out_shape=out_shape,
        scratch_shapes=[pltpu.VMEM((2, CH_BLOCKS, N1h, R_BLOCK, LANES), F32)],
        compiler_params=_cparams("parallel", "parallel"),
        name="idft_n1_dft_n1" if chain else "idft_n1",
    )(*args)


def _outproj_body(a_ref, ga_ref, z_ref, gh_ref, x_ref, wa_ref, wh_ref, w_ref, fw_ref, o_ref, *, final):
    def norm_gate(y, w, g):
        ms = jnp.mean(y * y, axis=-1, keepdims=True)
        g = g.astype(F32)
        return (y * lax.rsqrt(ms + EPS) * w) * (g * (1.0 / (1.0 + jnp.exp(-g))))

    attn = a_ref[0].reshape(ATTN_W, a_ref.shape[-1]).T
    oa = norm_gate(attn, wa_ref[...], ga_ref[...]).astype(BF16)
    zz = jnp.concatenate([z_ref[0, c] for c in range(CH_BLOCKS)], axis=1)
    oh = norm_gate(zz, wh_ref[...], gh_ref[...]).astype(BF16)
    y = x_ref[...] + (_dot(oa, w_ref[0:ATTN_W, :]) + _dot(oh, w_ref[ATTN_W:MIX_W, :]))
    if final:
        ms = jnp.mean(y * y, axis=-1, keepdims=True)
        y = y * lax.rsqrt(ms + EPS) * fw_ref[...]
    o_ref[...] = y


def _outproj(attn_t, ga, zz4, gh, x2d, wa, wh, w_bf, fw, final):
    tm = ROW_TILE
    T = x2d.shape[0]
    nlb = zz4.shape[2] // tm
    row = lambda i: (i, 0)
    const = lambda i: (0, 0)
    body = functools.partial(_outproj_body, final=final)
    return pl.pallas_call(
        body,
        grid=(T // tm,),
        in_specs=[
            pl.BlockSpec((1, N_Q_HEADS, HEAD_DIM, tm), lambda i: (i // nlb, 0, 0, i % nlb)),
            pl.BlockSpec((tm, ATTN_W), row),
            pl.BlockSpec((1, CH_BLOCKS, tm, LANES), lambda i: (i // nlb, 0, i % nlb, 0)),
            pl.BlockSpec((tm, HYENA_W), row),
            pl.BlockSpec((tm, D_MODEL), row),
            pl.BlockSpec((1, ATTN_W), const),
            pl.BlockSpec((1, HYENA_W), const),
            pl.BlockSpec((MIX_W, D_MODEL), const),
            pl.BlockSpec((1, D_MODEL), const),
        ],
        out_specs=pl.BlockSpec((tm, D_MODEL), row),
        out_shape=jax.ShapeDtypeStruct((T, D_MODEL), F32),
        compiler_params=_cparams("parallel"),
        name="outproj",
    )(attn_t, ga, zz4, gh, x2d, wa, wh, w_bf, fw)


def _rope_tables(L):
    t = jnp.arange(L, dtype=jnp.int32)
    pos = jnp.stack([t // GRID_W, t % GRID_W], axis=-1).astype(F32)
    freqs = ROPE_THETA ** (-jnp.arange(ROPE_FREQS, dtype=F32) / ROPE_FREQS)
    ang = pos[:, :, None] * freqs
    cos, sin = jnp.cos(ang), jnp.sin(ang)
    c_head = jnp.stack([cos, cos], axis=2).reshape(L, HEAD_DIM)
    s_head = jnp.stack([-sin, sin], axis=2).reshape(L, HEAD_DIM)
    reps = LANES // HEAD_DIM
    return jnp.tile(c_head, (1, reps)), jnp.tile(s_head, (1, reps))


def _head_mean_matrix():
    idx = jnp.arange(LANES) // HEAD_DIM
    return jnp.where(idx[:, None] == idx[None, :], 1.0 / HEAD_DIM, 0.0).astype(BF16)


def _dft_tables(L):
    N = 2 * L
    N1 = N // FFT_N2
    N1h = N1 // 2
    two_pi = 2.0 * math.pi
    k1 = jnp.arange(N1, dtype=jnp.int32)
    th = ((k1[:, None] * k1[None, :]) % N1).astype(F32) * (two_pi / N1)
    c, s = jnp.cos(th), jnp.sin(th)
    ch, sh = c[:, :N1h], s[:, :N1h]
    w1 = jnp.stack([jnp.concatenate([ch, sh], axis=1),
                    jnp.concatenate([-sh, ch], axis=1)], axis=1).reshape(2 * N1, N1)
    w1f = jnp.stack([c, -s], axis=1).reshape(2 * N1, N1)
    ct, st = ch.T, sh.T
    top = jnp.stack([ct, -st], axis=2).reshape(N1h, 2 * N1)
    bot = jnp.stack([st, ct], axis=2).reshape(N1h, 2 * N1)
    w1i = jnp.concatenate([top, bot], axis=0) * (1.0 / N)
    k2 = jnp.arange(FFT_N2, dtype=jnp.int32)
    freq = k1[:, None, None] + N1 * k2[None, :, None]
    ph = ((freq * k2[None, None, :]) % N).astype(F32) * (two_pi / N)
    cp, sp = jnp.cos(ph), jnp.sin(ph)
    g = jnp.concatenate([jnp.stack([cp, sp], axis=3).reshape(N1, FFT_N2, 2 * FFT_N2),
                         jnp.stack([-sp, cp], axis=3).reshape(N1, FFT_N2, 2 * FFT_N2)], axis=1)
    cpt, spt = jnp.swapaxes(cp, 1, 2), jnp.swapaxes(sp, 1, 2)
    gi = jnp.stack([jnp.concatenate([cpt, -spt], axis=2),
                    jnp.concatenate([spt, cpt], axis=2)], axis=2).reshape(N1, 2 * FFT_N2, 2 * FFT_N2)
    return w1.astype(BF16), w1f.astype(BF16), w1i.astype(BF16), g.astype(BF16), gi.astype(BF16)


def _filter_positions(L):
    m = jnp.arange(2 * L, dtype=jnp.int32)
    pos = jnp.where(m < L, m, 2 * L - m)
    valid = (m != L).astype(F32)[:, None]
    pos = jnp.where(m == L, 0, pos).astype(F32)
    t = pos / (L - 1)
    w = 2.0 * math.pi * pos / L
    bands = jnp.linspace(1e-4, FILT_BANDS - 1, FILT_BANDS, dtype=F32)
    ang = w[:, None] * bands[None, :]
    z = jnp.concatenate([t[:, None], jnp.cos(ang), -jnp.sin(ang)], axis=-1)
    z = jnp.pad(z, ((0, 0), (0, FILT_HID - FILT_EMB)))
    hb = ROW_TILE // 2
    z = z.reshape(2 * L // ROW_TILE, 2, hb, FILT_HID)
    z = jnp.transpose(z, (0, 2, 1, 3)).reshape(L, 2 * FILT_HID)
    return z, valid


def _two_row_weights(w):
    zero = jnp.zeros_like(w)
    return jnp.concatenate([jnp.concatenate([w, zero], axis=1), jnp.concatenate([zero, w], axis=1)], axis=0)


def _pad_to(a, shape):
    return jnp.pad(a, [(0, s - d) for d, s in zip(a.shape, shape)])


def _trunk(x, norm_w, w_in, q_norm_w, k_norm_w, conv_w, conv_b, filt_w1, filt_b1,
           filt_w2, filt_b2, filt_w3, filt_freq, filt_decay, hyena_bias,
           attn_out_norm_w, hyena_out_norm_w, w_out, final_norm_w):
    B, L, _ = x.shape
    assert B % 2 == 0 and L % 1024 == 0
    P = B // 2
    N1 = 2 * L // FFT_N2
    N1h = N1 // 2
    T = B * L

    ctab, stab = _rope_tables(L)
    pmat = _head_mean_matrix()
    w1m, w1f, w1i, gmat, gimat = _dft_tables(L)
    ztab, valid = _filter_positions(L)
    reps = LANES // HEAD_DIM

    x2d = x.reshape(T, D_MODEL)
    for l in range(DEPTH):
        q, k, v, ga, uc, gh = _inproj(
            x2d, norm_w[l][None], w_in[l].astype(BF16),
            jnp.tile(q_norm_w[l], reps)[None], jnp.tile(k_norm_w[l], reps)[None],
            pmat, ctab, stab, conv_w[l], conv_b[l][None], B, L)
        attn = _attention(q, k, v, B, L)

        W = HYENA_ORDER * HYENA_W
        w3 = filt_w3[l].reshape(FILT_HID, HYENA_ORDER, 2, HYENA_W)
        w3 = jnp.transpose(w3, (2, 0, 1, 3)).reshape(2, FILT_HID, W)
        zero = jnp.zeros_like(w3)
        w3d = jnp.stack([jnp.concatenate([w3, zero], axis=1), jnp.concatenate([zero, w3], axis=1)], axis=1)
        dec = filt_decay[l].reshape(HYENA_ORDER, 2, HYENA_W)
        decd = jnp.transpose(dec, (1, 0, 2)).reshape(2, 1, W)
        hid = (FILT_HID, FILT_HID)
        buf, ssum = _filters(
            ztab, valid,
            _two_row_weights(_pad_to(filt_w1[l], hid)), jnp.tile(filt_b1[l], 2)[None],
            _two_row_weights(filt_w2[l]), jnp.tile(filt_b2[l], 2)[None],
            jnp.tile(filt_freq[l], 2)[None], w3d, decd, L)
        hpre = _s1_filter(buf.reshape(FILT_BLOCKS, N1, FFT_N2, LANES), ssum, w1f, N1)
        hspec = _mid_filter(hpre, gmat, N1)

        parts = uc.reshape(HYENA_ORDER + 1, 2, P, CH_BLOCKS, N1h, FFT_N2, LANES)
        zz = parts
        o1 = _s1(zz, 0, w1m, P, N1)
        for o in range(HYENA_ORDER):
            e = _mid(o1, gmat, gimat, hspec, o, P, N1)
            bias3 = hyena_bias[l, o].reshape(CH_BLOCKS, 1, LANES)
            if o + 1 < HYENA_ORDER:
                zz, o1 = _s1inv(e, w1i, zz, 0, parts, o + 1, bias3, P, N1, w1m=w1m)
            else:
                zz, = _s1inv(e, w1i, zz, 0, parts, o + 1, bias3, P, N1)
        zz4 = zz.reshape(B, CH_BLOCKS, L, LANES)

        x2d = _outproj(attn, ga, zz4, gh, x2d,
                       attn_out_norm_w[l][None], hyena_out_norm_w[l][None],
                       w_out[l].astype(BF16), final_norm_w[None], final=(l == DEPTH - 1))
    return x2d.reshape(B, L, D_MODEL)


def kernel(x_prompt, x_sample, norm_w, w_in, q_norm_w, k_norm_w, conv_w, conv_b, filt_w1, filt_b1, filt_w2, filt_b2, filt_w3, filt_freq, filt_decay, hyena_bias, attn_out_norm_w, hyena_out_norm_w, w_out, final_norm_w):
    weights = (norm_w, w_in, q_norm_w, k_norm_w, conv_w, conv_b, filt_w1, filt_b1, filt_w2, filt_b2,
               filt_w3, filt_freq, filt_decay, hyena_bias, attn_out_norm_w, hyena_out_norm_w, w_out,
               final_norm_w)
    return (_trunk(x_prompt, *weights), _trunk(x_sample, *weights))
```

```python
import functools
import math

import jax
import jax.numpy as jnp
from jax import lax
from jax.experimental import pallas as pl
from jax.experimental.pallas import tpu as pltpu

F32 = jnp.float32
BF16 = jnp.bfloat16
PACKED = jnp.uint32

D_MODEL = 1024
DEPTH = 4
GRID_W = 64
HEAD_DIM = 64
N_Q_HEADS = 8
N_KV_HEADS = 2
Q_PER_KV = N_Q_HEADS // N_KV_HEADS
ATTN_W = N_Q_HEADS * HEAD_DIM
KV_W = N_KV_HEADS * HEAD_DIM
HYENA_W = 512
HYENA_ORDER = 2
MIX_W = ATTN_W + HYENA_W
SHORT_CONV = 3
FILT_EMB = 33
FILT_BANDS = 16
FILT_HID = 64
N_FILT = HYENA_ORDER * 2 * HYENA_W
MOD_SHIFT = 0.05
ROPE_THETA = 10000.0
ROPE_FREQS = HEAD_DIM // 4
EPS = 1e-6
COL_K = ATTN_W
COL_V = COL_K + KV_W
COL_GA = COL_V + KV_W
COL_U = COL_GA + ATTN_W
COL_GH = COL_U + (HYENA_ORDER + 1) * HYENA_W
D_IN_PROJ = COL_GH + HYENA_W

ROW_TILE = 512
HALO_ROWS = 8
QK_SCALE = math.log2(math.e) / math.sqrt(HEAD_DIM)
KV_CHUNK = 512
Q_TILE = 512
Q_SUB = 256
V_ROWS = HEAD_DIM + 16
LANES = 128
FFT_N2 = 128
CH_BLOCKS = HYENA_W // LANES
FILT_BLOCKS = HYENA_ORDER * CH_BLOCKS
R_BLOCK = 8
K_BLOCK = 8
VMEM_LIMIT_BYTES = 48 * 1024 * 1024
HIGHEST = lax.Precision.HIGHEST


def _cparams(*sem):
    return pltpu.CompilerParams(dimension_semantics=sem, vmem_limit_bytes=VMEM_LIMIT_BYTES)


def _dot(a, b):
    return jnp.dot(a, b, preferred_element_type=F32)


def _dot_f32(a, b):
    return jnp.dot(a, b, preferred_element_type=F32, precision=HIGHEST)


def _inproj_body(x_ref, xprev_ref, xnext_ref, nw_ref, w_ref, qnw_ref, knw_ref, p_ref, c_ref, s_ref,
                 cw_ref, cb_ref, q_ref, k_ref, v_ref, ga_ref, parts_ref, gh_ref, *, nlb):
    def normed(x):
        ms = jnp.mean(x * x, axis=-1, keepdims=True)
        return (x * lax.rsqrt(ms + EPS) * nw_ref[...]).astype(BF16)

    h = normed(x_ref[...])
    p = p_ref[...]
    c = c_ref[...]
    s = s_ref[...]
    lane = lax.broadcasted_iota(jnp.int32, c.shape, 1)
    first_half = (lane % (2 * ROPE_FREQS)) < ROPE_FREQS

    def norm_rope(y, nw):
        y2 = y * y
        hi = y2.astype(BF16)
        lo = (y2 - hi.astype(F32)).astype(BF16)
        msq = _dot(hi, p) + _dot(lo, p)
        yn = y * lax.rsqrt(msq + EPS) * nw
        partner = jnp.where(first_half, pltpu.roll(yn, LANES - ROPE_FREQS, 1),
                            pltpu.roll(yn, ROPE_FREQS, 1))
        return yn * c + partner * s

    q = _dot(h, w_ref[:, 0:COL_K])
    qnw = qnw_ref[...]
    for j in range(ATTN_W // LANES):
        r = norm_rope(q[:, j * LANES:(j + 1) * LANES], qnw) * QK_SCALE
        rt = r.T
        q_ref[0, 2 * j] = rt[:HEAD_DIM].astype(BF16)
        q_ref[0, 2 * j + 1] = rt[HEAD_DIM:].astype(BF16)
    kv = _dot(h, w_ref[:, COL_K:COL_GA])
    kr = norm_rope(kv[:, :KV_W], knw_ref[...])
    k_ref[0, 0] = kr[:, :HEAD_DIM].astype(BF16)
    k_ref[0, 1] = pltpu.roll(kr, HEAD_DIM, 1)[:, :HEAD_DIM].astype(BF16)
    vt = kv[:, KV_W:].T
    extra = lax.broadcasted_iota(jnp.int32, (V_ROWS - HEAD_DIM, KV_CHUNK), 0)
    ones_row = jnp.where(extra == 0, 1.0, 0.0).astype(BF16)
    for c in range(vt.shape[1] // KV_CHUNK):
        for g in range(N_KV_HEADS):
            v_ref[0, g, c, 0:HEAD_DIM, :] = vt[g * HEAD_DIM:(g + 1) * HEAD_DIM,
                                               c * KV_CHUNK:(c + 1) * KV_CHUNK].astype(BF16)
            v_ref[0, g, c, HEAD_DIM:V_ROWS, :] = ones_row
    ga_ref[...] = _dot(h, w_ref[:, COL_GA:COL_U]).astype(ga_ref.dtype)
    gh_ref[...] = _dot(h, w_ref[:, COL_GH:D_IN_PROJ]).astype(gh_ref.dtype)

    u = _dot(h, w_ref[:, COL_U:COL_GH])
    halo = xprev_ref.shape[0]
    hh = normed(jnp.concatenate([xprev_ref[...], xnext_ref[...]], axis=0))
    uh = _dot(hh, w_ref[:, COL_U:COL_GH])
    li = pl.program_id(0) % nlb
    prev_row = jnp.where(li == 0, 0.0, uh[halo - 1:halo])
    next_row = jnp.where(li == nlb - 1, 0.0, uh[halo:halo + 1])
    tm = u.shape[0]
    row = lax.broadcasted_iota(jnp.int32, u.shape, 0)
    um = jnp.where(row == 0, prev_row, pltpu.roll(u, 1, 0))
    up = jnp.where(row == tm - 1, next_row, pltpu.roll(u, tm - 1, 0))
    cw = cw_ref[...]
    res = um * cw[0:1] + u * cw[1:2] + up * cw[2:3] + cb_ref[...]
    for part in range(HYENA_ORDER + 1):
        for cblk in range(CH_BLOCKS):
            lo = part * HYENA_W + cblk * LANES
            parts_ref[part, 0, cblk] = res[:, lo:lo + LANES]


def _inproj(x2d, nw, w_bf, qnw, knw, pmat, ctab, stab, cw, cb, B, L):
    tm, halo = ROW_TILE, HALO_ROWS
    assert tm % KV_CHUNK == 0 and L % tm == 0
    T = B * L
    nlb = L // tm
    rpb = tm // halo
    UW = (HYENA_ORDER + 1) * HYENA_W
    row = lambda i: (i, 0)
    const = lambda i: (0, 0)
    pos = lambda i: (i % nlb, 0)
    heads = lambda i: (i // nlb, 0, i % nlb, 0)
    return pl.pallas_call(
        functools.partial(_inproj_body, nlb=nlb),
        grid=(T // tm,),
        in_specs=[
            pl.BlockSpec((tm, D_MODEL), row),
            pl.BlockSpec((halo, D_MODEL), lambda i: (jnp.maximum(i * rpb - 1, 0), 0)),
            pl.BlockSpec((halo, D_MODEL), lambda i: (jnp.minimum((i + 1) * rpb, T // halo - 1), 0)),
            pl.BlockSpec((1, D_MODEL), const),
            pl.BlockSpec((D_MODEL, D_IN_PROJ), const),
            pl.BlockSpec((1, LANES), const),
            pl.BlockSpec((1, LANES), const),
            pl.BlockSpec((LANES, LANES), const),
            pl.BlockSpec((tm, LANES), pos),
            pl.BlockSpec((tm, LANES), pos),
            pl.BlockSpec((SHORT_CONV, UW), const),
            pl.BlockSpec((1, UW), const),
        ],
        out_specs=[
            pl.BlockSpec((1, N_Q_HEADS, HEAD_DIM, tm), lambda i: (i // nlb, 0, 0, i % nlb)),
            pl.BlockSpec((1, N_KV_HEADS, tm, HEAD_DIM), heads),
            pl.BlockSpec((1, N_KV_HEADS, tm // KV_CHUNK, V_ROWS, KV_CHUNK),
                         lambda i: (i // nlb, 0, i % nlb, 0, 0)),
            pl.BlockSpec((tm, ATTN_W), row),
            pl.BlockSpec((HYENA_ORDER + 1, 1, CH_BLOCKS, tm, LANES), lambda i: (0, i // nlb, 0, i % nlb, 0)),
            pl.BlockSpec((tm, HYENA_W), row),
        ],
        out_shape=[
            jax.ShapeDtypeStruct((B, N_Q_HEADS, HEAD_DIM, L), BF16),
            jax.ShapeDtypeStruct((B, N_KV_HEADS, L, HEAD_DIM), BF16),
            jax.ShapeDtypeStruct((B, N_KV_HEADS, L // KV_CHUNK, V_ROWS, KV_CHUNK), BF16),
            jax.ShapeDtypeStruct((T, ATTN_W), BF16),
            jax.ShapeDtypeStruct((HYENA_ORDER + 1, B, CH_BLOCKS, L, LANES), F32),
            jax.ShapeDtypeStruct((T, HYENA_W), BF16),
        ],
        compiler_params=_cparams("parallel"),
        name="inproj",
    )(x2d, x2d, x2d, nw, w_bf, qnw, knw, pmat, ctab, stab, cw, cb)


def _attn_body(qt_ref, k_ref, vt_ref, o_ref, s_ref, p_ref, al_ref, cm_ref, m_ref, acc_ref, *, tq, nk):
    cols = Q_PER_KV * tq
    ncb = cols // LANES
    m_ref[...] = jnp.full(m_ref.shape, -jnp.inf, F32)
    acc_ref[...] = jnp.zeros(acc_ref.shape, F32)

    def scores(j, slot):
        start = pl.multiple_of(j * KV_CHUNK, KV_CHUNK)
        k = k_ref[0, 0, pl.ds(start, KV_CHUNK), :]
        for h in range(Q_PER_KV):
            for q0 in range(0, tq, Q_SUB):
                c0 = h * tq + q0
                r = _dot(k, qt_ref[0, h, :, q0:q0 + Q_SUB])
                for i in range(Q_SUB // LANES):
                    s_ref[slot, c0 // LANES + i] = r[:, i * LANES:(i + 1) * LANES]
                cm_ref[slot, :, c0:c0 + Q_SUB] = jnp.max(r, axis=0, keepdims=True)

    def accumulate(j, slot):
        vt = vt_ref[0, 0, j]
        for i in range(cols // Q_SUB):
            cs = slice(i * Q_SUB, (i + 1) * Q_SUB)
            nb = Q_SUB // LANES
            p = jnp.concatenate([p_ref[slot, nb * i + t] for t in range(nb)], axis=1)
            acc_ref[:, cs] = al_ref[slot, :, cs] * acc_ref[:, cs] + _dot(vt, p)

    def softmax(slot):
        m_old = m_ref[...]
        m_new = jnp.maximum(m_old, cm_ref[slot])
        m_ref[...] = m_new
        al_ref[slot] = jnp.exp2(m_old - m_new)
        for c in range(ncb):
            p_ref[slot, c] = jnp.exp2(s_ref[slot, c] - m_new[:, c * LANES:(c + 1) * LANES]).astype(BF16)

    scores(0, 0)
    scores(1, 1)
    softmax(0)
    scores(2, 0)
    accumulate(0, 0)
    softmax(1)

    def step(jj, carry):
        j = 2 * jj
        scores(j + 1, 1)
        accumulate(j - 1, 1)
        softmax(0)
        scores(jnp.minimum(j + 2, nk - 1), 0)
        accumulate(j, 0)
        softmax(1)
        return carry

    lax.fori_loop(1, nk // 2, step, 0)
    accumulate(nk - 1, 1)
    o = acc_ref[0:HEAD_DIM, :] / acc_ref[HEAD_DIM:HEAD_DIM + 1, :]
    for h in range(Q_PER_KV):
        o_ref[0, h] = o[:, h * tq:(h + 1) * tq]


def _attention(qt, k, vt, B, L):
    tq = Q_TILE
    nk = L // KV_CHUNK
    assert nk % 2 == 0 and nk >= 4
    cols = Q_PER_KV * tq
    ncb = cols // LANES
    body = functools.partial(_attn_body, tq=tq, nk=nk)
    return pl.pallas_call(
        body,
        grid=(B, N_KV_HEADS, L // tq),
        in_specs=[
            pl.BlockSpec((1, Q_PER_KV, HEAD_DIM, tq), lambda b, g, i: (b, g, 0, i)),
            pl.BlockSpec((1, 1, L, HEAD_DIM), lambda b, g, i: (b, g, 0, 0)),
            pl.BlockSpec((1, 1, nk, V_ROWS, KV_CHUNK), lambda b, g, i: (b, g, 0, 0, 0)),
        ],
        out_specs=pl.BlockSpec((1, Q_PER_KV, HEAD_DIM, tq), lambda b, g, i: (b, g, 0, i)),
        out_shape=jax.ShapeDtypeStruct((B, N_Q_HEADS, HEAD_DIM, L), F32),
        scratch_shapes=[
            pltpu.VMEM((2, ncb, KV_CHUNK, LANES), F32),
            pltpu.VMEM((2, ncb, KV_CHUNK, LANES), BF16),
            pltpu.VMEM((2, 1, cols), F32),
            pltpu.VMEM((2, 1, cols), F32),
            pltpu.VMEM((1, cols), F32),
            pltpu.VMEM((V_ROWS, cols), F32),
        ],
        compiler_params=_cparams("parallel", "parallel", "parallel"),
        name="attention",
    )(qt, k, vt)


def _filt_body(z_ref, valid_ref, w1_ref, b1_ref, w2_ref, b2_ref, fr_ref, w3_ref, dec_ref,
               buf_ref, sum_ref):
    i = pl.program_id(0)
    z = z_ref[...]
    fr = fr_ref[...]
    h = jnp.sin(fr * (_dot_f32(z, w1_ref[...]) + b1_ref[...]))
    h = jnp.sin(fr * (_dot_f32(h, w2_ref[...]) + b2_ref[...]))
    h = jnp.concatenate([_dot_f32(h, w3_ref[0, 0]), _dot_f32(h, w3_ref[0, 1])], axis=0)
    t = jnp.concatenate([z[:, 0:1], z[:, FILT_HID:FILT_HID + 1]], axis=0)
    win = jnp.exp(-t * jnp.abs(dec_ref[0])) + MOD_SHIFT
    out = h * win * valid_ref[...]
    for c in range(FILT_BLOCKS):
        buf_ref[c] = out[:, c * LANES:(c + 1) * LANES]

    @pl.when(i == 0)
    def _():
        sum_ref[...] = jnp.zeros_like(sum_ref)

    sum_ref[...] += jnp.sum(jnp.abs(out), axis=0, keepdims=True)


def _filters(ztab, valid, w1, b1, w2, b2, fr, w3d, decd, L):
    rb = ROW_TILE
    n = 2 * L
    nb = n // rb
    half = nb // 2
    W = HYENA_ORDER * HYENA_W
    const = lambda i: (0, 0)
    return pl.pallas_call(
        _filt_body,
        grid=(nb,),
        in_specs=[
            pl.BlockSpec((rb // 2, LANES), lambda i: (i, 0)),
            pl.BlockSpec((rb, 1), lambda i: (i, 0)),
            pl.BlockSpec((LANES, LANES), const),
            pl.BlockSpec((1, LANES), const),
            pl.BlockSpec((LANES, LANES), const),
            pl.BlockSpec((1, LANES), const),
            pl.BlockSpec((1, LANES), const),
            pl.BlockSpec((1, 2, LANES, W), lambda i: (i // half, 0, 0, 0)),
            pl.BlockSpec((1, 1, W), lambda i: (i // half, 0, 0)),
        ],
        out_specs=[
            pl.BlockSpec((FILT_BLOCKS, rb, LANES), lambda i: (0, i, 0)),
            pl.BlockSpec((1, W), const),
        ],
        out_shape=[
            jax.ShapeDtypeStruct((FILT_BLOCKS, n, LANES), F32),
            jax.ShapeDtypeStruct((1, W), F32),
        ],
        compiler_params=_cparams("arbitrary"),
        name="filters",
    )(ztab, valid, w1, b1, w2, b2, fr, w3d, decd)


def _strided_rows(ref, lead):
    n = len(lead)
    nblk, count, stride = ref.shape[n:n + 3]
    return ref.reshape(ref.shape[:n + 1] + (count * stride, LANES)), nblk, count, stride


def _gather_rows(ref, lead, r):
    flat, nblk, count, stride = _strided_rows(ref, lead)
    return jnp.concatenate([flat[lead + (c, pl.ds(r, count, stride=stride), slice(None))]
                            for c in range(nblk)], axis=1)


def _scatter_rows(ref, lead, r, val):
    flat, nblk, count, stride = _strided_rows(ref, lead)
    for c in range(nblk):
        flat[lead + (c, pl.ds(r, count, stride=stride), slice(None))] = val[:, c * LANES:(c + 1) * LANES]


def _pack_pairs(y):
    return pltpu.bitcast(y.astype(BF16), PACKED)


def _unpack_pairs(w):
    return pltpu.bitcast(w, BF16)


def _tile_row(ref, lead, k):
    return jnp.concatenate([ref[lead + (c, k)] for c in range(ref.shape[len(lead)])], axis=1)


def _s1_body(a_ref, w_ref, o_ref):
    rb = a_ref.shape[3]
    for r in range(rb):
        z = jnp.concatenate([_gather_rows(a_ref, (m,), r) for m in range(2)], axis=0)
        y = _pack_pairs(_dot(w_ref[...], z.astype(BF16)))
        _scatter_rows(o_ref, (0,), r, y)


def _s1(src, which, w1m, P, N1):
    N1h = N1 // 2
    return pl.pallas_call(
        _s1_body,
        grid=(P, FFT_N2 // R_BLOCK),
        in_specs=[
            pl.BlockSpec((None, 2, None, CH_BLOCKS, N1h, R_BLOCK, LANES), lambda p, j: (which, 0, p, 0, 0, j, 0)),
            pl.BlockSpec((2 * N1, N1), lambda p, j: (0, 0)),
        ],
        out_specs=pl.BlockSpec((1, CH_BLOCKS, N1, R_BLOCK, LANES), lambda p, j: (p, 0, 0, j, 0)),
        out_shape=jax.ShapeDtypeStruct((P, CH_BLOCKS, N1, FFT_N2, LANES), PACKED),
        compiler_params=_cparams("parallel", "parallel"),
        name="dft_n1",
    )(src, w1m)


def _s1f_body(b_ref, s_ref, w_ref, o_ref):
    rb = b_ref.shape[2]
    for r in range(rb):
        z = (_gather_rows(b_ref, (), r) / s_ref[...]).astype(BF16)
        _scatter_rows(o_ref, (), r, _pack_pairs(_dot(w_ref[...], z)))


def _s1_filter(buf4, ssum, w1f, N1):
    W = FILT_BLOCKS * LANES
    return pl.pallas_call(
        _s1f_body,
        grid=(FFT_N2 // R_BLOCK,),
        in_specs=[
            pl.BlockSpec((FILT_BLOCKS, N1, R_BLOCK, LANES), lambda j: (0, 0, j, 0)),
            pl.BlockSpec((1, W), lambda j: (0, 0)),
            pl.BlockSpec((2 * N1, N1), lambda j: (0, 0)),
        ],
        out_specs=pl.BlockSpec((FILT_BLOCKS, N1, R_BLOCK, LANES), lambda j: (0, 0, j, 0)),
        out_shape=jax.ShapeDtypeStruct((FILT_BLOCKS, N1, FFT_N2, LANES), PACKED),
        compiler_params=_cparams("parallel"),
        name="dft_n1_filter",
    )(buf4, ssum, w1f)


def _midf_body(x_ref, g_ref, h_ref):
    for k in range(x_ref.shape[1]):
        z = _unpack_pairs(_tile_row(x_ref, (), k))
        h = _dot(g_ref[k], z)
        for o in range(HYENA_ORDER):
            h_ref[o, k] = h[:, o * HYENA_W:(o + 1) * HYENA_W]


def _mid_filter(hpre, g, N1):
    W = FILT_BLOCKS * LANES
    kb = K_BLOCK
    return pl.pallas_call(
        _midf_body,
        grid=(N1 // kb,),
        in_specs=[
            pl.BlockSpec((FILT_BLOCKS, kb, FFT_N2, LANES), lambda i: (0, i, 0, 0)),
            pl.BlockSpec((kb, 2 * FFT_N2, 2 * FFT_N2), lambda i: (i, 0, 0)),
        ],
        out_specs=pl.BlockSpec((HYENA_ORDER, kb, 2 * FFT_N2, HYENA_W), lambda i: (0, i, 0, 0)),
        out_shape=jax.ShapeDtypeStruct((HYENA_ORDER, N1, 2 * FFT_N2, HYENA_W), F32),
        compiler_params=_cparams("parallel"),
        name="dft_n2_filter",
    )(hpre, g)


def _mid_body(x_ref, g_ref, gi_ref, h_ref, o_ref, d_ref, y_ref):
    kb = x_ref.shape[2]

    def forward(k):
        z = _unpack_pairs(_tile_row(x_ref, (0,), k))
        d_ref[k % 2] = _dot(g_ref[k], z)

    def product(k):
        hh = h_ref[k]
        dr, di = d_ref[k % 2, :FFT_N2], d_ref[k % 2, FFT_N2:]
        hr, hi = hh[:FFT_N2], hh[FFT_N2:]
        y_ref[k % 2, :FFT_N2] = (dr * hr - di * hi).astype(BF16)
        y_ref[k % 2, FFT_N2:] = (dr * hi + di * hr).astype(BF16)

    def inverse(k):
        e = _dot(gi_ref[k], y_ref[k % 2])
        ew = _pack_pairs(e)
        for c in range(o_ref.shape[1]):
            o_ref[0, c, k] = ew[:, c * LANES:(c + 1) * LANES]

    forward(0)
    for k in range(kb):
        if k + 1 < kb:
            forward(k + 1)
        if k > 0:
            inverse(k - 1)
        product(k)
    inverse(kb - 1)


def _mid(x5, g, gi, hspec, order, P, N1):
    kb = K_BLOCK
    blk = pl.BlockSpec((1, CH_BLOCKS, kb, FFT_N2, LANES), lambda i, p: (p, 0, i, 0, 0))
    return pl.pallas_call(
        _mid_body,
        grid=(N1 // kb, P),
        in_specs=[
            blk,
            pl.BlockSpec((kb, 2 * FFT_N2, 2 * FFT_N2), lambda i, p: (i, 0, 0)),
            pl.BlockSpec((kb, 2 * FFT_N2, 2 * FFT_N2), lambda i, p: (i, 0, 0)),
            pl.BlockSpec((None, kb, 2 * FFT_N2, HYENA_W), lambda i, p: (order, i, 0, 0)),
        ],
        out_specs=blk,
        out_shape=jax.ShapeDtypeStruct((P, CH_BLOCKS, N1, FFT_N2, LANES), PACKED),
        scratch_shapes=[pltpu.VMEM((2, 2 * FFT_N2, HYENA_W), F32),
                        pltpu.VMEM((2, 2 * FFT_N2, HYENA_W), BF16)],
        compiler_params=_cparams("parallel", "parallel"),
        name="dft_n2_conv",
    )(x5, g, gi, hspec)


def _s1inv_body(e_ref, w_ref, src_ref, mul_ref, bias_ref, *rest, chain):
    if chain:
        w1_ref, o_ref, o1_ref, y_ref = rest
    else:
        o_ref, y_ref = rest
    n1h, rb = src_ref.shape[2:4]
    for r in range(rb):
        e = _unpack_pairs(_gather_rows(e_ref, (0,), r))
        y = _dot(w_ref[...], e)
        for ro in range(2):
            _scatter_rows(y_ref, (ro,), r, y[ro * n1h:(ro + 1) * n1h])
    zz = mul_ref[...] * (y_ref[...] + src_ref[...] * bias_ref[...][None, :, None])
    o_ref[...] = zz
    if chain:
        y_ref[...] = zz
        _s1_body(y_ref, w1_ref, o1_ref)


def _s1inv(e5, w1i, src, src_which, mul, mul_which, bias3, P, N1, w1m=None):
    N1h = N1 // 2
    chain = w1m is not None
    member = lambda which: pl.BlockSpec((None, 2, None, CH_BLOCKS, N1h, R_BLOCK, LANES),
                                        lambda p, j: (which, 0, p, 0, 0, j, 0))
    packed_blk = pl.BlockSpec((1, CH_BLOCKS, N1, R_BLOCK, LANES), lambda p, j: (p, 0, 0, j, 0))
    in_specs = [
        packed_blk,
        pl.BlockSpec((N1, 2 * N1), lambda p, j: (0, 0)),
        member(src_which),
        member(mul_which),
        pl.BlockSpec((CH_BLOCKS, 1, LANES), lambda p, j: (0, 0, 0)),
    ]
    out_specs = [member(0)]
    out_shape = [jax.ShapeDtypeStruct((1, 2, P, CH_BLOCKS, N1h, FFT_N2, LANES), F32)]
    args = [e5, w1i, src, mul, bias3]
    if chain:
        in_specs.append(pl.BlockSpec((2 * N1, N1), lambda p, j: (0, 0)))
        out_specs.append(packed_blk)
        out_shape.append(jax.ShapeDtypeStruct((P, CH_BLOCKS, N1, FFT_N2, LANES), PACKED))
        args.append(w1m)
    return pl.pallas_call(
        functools.partial(_s1inv_body, chain=chain),
        grid=(P, FFT_N2 // R_BLOCK),
        in_specs=in_specs,
        out_specs=out_specs,
        out_shape=out_shape,
        scratch_shapes=[pltpu.VMEM((2, CH_BLOCKS, N1h, R_BLOCK, LANES), F32)],
        compiler_params=_cparams("parallel", "parallel"),
        name="idft_n1_dft_n1" if chain else "idft_n1",
    )(*args)


def _outproj_body(a_ref, ga_ref, z_ref, gh_ref, x_ref, wa_ref, wh_ref, w_ref, fw_ref, o_ref, *, final):
    def norm_gate(y, w, g):
        ms = jnp.mean(y * y, axis=-1, keepdims=True)
        g = g.astype(F32)
        return (y * lax.rsqrt(ms + EPS) * w) * (g * (1.0 / (1.0 + jnp.exp(-g))))

    attn = a_ref[0].reshape(ATTN_W, a_ref.shape[-1]).T
    oa = norm_gate(attn, wa_ref[...], ga_ref[...]).astype(BF16)
    zz = jnp.concatenate([z_ref[0, c] for c in range(CH_BLOCKS)], axis=1)
    oh = norm_gate(zz, wh_ref[...], gh_ref[...]).astype(BF16)
    y = x_ref[...] + (_dot(oa, w_ref[0:ATTN_W, :]) + _dot(oh, w_ref[ATTN_W:MIX_W, :]))
    if final:
        ms = jnp.mean(y * y, axis=-1, keepdims=True)
        y = y * lax.rsqrt(ms + EPS) * fw_ref[...]
    o_ref[...] = y


def _outproj(attn_t, ga, zz4, gh, x2d, wa, wh, w_bf, fw, final):
    tm = ROW_TILE
    T = x2d.shape[0]
    nlb = zz4.shape[2] // tm
    row = lambda i: (i, 0)
    const = lambda i: (0, 0)
    body = functools.partial(_outproj_body, final=final)
    return pl.pallas_call(
        body,
        grid=(T // tm,),
        in_specs=[
            pl.BlockSpec((1, N_Q_HEADS, HEAD_DIM, tm), lambda i: (i // nlb, 0, 0, i % nlb)),
            pl.BlockSpec((tm, ATTN_W), row),
            pl.BlockSpec((1, CH_BLOCKS, tm, LANES), lambda i: (i // nlb, 0, i % nlb, 0)),
            pl.BlockSpec((tm, HYENA_W), row),
            pl.BlockSpec((tm, D_MODEL), row),
            pl.BlockSpec((1, ATTN_W), const),
            pl.BlockSpec((1, HYENA_W), const),
            pl.BlockSpec((MIX_W, D_MODEL), const),
            pl.BlockSpec((1, D_MODEL), const),
        ],
        out_specs=pl.BlockSpec((tm, D_MODEL), row),
        out_shape=jax.ShapeDtypeStruct((T, D_MODEL), F32),
        compiler_params=_cparams("parallel"),
        name="outproj",
    )(attn_t, ga, zz4, gh, x2d, wa, wh, w_bf, fw)


def _rope_tables(L):
    t = jnp.arange(L, dtype=jnp.int32)
    pos = jnp.stack([t // GRID_W, t % GRID_W], axis=-1).astype(F32)
    freqs = ROPE_THETA ** (-jnp.arange(ROPE_FREQS, dtype=F32) / ROPE_FREQS)
    ang = pos[:, :, None] * freqs
    cos, sin = jnp.cos(ang), jnp.sin(ang)
    c_head = jnp.stack([cos, cos], axis=2).reshape(L, HEAD_DIM)
    s_head = jnp.stack([-sin, sin], axis=2).reshape(L, HEAD_DIM)
    reps = LANES // HEAD_DIM
    return jnp.tile(c_head, (1, reps)), jnp.tile(s_head, (1, reps))


def _head_mean_matrix():
    idx = jnp.arange(LANES) // HEAD_DIM
    return jnp.where(idx[:, None] == idx[None, :], 1.0 / HEAD_DIM, 0.0).astype(BF16)


def _dft_tables(L):
    N = 2 * L
    N1 = N // FFT_N2
    N1h = N1 // 2
    two_pi = 2.0 * math.pi
    k1 = jnp.arange(N1, dtype=jnp.int32)
    th = ((k1[:, None] * k1[None, :]) % N1).astype(F32) * (two_pi / N1)
    c, s = jnp.cos(th), jnp.sin(th)
    ch, sh = c[:, :N1h], s[:, :N1h]
    w1 = jnp.stack([jnp.concatenate([ch, sh], axis=1),
                    jnp.concatenate([-sh, ch], axis=1)], axis=1).reshape(2 * N1, N1)
    w1f = jnp.stack([c, -s], axis=1).reshape(2 * N1, N1)
    ct, st = ch.T, sh.T
    top = jnp.stack([ct, -st], axis=2).reshape(N1h, 2 * N1)
    bot = jnp.stack([st, ct], axis=2).reshape(N1h, 2 * N1)
    w1i = jnp.concatenate([top, bot], axis=0) * (1.0 / N)
    k2 = jnp.arange(FFT_N2, dtype=jnp.int32)
    freq = k1[:, None, None] + N1 * k2[None, :, None]
    ph = ((freq * k2[None, None, :]) % N).astype(F32) * (two_pi / N)
    cp, sp = jnp.cos(ph), jnp.sin(ph)
    g = jnp.concatenate([jnp.stack([cp, sp], axis=3).reshape(N1, FFT_N2, 2 * FFT_N2),
                         jnp.stack([-sp, cp], axis=3).reshape(N1, FFT_N2, 2 * FFT_N2)], axis=1)
    cpt, spt = jnp.swapaxes(cp, 1, 2), jnp.swapaxes(sp, 1, 2)
    gi = jnp.stack([jnp.concatenate([cpt, -spt], axis=2),
                    jnp.concatenate([spt, cpt], axis=2)], axis=2).reshape(N1, 2 * FFT_N2, 2 * FFT_N2)
    return w1.astype(BF16), w1f.astype(BF16), w1i.astype(BF16), g.astype(BF16), gi.astype(BF16)


def _filter_positions(L):
    m = jnp.arange(2 * L, dtype=jnp.int32)
    pos = jnp.where(m < L, m, 2 * L - m)
    valid = (m != L).astype(F32)[:, None]
    pos = jnp.where(m == L, 0, pos).astype(F32)
    t = pos / (L - 1)
    w = 2.0 * math.pi * pos / L
    bands = jnp.linspace(1e-4, FILT_BANDS - 1, FILT_BANDS, dtype=F32)
    ang = w[:, None] * bands[None, :]
    z = jnp.concatenate([t[:, None], jnp.cos(ang), -jnp.sin(ang)], axis=-1)
    z = jnp.pad(z, ((0, 0), (0, FILT_HID - FILT_EMB)))
    hb = ROW_TILE // 2
    z = z.reshape(2 * L // ROW_TILE, 2, hb, FILT_HID)
    z = jnp.transpose(z, (0, 2, 1, 3)).reshape(L, 2 * FILT_HID)
    return z, valid


def _two_row_weights(w):
    zero = jnp.zeros_like(w)
    return jnp.concatenate([jnp.concatenate([w, zero], axis=1), jnp.concatenate([zero, w], axis=1)], axis=0)


def _pad_to(a, shape):
    return jnp.pad(a, [(0, s - d) for d, s in zip(a.shape, shape)])


def _trunk(x, norm_w, w_in, q_norm_w, k_norm_w, conv_w, conv_b, filt_w1, filt_b1,
           filt_w2, filt_b2, filt_w3, filt_freq, filt_decay, hyena_bias,
           attn_out_norm_w, hyena_out_norm_w, w_out, final_norm_w):
    B, L, _ = x.shape
    assert B % 2 == 0 and L % 1024 == 0
    P = B // 2
    N1 = 2 * L // FFT_N2
    N1h = N1 // 2
    T = B * L

    ctab, stab = _rope_tables(L)
    pmat = _head_mean_matrix()
    w1m, w1f, w1i, gmat, gimat = _dft_tables(L)
    ztab, valid = _filter_positions(L)
    reps = LANES // HEAD_DIM

    x2d = x.reshape(T, D_MODEL)
    for l in range(DEPTH):
        q, k, v, ga, uc, gh = _inproj(
            x2d, norm_w[l][None], w_in[l].astype(BF16),
            jnp.tile(q_norm_w[l], reps)[None], jnp.tile(k_norm_w[l], reps)[None],
            pmat, ctab, stab, conv_w[l], conv_b[l][None], B, L)
        attn = _attention(q, k, v, B, L)

        W = HYENA_ORDER * HYENA_W
        w3 = filt_w3[l].reshape(FILT_HID, HYENA_ORDER, 2, HYENA_W)
        w3 = jnp.transpose(w3, (2, 0, 1, 3)).reshape(2, FILT_HID, W)
        zero = jnp.zeros_like(w3)
        w3d = jnp.stack([jnp.concatenate([w3, zero], axis=1), jnp.concatenate([zero, w3], axis=1)], axis=1)
        dec = filt_decay[l].reshape(HYENA_ORDER, 2, HYENA_W)
        decd = jnp.transpose(dec, (1, 0, 2)).reshape(2, 1, W)
        hid = (FILT_HID, FILT_HID)
        buf, ssum = _filters(
            ztab, valid,
            _two_row_weights(_pad_to(filt_w1[l], hid)), jnp.tile(filt_b1[l], 2)[None],
            _two_row_weights(filt_w2[l]), jnp.tile(filt_b2[l], 2)[None],
            jnp.tile(filt_freq[l], 2)[None], w3d, decd, L)
        hpre = _s1_filter(buf.reshape(FILT_BLOCKS, N1, FFT_N2, LANES), ssum, w1f, N1)
        hspec = _mid_filter(hpre, gmat, N1)

        parts = uc.reshape(HYENA_ORDER + 1, 2, P, CH_BLOCKS, N1h, FFT_N2, LANES)
        zz = parts
        o1 = _s1(zz, 0, w1m, P, N1)
        for o in range(HYENA_ORDER):
            e = _mid(o1, gmat, gimat, hspec, o, P, N1)
            bias3 = hyena_bias[l, o].reshape(CH_BLOCKS, 1, LANES)
            if o + 1 < HYENA_ORDER:
                zz, o1 = _s1inv(e, w1i, zz, 0, parts, o + 1, bias3, P, N1, w1m=w1m)
            else:
                zz, = _s1inv(e, w1i, zz, 0, parts, o + 1, bias3, P, N1)
        zz4 = zz.reshape(B, CH_BLOCKS, L, LANES)

        x2d = _outproj(attn, ga, zz4, gh, x2d,
                       attn_out_norm_w[l][None], hyena_out_norm_w[l][None],
                       w_out[l].astype(BF16), final_norm_w[None], final=(l == DEPTH - 1))
    return x2d.reshape(B, L, D_MODEL)


def kernel(x_prompt, x_sample, norm_w, w_in, q_norm_w, k_norm_w, conv_w, conv_b, filt_w1, filt_b1, filt_w2, filt_b2, filt_w3, filt_freq, filt_decay, hyena_bias, attn_out_norm_w, hyena_out_norm_w, w_out, final_norm_w):
    weights = (norm_w, w_in, q_norm_w, k_norm_w, conv_w, conv_b, filt_w1, filt_b1, filt_w2, filt_b2,
               filt_w3, filt_freq, filt_decay, hyena_bias, attn_out_norm_w, hyena_out_norm_w, w_out,
               final_norm_w)
    return (_trunk(x_prompt, *weights), _trunk(x_sample, *weights))
```

```python
import functools
import math

import jax
import jax.numpy as jnp
from jax import lax
from jax.experimental import pallas as pl
from jax.experimental.pallas import tpu as pltpu

F32 = jnp.float32
BF16 = jnp.bfloat16
PACKED = jnp.uint32

D_MODEL = 1024
DEPTH = 4
GRID_W = 64
HEAD_DIM = 64
N_Q_HEADS = 8
N_KV_HEADS = 2
Q_PER_KV = N_Q_HEADS // N_KV_HEADS
ATTN_W = N_Q_HEADS * HEAD_DIM
KV_W = N_KV_HEADS * HEAD_DIM
HYENA_W = 512
HYENA_ORDER = 2
MIX_W = ATTN_W + HYENA_W
SHORT_CONV = 3
FILT_EMB = 33
FILT_BANDS = 16
FILT_HID = 64
N_FILT = HYENA_ORDER * 2 * HYENA_W
MOD_SHIFT = 0.05
ROPE_THETA = 10000.0
ROPE_FREQS = HEAD_DIM // 4
EPS = 1e-6
COL_K = ATTN_W
COL_V = COL_K + KV_W
COL_GA = COL_V + KV_W
COL_U = COL_GA + ATTN_W
COL_GH = COL_U + (HYENA_ORDER + 1) * HYENA_W
D_IN_PROJ = COL_GH + HYENA_W

ROW_TILE = 512
HALO_ROWS = 8
QK_SCALE = math.log2(math.e) / math.sqrt(HEAD_DIM)
KV_CHUNK = 512
Q_TILE = 512
Q_SUB = 256
V_ROWS = HEAD_DIM + 16
LANES = 128
FFT_N2 = 128
CH_BLOCKS = HYENA_W // LANES
FILT_BLOCKS = HYENA_ORDER * CH_BLOCKS
R_BLOCK = 8
K_BLOCK = 8
VMEM_LIMIT_BYTES = 48 * 1024 * 1024
HIGHEST = lax.Precision.HIGHEST


def _cparams(*sem):
    return pltpu.CompilerParams(dimension_semantics=sem, vmem_limit_bytes=VMEM_LIMIT_BYTES)


def _dot(a, b):
    return jnp.dot(a, b, preferred_element_type=F32)


def _dot_f32(a, b):
    return jnp.dot(a, b, preferred_element_type=F32, precision=HIGHEST)


def _inproj_body(x_ref, xprev_ref, xnext_ref, nw_ref, w_ref, qnw_ref, knw_ref, p_ref, c_ref, s_ref,
                 cw_ref, cb_ref, q_ref, k_ref, v_ref, ga_ref, parts_ref, gh_ref, *, nlb):
    def normed(x):
        ms = jnp.mean(x * x, axis=-1, keepdims=True)
        return (x * lax.rsqrt(ms + EPS) * nw_ref[...]).astype(BF16)

    h = normed(x_ref[...])
    p = p_ref[...]
    c = c_ref[...]
    s = s_ref[...]
    lane = lax.broadcasted_iota(jnp.int32, c.shape, 1)
    first_half = (lane % (2 * ROPE_FREQS)) < ROPE_FREQS

    def norm_rope(y, nw):
        y2 = y * y
        hi = y2.astype(BF16)
        lo = (y2 - hi.astype(F32)).astype(BF16)
        msq = _dot(hi, p) + _dot(lo, p)
        yn = y * lax.rsqrt(msq + EPS) * nw
        partner = jnp.where(first_half, pltpu.roll(yn, LANES - ROPE_FREQS, 1),
                            pltpu.roll(yn, ROPE_FREQS, 1))
        return yn * c + partner * s

    q = _dot(h, w_ref[:, 0:COL_K])
    qnw = qnw_ref[...]
    for j in range(ATTN_W // LANES):
        r = norm_rope(q[:, j * LANES:(j + 1) * LANES], qnw) * QK_SCALE
        rt = r.T
        q_ref[0, 2 * j] = rt[:HEAD_DIM].astype(BF16)
        q_ref[0, 2 * j + 1] = rt[HEAD_DIM:].astype(BF16)
    kv = _dot(h, w_ref[:, COL_K:COL_GA])
    kr = norm_rope(kv[:, :KV_W], knw_ref[...])
    k_ref[0, 0] = kr[:, :HEAD_DIM].astype(BF16)
    k_ref[0, 1] = pltpu.roll(kr, HEAD_DIM, 1)[:, :HEAD_DIM].astype(BF16)
    vt = kv[:, KV_W:].T
    extra = lax.broadcasted_iota(jnp.int32, (V_ROWS - HEAD_DIM, KV_CHUNK), 0)
    ones_row = jnp.where(extra == 0, 1.0, 0.0).astype(BF16)
    for c in range(vt.shape[1] // KV_CHUNK):
        for g in range(N_KV_HEADS):
            v_ref[0, g, c, 0:HEAD_DIM, :] = vt[g * HEAD_DIM:(g + 1) * HEAD_DIM,
                                               c * KV_CHUNK:(c + 1) * KV_CHUNK].astype(BF16)
            v_ref[0, g, c, HEAD_DIM:V_ROWS, :] = ones_row
    ga_ref[...] = _dot(h, w_ref[:, COL_GA:COL_U]).astype(ga_ref.dtype)
    gh_ref[...] = _dot(h, w_ref[:, COL_GH:D_IN_PROJ]).astype(gh_ref.dtype)

    u = _dot(h, w_ref[:, COL_U:COL_GH])
    halo = xprev_ref.shape[0]
    hh = normed(jnp.concatenate([xprev_ref[...], xnext_ref[...]], axis=0))
    uh = _dot(hh, w_ref[:, COL_U:COL_GH])
    li = pl.program_id(0) % nlb
    prev_row = jnp.where(li == 0, 0.0, uh[halo - 1:halo])
    next_row = jnp.where(li == nlb - 1, 0.0, uh[halo:halo + 1])
    tm = u.shape[0]
    row = lax.broadcasted_iota(jnp.int32, u.shape, 0)
    um = jnp.where(row == 0, prev_row, pltpu.roll(u, 1, 0))
    up = jnp.where(row == tm - 1, next_row, pltpu.roll(u, tm - 1, 0))
    cw = cw_ref[...]
    res = um * cw[0:1] + u * cw[1:2] + up * cw[2:3] + cb_ref[...]
    for part in range(HYENA_ORDER + 1):
        for cblk in range(CH_BLOCKS):
            lo = part * HYENA_W + cblk * LANES
            parts_ref[part, 0, cblk] = res[:, lo:lo + LANES]


def _inproj(x2d, nw, w_bf, qnw, knw, pmat, ctab, stab, cw, cb, B, L):
    tm, halo = ROW_TILE, HALO_ROWS
    assert tm % KV_CHUNK == 0 and L % tm == 0
    T = B * L
    nlb = L // tm
    rpb = tm // halo
    UW = (HYENA_ORDER + 1) * HYENA_W
    row = lambda i: (i, 0)
    const = lambda i: (0, 0)
    pos = lambda i: (i % nlb, 0)
    heads = lambda i: (i // nlb, 0, i % nlb, 0)
    return pl.pallas_call(
        functools.partial(_inproj_body, nlb=nlb),
        grid=(T // tm,),
        in_specs=[
            pl.BlockSpec((tm, D_MODEL), row),
            pl.BlockSpec((halo, D_MODEL), lambda i: (jnp.maximum(i * rpb - 1, 0), 0)),
            pl.BlockSpec((halo, D_MODEL), lambda i: (jnp.minimum((i + 1) * rpb, T // halo - 1), 0)),
            pl.BlockSpec((1, D_MODEL), const),
            pl.BlockSpec((D_MODEL, D_IN_PROJ), const),
            pl.BlockSpec((1, LANES), const),
            pl.BlockSpec((1, LANES), const),
            pl.BlockSpec((LANES, LANES), const),
            pl.BlockSpec((tm, LANES), pos),
            pl.BlockSpec((tm, LANES), pos),
            pl.BlockSpec((SHORT_CONV, UW), const),
            pl.BlockSpec((1, UW), const),
        ],
        out_specs=[
            pl.BlockSpec((1, N_Q_HEADS, HEAD_DIM, tm), lambda i: (i // nlb, 0, 0, i % nlb)),
            pl.BlockSpec((1, N_KV_HEADS, tm, HEAD_DIM), heads),
            pl.BlockSpec((1, N_KV_HEADS, tm // KV_CHUNK, V_ROWS, KV_CHUNK),
                         lambda i: (i // nlb, 0, i % nlb, 0, 0)),
            pl.BlockSpec((tm, ATTN_W), row),
            pl.BlockSpec((HYENA_ORDER + 1, 1, CH_BLOCKS, tm, LANES), lambda i: (0, i // nlb, 0, i % nlb, 0)),
            pl.BlockSpec((tm, HYENA_W), row),
        ],
        out_shape=[
            jax.ShapeDtypeStruct((B, N_Q_HEADS, HEAD_DIM, L), BF16),
            jax.ShapeDtypeStruct((B, N_KV_HEADS, L, HEAD_DIM), BF16),
            jax.ShapeDtypeStruct((B, N_KV_HEADS, L // KV_CHUNK, V_ROWS, KV_CHUNK), BF16),
            jax.ShapeDtypeStruct((T, ATTN_W), BF16),
            jax.ShapeDtypeStruct((HYENA_ORDER + 1, B, CH_BLOCKS, L, LANES), F32),
            jax.ShapeDtypeStruct((T, HYENA_W), BF16),
        ],
        compiler_params=_cparams("parallel"),
        name="inproj",
    )(x2d, x2d, x2d, nw, w_bf, qnw, knw, pmat, ctab, stab, cw, cb)


def _attn_body(qt_ref, k_ref, vt_ref, o_ref, s_ref, p_ref, al_ref, cm_ref, m_ref, acc_ref, *, tq, nk):
    cols = Q_PER_KV * tq
    ncb = cols // LANES
    m_ref[...] = jnp.full(m_ref.shape, -jnp.inf, F32)
    acc_ref[...] = jnp.zeros(acc_ref.shape, F32)

    def scores(j, slot):
        start = pl.multiple_of(j * KV_CHUNK, KV_CHUNK)
        k = k_ref[0, 0, pl.ds(start, KV_CHUNK), :]
        for h in range(Q_PER_KV):
            for q0 in range(0, tq, Q_SUB):
                c0 = h * tq + q0
                r = _dot(k, qt_ref[0, h, :, q0:q0 + Q_SUB])
                for i in range(Q_SUB // LANES):
                    s_ref[slot, c0 // LANES + i] = r[:, i * LANES:(i + 1) * LANES]
                cm_ref[slot, :, c0:c0 + Q_SUB] = jnp.max(r, axis=0, keepdims=True)

    def accumulate(j, slot):
        vt = vt_ref[0, 0, j]
        for i in range(cols // Q_SUB):
            cs = slice(i * Q_SUB, (i + 1) * Q_SUB)
            nb = Q_SUB // LANES
            p = jnp.concatenate([p_ref[slot, nb * i + t] for t in range(nb)], axis=1)
            acc_ref[:, cs] = al_ref[slot, :, cs] * acc_ref[:, cs] + _dot(vt, p)

    def softmax(slot):
        m_old = m_ref[...]
        m_new = jnp.maximum(m_old, cm_ref[slot])
        m_ref[...] = m_new
        al_ref[slot] = jnp.exp2(m_old - m_new)
        for c in range(ncb):
            p_ref[slot, c] = jnp.exp2(s_ref[slot, c] - m_new[:, c * LANES:(c + 1) * LANES]).astype(BF16)

    scores(0, 0)
    scores(1, 1)
    softmax(0)
    scores(2, 0)
    accumulate(0, 0)
    softmax(1)

    def step(jj, carry):
        j = 2 * jj
        scores(j + 1, 1)
        accumulate(j - 1, 1)
        softmax(0)
        scores(jnp.minimum(j + 2, nk - 1), 0)
        accumulate(j, 0)
        softmax(1)
        return carry

    lax.fori_loop(1, nk // 2, step, 0)
    accumulate(nk - 1, 1)
    o = acc_ref[0:HEAD_DIM, :] / acc_ref[HEAD_DIM:HEAD_DIM + 1, :]
    for h in range(Q_PER_KV):
        o_ref[0, h] = o[:, h * tq:(h + 1) * tq]


def _attention(qt, k, vt, B, L):
    tq = Q_TILE
    nk = L // KV_CHUNK
    assert nk % 2 == 0 and nk >= 4
    cols = Q_PER_KV * tq
    ncb = cols // LANES
    body = functools.partial(_attn_body, tq=tq, nk=nk)
    return pl.pallas_call(
        body,
        grid=(B, N_KV_HEADS, L // tq),
        in_specs=[
            pl.BlockSpec((1, Q_PER_KV, HEAD_DIM, tq), lambda b, g, i: (b, g, 0, i)),
            pl.BlockSpec((1, 1, L, HEAD_DIM), lambda b, g, i: (b, g, 0, 0)),
            pl.BlockSpec((1, 1, nk, V_ROWS, KV_CHUNK), lambda b, g, i: (b, g, 0, 0, 0)),
        ],
        out_specs=pl.BlockSpec((1, Q_PER_KV, HEAD_DIM, tq), lambda b, g, i: (b, g, 0, i)),
        out_shape=jax.ShapeDtypeStruct((B, N_Q_HEADS, HEAD_DIM, L), F32),
        scratch_shapes=[
            pltpu.VMEM((2, ncb, KV_CHUNK, LANES), F32),
            pltpu.VMEM((2, ncb, KV_CHUNK, LANES), BF16),
            pltpu.VMEM((2, 1, cols), F32),
            pltpu.VMEM((2, 1, cols), F32),
            pltpu.VMEM((1, cols), F32),
            pltpu.VMEM((V_ROWS, cols), F32),
        ],
        compiler_params=_cparams("parallel", "parallel", "parallel"),
        name="attention",
    )(qt, k, vt)


def _filt_body(z_ref, valid_ref, w1_ref, b1_ref, w2_ref, b2_ref, fr_ref, w3_ref, dec_ref,
               buf_ref, sum_ref):
    i = pl.program_id(0)
    z = z_ref[...]
    fr = fr_ref[...]
    h = jnp.sin(fr * (_dot_f32(z, w1_ref[...]) + b1_ref[...]))
    h = jnp.sin(fr * (_dot_f32(h, w2_ref[...]) + b2_ref[...]))
    h = jnp.concatenate([_dot_f32(h, w3_ref[0, 0]), _dot_f32(h, w3_ref[0, 1])], axis=0)
    t = jnp.concatenate([z[:, 0:1], z[:, FILT_HID:FILT_HID + 1]], axis=0)
    win = jnp.exp(-t * jnp.abs(dec_ref[0])) + MOD_SHIFT
    out = h * win * valid_ref[...]
    for c in range(FILT_BLOCKS):
        buf_ref[c] = out[:, c * LANES:(c + 1) * LANES]

    @pl.when(i == 0)
    def _():
        sum_ref[...] = jnp.zeros_like(sum_ref)

    sum_ref[...] += jnp.sum(jnp.abs(out), axis=0, keepdims=True)


def _filters(ztab, valid, w1, b1, w2, b2, fr, w3d, decd, L):
    rb = ROW_TILE
    n = 2 * L
    nb = n // rb
    half = nb // 2
    W = HYENA_ORDER * HYENA_W
    const = lambda i: (0, 0)
    return pl.pallas_call(
        _filt_body,
        grid=(nb,),
        in_specs=[
            pl.BlockSpec((rb // 2, LANES), lambda i: (i, 0)),
            pl.BlockSpec((rb, 1), lambda i: (i, 0)),
            pl.BlockSpec((LANES, LANES), const),
            pl.BlockSpec((1, LANES), const),
            pl.BlockSpec((LANES, LANES), const),
            pl.BlockSpec((1, LANES), const),
            pl.BlockSpec((1, LANES), const),
            pl.BlockSpec((1, 2, LANES, W), lambda i: (i // half, 0, 0, 0)),
            pl.BlockSpec((1, 1, W), lambda i: (i // half, 0, 0)),
        ],
        out_specs=[
            pl.BlockSpec((FILT_BLOCKS, rb, LANES), lambda i: (0, i, 0)),
            pl.BlockSpec((1, W), const),
        ],
        out_shape=[
            jax.ShapeDtypeStruct((FILT_BLOCKS, n, LANES), F32),
            jax.ShapeDtypeStruct((1, W), F32),
        ],
        compiler_params=_cparams("arbitrary"),
        name="filters",
    )(ztab, valid, w1, b1, w2, b2, fr, w3d, decd)


def _strided_rows(ref, lead):
    n = len(lead)
    nblk, count, stride = ref.shape[n:n + 3]
    return ref.reshape(ref.shape[:n + 1] + (count * stride, LANES)), nblk, count, stride


def _gather_rows(ref, lead, r):
    flat, nblk, count, stride = _strided_rows(ref, lead)
    return jnp.concatenate([flat[lead + (c, pl.ds(r, count, stride=stride), slice(None))]
                            for c in range(nblk)], axis=1)


def _scatter_rows(ref, lead, r, val):
    flat, nblk, count, stride = _strided_rows(ref, lead)
    for c in range(nblk):
        flat[lead + (c, pl.ds(r, count, stride=stride), slice(None))] = val[:, c * LANES:(c + 1) * LANES]


def _pack_pairs(y):
    return pltpu.bitcast(y.astype(BF16), PACKED)


def _unpack_pairs(w):
    return pltpu.bitcast(w, BF16)


def _tile_row(ref, lead, k):
    return jnp.concatenate([ref[lead + (c, k)] for c in range(ref.shape[len(lead)])], axis=1)


def _s1_body(a_ref, w_ref, o_ref):
    rb = a_ref.shape[3]
    for r in range(rb):
        z = jnp.concatenate([_gather_rows(a_ref, (m,), r) for m in range(2)], axis=0)
        y = _pack_pairs(_dot(w_ref[...], z.astype(BF16)))
        _scatter_rows(o_ref, (0,), r, y)


def _s1(src, which, w1m, P, N1):
    N1h = N1 // 2
    return pl.pallas_call(
        _s1_body,
        grid=(P, FFT_N2 // R_BLOCK),
        in_specs=[
            pl.BlockSpec((None, 2, None, CH_BLOCKS, N1h, R_BLOCK, LANES), lambda p, j: (which, 0, p, 0, 0, j, 0)),
            pl.BlockSpec((2 * N1, N1), lambda p, j: (0, 0)),
        ],
        out_specs=pl.BlockSpec((1, CH_BLOCKS, N1, R_BLOCK, LANES), lambda p, j: (p, 0, 0, j, 0)),
        out_shape=jax.ShapeDtypeStruct((P, CH_BLOCKS, N1, FFT_N2, LANES), PACKED),
        compiler_params=_cparams("parallel", "parallel"),
        name="dft_n1",
    )(src, w1m)


def _s1f_body(b_ref, s_ref, w_ref, o_ref):
    rb = b_ref.shape[2]
    for r in range(rb):
        z = (_gather_rows(b_ref, (), r) / s_ref[...]).astype(BF16)
        _scatter_rows(o_ref, (), r, _pack_pairs(_dot(w_ref[...], z)))


def _s1_filter(buf4, ssum, w1f, N1):
    W = FILT_BLOCKS * LANES
    return pl.pallas_call(
        _s1f_body,
        grid=(FFT_N2 // R_BLOCK,),
        in_specs=[
            pl.BlockSpec((FILT_BLOCKS, N1, R_BLOCK, LANES), lambda j: (0, 0, j, 0)),
            pl.BlockSpec((1, W), lambda j: (0, 0)),
            pl.BlockSpec((2 * N1, N1), lambda j: (0, 0)),
        ],
        out_specs=pl.BlockSpec((FILT_BLOCKS, N1, R_BLOCK, LANES), lambda j: (0, 0, j, 0)),
        out_shape=jax.ShapeDtypeStruct((FILT_BLOCKS, N1, FFT_N2, LANES), PACKED),
        compiler_params=_cparams("parallel"),
        name="dft_n1_filter",
    )(buf4, ssum, w1f)


def _midf_body(x_ref, g_ref, h_ref):
    for k in range(x_ref.shape[1]):
        z = _unpack_pairs(_tile_row(x_ref, (), k))
        h_ref[k] = _dot(g_ref[k], z).astype(h_ref.dtype)


def _mid_filter(hpre, g, N1):
    W = FILT_BLOCKS * LANES
    kb = K_BLOCK
    return pl.pallas_call(
        _midf_body,
        grid=(N1 // kb,),
        in_specs=[
            pl.BlockSpec((FILT_BLOCKS, kb, FFT_N2, LANES), lambda i: (0, i, 0, 0)),
            pl.BlockSpec((kb, 2 * FFT_N2, 2 * FFT_N2), lambda i: (i, 0, 0)),
        ],
        out_specs=pl.BlockSpec((kb, 2 * FFT_N2, W), lambda i: (i, 0, 0)),
        out_shape=jax.ShapeDtypeStruct((N1, 2 * FFT_N2, W), BF16),
        compiler_params=_cparams("parallel"),
        name="dft_n2_filter",
    )(hpre, g)


def _mid_body(x_ref, g_ref, gi_ref, h_ref, o_ref, d_ref, y_ref):
    kb = x_ref.shape[2]

    def forward(k):
        z = _unpack_pairs(_tile_row(x_ref, (0,), k))
        d_ref[k % 2] = _dot(g_ref[k], z)

    def product(k):
        hh = h_ref[k].astype(F32)
        dr, di = d_ref[k % 2, :FFT_N2], d_ref[k % 2, FFT_N2:]
        hr, hi = hh[:FFT_N2], hh[FFT_N2:]
        y_ref[k % 2, :FFT_N2] = (dr * hr - di * hi).astype(BF16)
        y_ref[k % 2, FFT_N2:] = (dr * hi + di * hr).astype(BF16)

    def inverse(k):
        e = _dot(gi_ref[k], y_ref[k % 2])
        ew = _pack_pairs(e)
        for c in range(o_ref.shape[1]):
            o_ref[0, c, k] = ew[:, c * LANES:(c + 1) * LANES]

    forward(0)
    for k in range(kb):
        if k + 1 < kb:
            forward(k + 1)
        if k > 0:
            inverse(k - 1)
        product(k)
    inverse(kb - 1)


def _mid(x5, g, gi, hspec, order, P, N1):
    kb = K_BLOCK
    blk = pl.BlockSpec((1, CH_BLOCKS, kb, FFT_N2, LANES), lambda i, p: (p, 0, i, 0, 0))
    return pl.pallas_call(
        _mid_body,
        grid=(N1 // kb, P),
        in_specs=[
            blk,
            pl.BlockSpec((kb, 2 * FFT_N2, 2 * FFT_N2), lambda i, p: (i, 0, 0)),
            pl.BlockSpec((kb, 2 * FFT_N2, 2 * FFT_N2), lambda i, p: (i, 0, 0)),
            pl.BlockSpec((kb, 2 * FFT_N2, HYENA_W), lambda i, p: (i, 0, order)),
        ],
        out_specs=blk,
        out_shape=jax.ShapeDtypeStruct((P, CH_BLOCKS, N1, FFT_N2, LANES), PACKED),
        scratch_shapes=[pltpu.VMEM((2, 2 * FFT_N2, HYENA_W), F32),
                        pltpu.VMEM((2, 2 * FFT_N2, HYENA_W), BF16)],
        compiler_params=_cparams("parallel", "parallel"),
        name="dft_n2_conv",
    )(x5, g, gi, hspec)


def _s1inv_body(e_ref, w_ref, src_ref, mul_ref, bias_ref, *rest, chain):
    if chain:
        w1_ref, o_ref, o1_ref, y_ref = rest
    else:
        o_ref, y_ref = rest
    n1h, rb = src_ref.shape[2:4]
    for r in range(rb):
        e = _unpack_pairs(_gather_rows(e_ref, (0,), r))
        y = _dot(w_ref[...], e)
        for ro in range(2):
            _scatter_rows(y_ref, (ro,), r, y[ro * n1h:(ro + 1) * n1h])
    zz = mul_ref[...] * (y_ref[...] + src_ref[...] * bias_ref[...][None, :, None])
    o_ref[...] = zz
    if chain:
        y_ref[...] = zz
        _s1_body(y_ref, w1_ref, o1_ref)


def _s1inv(e5, w1i, src, src_which, mul, mul_which, bias3, P, N1, w1m=None):
    N1h = N1 // 2
    chain = w1m is not None
    member = lambda which: pl.BlockSpec((None, 2, None, CH_BLOCKS, N1h, R_BLOCK, LANES),
                                        lambda p, j: (which, 0, p, 0, 0, j, 0))
    packed_blk = pl.BlockSpec((1, CH_BLOCKS, N1, R_BLOCK, LANES), lambda p, j: (p, 0, 0, j, 0))
    in_specs = [
        packed_blk,
        pl.BlockSpec((N1, 2 * N1), lambda p, j: (0, 0)),
        member(src_which),
        member(mul_which),
        pl.BlockSpec((CH_BLOCKS, 1, LANES), lambda p, j: (0, 0, 0)),
    ]
    out_specs = [member(0)]
    out_shape = [jax.ShapeDtypeStruct((1, 2, P, CH_BLOCKS, N1h, FFT_N2, LANES), F32)]
    args = [e5, w1i, src, mul, bias3]
    if chain:
        in_specs.append(pl.BlockSpec((2 * N1, N1), lambda p, j: (0, 0)))
        out_specs.append(packed_blk)
        out_shape.append(jax.ShapeDtypeStruct((P, CH_BLOCKS, N1, FFT_N2, LANES), PACKED))
        args.append(w1m)
    return pl.pallas_call(
        functools.partial(_s1inv_body, chain=chain),
        grid=(P, FFT_N2 // R_BLOCK),
        in_specs=in_specs,
        out_specs=out_specs,
        out_shape=out_shape,
        scratch_shapes=[pltpu.VMEM((2, CH_BLOCKS, N1h, R_BLOCK, LANES), F32)],
        compiler_params=_cparams("parallel", "parallel"),
        name="idft_n1_dft_n1" if chain else "idft_n1",
    )(*args)


def _outproj_body(a_ref, ga_ref, z_ref, gh_ref, x_ref, wa_ref, wh_ref, w_ref, fw_ref, o_ref, *, final):
    def norm_gate(y, w, g):
        ms = jnp.mean(y * y, axis=-1, keepdims=True)
        g = g.astype(F32)
        return (y * lax.rsqrt(ms + EPS) * w) * (g * (1.0 / (1.0 + jnp.exp(-g))))

    attn = a_ref[0].reshape(ATTN_W, a_ref.shape[-1]).T
    oa = norm_gate(attn, wa_ref[...], ga_ref[...]).astype(BF16)
    zz = jnp.concatenate([z_ref[0, c] for c in range(CH_BLOCKS)], axis=1)
    oh = norm_gate(zz, wh_ref[...], gh_ref[...]).astype(BF16)
    y = x_ref[...] + (_dot(oa, w_ref[0:ATTN_W, :]) + _dot(oh, w_ref[ATTN_W:MIX_W, :]))
    if final:
        ms = jnp.mean(y * y, axis=-1, keepdims=True)
        y = y * lax.rsqrt(ms + EPS) * fw_ref[...]
    o_ref[...] = y


def _outproj(attn_t, ga, zz4, gh, x2d, wa, wh, w_bf, fw, final):
    tm = ROW_TILE
    T = x2d.shape[0]
    nlb = zz4.shape[2] // tm
    row = lambda i: (i, 0)
    const = lambda i: (0, 0)
    body = functools.partial(_outproj_body, final=final)
    return pl.pallas_call(
        body,
        grid=(T // tm,),
        in_specs=[
            pl.BlockSpec((1, N_Q_HEADS, HEAD_DIM, tm), lambda i: (i // nlb, 0, 0, i % nlb)),
            pl.BlockSpec((tm, ATTN_W), row),
            pl.BlockSpec((1, CH_BLOCKS, tm, LANES), lambda i: (i // nlb, 0, i % nlb, 0)),
            pl.BlockSpec((tm, HYENA_W), row),
            pl.BlockSpec((tm, D_MODEL), row),
            pl.BlockSpec((1, ATTN_W), const),
            pl.BlockSpec((1, HYENA_W), const),
            pl.BlockSpec((MIX_W, D_MODEL), const),
            pl.BlockSpec((1, D_MODEL), const),
        ],
        out_specs=pl.BlockSpec((tm, D_MODEL), row),
        out_shape=jax.ShapeDtypeStruct((T, D_MODEL), F32),
        compiler_params=_cparams("parallel"),
        name="outproj",
    )(attn_t, ga, zz4, gh, x2d, wa, wh, w_bf, fw)


def _rope_tables(L):
    t = jnp.arange(L, dtype=jnp.int32)
    pos = jnp.stack([t // GRID_W, t % GRID_W], axis=-1).astype(F32)
    freqs = ROPE_THETA ** (-jnp.arange(ROPE_FREQS, dtype=F32) / ROPE_FREQS)
    ang = pos[:, :, None] * freqs
    cos, sin = jnp.cos(ang), jnp.sin(ang)
    c_head = jnp.stack([cos, cos], axis=2).reshape(L, HEAD_DIM)
    s_head = jnp.stack([-sin, sin], axis=2).reshape(L, HEAD_DIM)
    reps = LANES // HEAD_DIM
    return jnp.tile(c_head, (1, reps)), jnp.tile(s_head, (1, reps))


def _head_mean_matrix():
    idx = jnp.arange(LANES) // HEAD_DIM
    return jnp.where(idx[:, None] == idx[None, :], 1.0 / HEAD_DIM, 0.0).astype(BF16)


def _dft_tables(L):
    N = 2 * L
    N1 = N // FFT_N2
    N1h = N1 // 2
    two_pi = 2.0 * math.pi
    k1 = jnp.arange(N1, dtype=jnp.int32)
    th = ((k1[:, None] * k1[None, :]) % N1).astype(F32) * (two_pi / N1)
    c, s = jnp.cos(th), jnp.sin(th)
    ch, sh = c[:, :N1h], s[:, :N1h]
    w1 = jnp.stack([jnp.concatenate([ch, sh], axis=1),
                    jnp.concatenate([-sh, ch], axis=1)], axis=1).reshape(2 * N1, N1)
    w1f = jnp.stack([c, -s], axis=1).reshape(2 * N1, N1)
    ct, st = ch.T, sh.T
    top = jnp.stack([ct, -st], axis=2).reshape(N1h, 2 * N1)
    bot = jnp.stack([st, ct], axis=2).reshape(N1h, 2 * N1)
    w1i = jnp.concatenate([top, bot], axis=0) * (1.0 / N)
    k2 = jnp.arange(FFT_N2, dtype=jnp.int32)
    freq = k1[:, None, None] + N1 * k2[None, :, None]
    ph = ((freq * k2[None, None, :]) % N).astype(F32) * (two_pi / N)
    cp, sp = jnp.cos(ph), jnp.sin(ph)
    g = jnp.concatenate([jnp.stack([cp, sp], axis=3).reshape(N1, FFT_N2, 2 * FFT_N2),
                         jnp.stack([-sp, cp], axis=3).reshape(N1, FFT_N2, 2 * FFT_N2)], axis=1)
    cpt, spt = jnp.swapaxes(cp, 1, 2), jnp.swapaxes(sp, 1, 2)
    gi = jnp.stack([jnp.concatenate([cpt, -spt], axis=2),
                    jnp.concatenate([spt, cpt], axis=2)], axis=2).reshape(N1, 2 * FFT_N2, 2 * FFT_N2)
    return w1.astype(BF16), w1f.astype(BF16), w1i.astype(BF16), g.astype(BF16), gi.astype(BF16)


def _filter_positions(L):
    m = jnp.arange(2 * L, dtype=jnp.int32)
    pos = jnp.where(m < L, m, 2 * L - m)
    valid = (m != L).astype(F32)[:, None]
    pos = jnp.where(m == L, 0, pos).astype(F32)
    t = pos / (L - 1)
    w = 2.0 * math.pi * pos / L
    bands = jnp.linspace(1e-4, FILT_BANDS - 1, FILT_BANDS, dtype=F32)
    ang = w[:, None] * bands[None, :]
    z = jnp.concatenate([t[:, None], jnp.cos(ang), -jnp.sin(ang)], axis=-1)
    z = jnp.pad(z, ((0, 0), (0, FILT_HID - FILT_EMB)))
    hb = ROW_TILE // 2
    z = z.reshape(2 * L // ROW_TILE, 2, hb, FILT_HID)
    z = jnp.transpose(z, (0, 2, 1, 3)).reshape(L, 2 * FILT_HID)
    return z, valid


def _two_row_weights(w):
    zero = jnp.zeros_like(w)
    return jnp.concatenate([jnp.concatenate([w, zero], axis=1), jnp.concatenate([zero, w], axis=1)], axis=0)


def _pad_to(a, shape):
    return jnp.pad(a, [(0, s - d) for d, s in zip(a.shape, shape)])


def _trunk(x, norm_w, w_in, q_norm_w, k_norm_w, conv_w, conv_b, filt_w1, filt_b1,
           filt_w2, filt_b2, filt_w3, filt_freq, filt_decay, hyena_bias,
           attn_out_norm_w, hyena_out_norm_w, w_out, final_norm_w):
    B, L, _ = x.shape
    assert B % 2 == 0 and L % 1024 == 0
    P = B // 2
    N1 = 2 * L // FFT_N2
    N1h = N1 // 2
    T = B * L

    ctab, stab = _rope_tables(L)
    pmat = _head_mean_matrix()
    w1m, w1f, w1i, gmat, gimat = _dft_tables(L)
    ztab, valid = _filter_positions(L)
    reps = LANES // HEAD_DIM

    x2d = x.reshape(T, D_MODEL)
    for l in range(DEPTH):
        q, k, v, ga, uc, gh = _inproj(
            x2d, norm_w[l][None], w_in[l].astype(BF16),
            jnp.tile(q_norm_w[l], reps)[None], jnp.tile(k_norm_w[l], reps)[None],
            pmat, ctab, stab, conv_w[l], conv_b[l][None], B, L)
        attn = _attention(q, k, v, B, L)

        W = HYENA_ORDER * HYENA_W
        w3 = filt_w3[l].reshape(FILT_HID, HYENA_ORDER, 2, HYENA_W)
        w3 = jnp.transpose(w3, (2, 0, 1, 3)).reshape(2, FILT_HID, W)
        zero = jnp.zeros_like(w3)
        w3d = jnp.stack([jnp.concatenate([w3, zero], axis=1), jnp.concatenate([zero, w3], axis=1)], axis=1)
        dec = filt_decay[l].reshape(HYENA_ORDER, 2, HYENA_W)
        decd = jnp.transpose(dec, (1, 0, 2)).reshape(2, 1, W)
        hid = (FILT_HID, FILT_HID)
        buf, ssum = _filters(
            ztab, valid,
            _two_row_weights(_pad_to(filt_w1[l], hid)), jnp.tile(filt_b1[l], 2)[None],
            _two_row_weights(filt_w2[l]), jnp.tile(filt_b2[l], 2)[None],
            jnp.tile(filt_freq[l], 2)[None], w3d, decd, L)
        hpre = _s1_filter(buf.reshape(FILT_BLOCKS, N1, FFT_N2, LANES), ssum, w1f, N1)
        hspec = _mid_filter(hpre, gmat, N1)

        parts = uc.reshape(HYENA_ORDER + 1, 2, P, CH_BLOCKS, N1h, FFT_N2, LANES)
        zz = parts
        o1 = _s1(zz, 0, w1m, P, N1)
        for o in range(HYENA_ORDER):
            e = _mid(o1, gmat, gimat, hspec, o, P, N1)
            bias3 = hyena_bias[l, o].reshape(CH_BLOCKS, 1, LANES)
            if o + 1 < HYENA_ORDER:
                zz, o1 = _s1inv(e, w1i, zz, 0, parts, o + 1, bias3, P, N1, w1m=w1m)
            else:
                zz, = _s1inv(e, w1i, zz, 0, parts, o + 1, bias3, P, N1)
        zz4 = zz.reshape(B, CH_BLOCKS, L, LANES)

        x2d = _outproj(attn, ga, zz4, gh, x2d,
                       attn_out_norm_w[l][None], hyena_out_norm_w[l][None],
                       w_out[l].astype(BF16), final_norm_w[None], final=(l == DEPTH - 1))
    return x2d.reshape(B, L, D_MODEL)


def kernel(x_prompt, x_sample, norm_w, w_in, q_norm_w, k_norm_w, conv_w, conv_b, filt_w1, filt_b1, filt_w2, filt_b2, filt_w3, filt_freq, filt_decay, hyena_bias, attn_out_norm_w, hyena_out_norm_w, w_out, final_norm_w):
    weights = (norm_w, w_in, q_norm_w, k_norm_w, conv_w, conv_b, filt_w1, filt_b1, filt_w2, filt_b2,
               filt_w3, filt_freq, filt_decay, hyena_bias, attn_out_norm_w, hyena_out_norm_w, w_out,
               final_norm_w)
    return (_trunk(x_prompt, *weights), _trunk(x_sample, *weights))
```

```python
import functools
import math

import jax
import jax.numpy as jnp
from jax import lax
from jax.experimental import pallas as pl
from jax.experimental.pallas import tpu as pltpu

F32 = jnp.float32
BF16 = jnp.bfloat16
PACKED = jnp.uint32

D_MODEL = 1024
DEPTH = 4
GRID_W = 64
HEAD_DIM = 64
N_Q_HEADS = 8
N_KV_HEADS = 2
Q_PER_KV = N_Q_HEADS // N_KV_HEADS
ATTN_W = N_Q_HEADS * HEAD_DIM
KV_W = N_KV_HEADS * HEAD_DIM
HYENA_W = 512
HYENA_ORDER = 2
MIX_W = ATTN_W + HYENA_W
SHORT_CONV = 3
FILT_EMB = 33
FILT_BANDS = 16
FILT_HID = 64
N_FILT = HYENA_ORDER * 2 * HYENA_W
MOD_SHIFT = 0.05
ROPE_THETA = 10000.0
ROPE_FREQS = HEAD_DIM // 4
EPS = 1e-6
COL_K = ATTN_W
COL_V = COL_K + KV_W
COL_GA = COL_V + KV_W
COL_U = COL_GA + ATTN_W
COL_GH = COL_U + (HYENA_ORDER + 1) * HYENA_W
D_IN_PROJ = COL_GH + HYENA_W

ROW_TILE = 512
HALO_ROWS = 8
QK_SCALE = math.log2(math.e) / math.sqrt(HEAD_DIM)
KV_CHUNK = 512
Q_TILE = 512
Q_SUB = 256
V_ROWS = HEAD_DIM + 16
LANES = 128
FFT_N2 = 128
CH_BLOCKS = HYENA_W // LANES
FILT_BLOCKS = HYENA_ORDER * CH_BLOCKS
R_BLOCK = 8
K_BLOCK = 8
VMEM_LIMIT_BYTES = 48 * 1024 * 1024
HIGHEST = lax.Precision.HIGHEST


def _cparams(*sem):
    return pltpu.CompilerParams(dimension_semantics=sem, vmem_limit_bytes=VMEM_LIMIT_BYTES)


def _dot(a, b):
    return jnp.dot(a, b, preferred_element_type=F32)


def _dot_f32(a, b):
    return jnp.dot(a, b, preferred_element_type=F32, precision=HIGHEST)


def _inproj_body(x_ref, xprev_ref, xnext_ref, nw_ref, w_ref, qnw_ref, knw_ref, p_ref, c_ref, s_ref,
                 cw_ref, cb_ref, q_ref, k_ref, v_ref, ga_ref, parts_ref, gh_ref, *, nlb):
    def normed(x):
        ms = jnp.mean(x * x, axis=-1, keepdims=True)
        return (x * lax.rsqrt(ms + EPS) * nw_ref[...]).astype(BF16)

    h = normed(x_ref[...])
    p = p_ref[...]
    c = c_ref[...]
    s = s_ref[...]
    lane = lax.broadcasted_iota(jnp.int32, c.shape, 1)
    first_half = (lane % (2 * ROPE_FREQS)) < ROPE_FREQS

    def norm_rope(y, nw):
        y2 = y * y
        hi = y2.astype(BF16)
        lo = (y2 - hi.astype(F32)).astype(BF16)
        msq = _dot(hi, p) + _dot(lo, p)
        yn = y * lax.rsqrt(msq + EPS) * nw
        partner = jnp.where(first_half, pltpu.roll(yn, LANES - ROPE_FREQS, 1),
                            pltpu.roll(yn, ROPE_FREQS, 1))
        return yn * c + partner * s

    q = _dot(h, w_ref[:, 0:COL_K])
    qnw = qnw_ref[...]
    for j in range(ATTN_W // LANES):
        r = norm_rope(q[:, j * LANES:(j + 1) * LANES], qnw) * QK_SCALE
        rt = r.T
        q_ref[0, 2 * j] = rt[:HEAD_DIM].astype(BF16)
        q_ref[0, 2 * j + 1] = rt[HEAD_DIM:].astype(BF16)
    kv = _dot(h, w_ref[:, COL_K:COL_GA])
    kr = norm_rope(kv[:, :KV_W], knw_ref[...])
    k_ref[0, 0] = kr[:, :HEAD_DIM].astype(BF16)
    k_ref[0, 1] = pltpu.roll(kr, HEAD_DIM, 1)[:, :HEAD_DIM].astype(BF16)
    vt = kv[:, KV_W:].T
    extra = lax.broadcasted_iota(jnp.int32, (V_ROWS - HEAD_DIM, KV_CHUNK), 0)
    ones_row = jnp.where(extra == 0, 1.0, 0.0).astype(BF16)
    for c in range(vt.shape[1] // KV_CHUNK):
        for g in range(N_KV_HEADS):
            v_ref[0, g, c, 0:HEAD_DIM, :] = vt[g * HEAD_DIM:(g + 1) * HEAD_DIM,
                                               c * KV_CHUNK:(c + 1) * KV_CHUNK].astype(BF16)
            v_ref[0, g, c, HEAD_DIM:V_ROWS, :] = ones_row
    ga_ref[...] = _dot(h, w_ref[:, COL_GA:COL_U]).astype(ga_ref.dtype)
    gh_ref[...] = _dot(h, w_ref[:, COL_GH:D_IN_PROJ]).astype(gh_ref.dtype)

    u = _dot(h, w_ref[:, COL_U:COL_GH])
    halo = xprev_ref.shape[0]
    hh = normed(jnp.concatenate([xprev_ref[...], xnext_ref[...]], axis=0))
    uh = _dot(hh, w_ref[:, COL_U:COL_GH])
    li = pl.program_id(0) % nlb
    prev_row = jnp.where(li == 0, 0.0, uh[halo - 1:halo])
    next_row = jnp.where(li == nlb - 1, 0.0, uh[halo:halo + 1])
    tm = u.shape[0]
    row = lax.broadcasted_iota(jnp.int32, u.shape, 0)
    um = jnp.where(row == 0, prev_row, pltpu.roll(u, 1, 0))
    up = jnp.where(row == tm - 1, next_row, pltpu.roll(u, tm - 1, 0))
    cw = cw_ref[...]
    res = um * cw[0:1] + u * cw[1:2] + up * cw[2:3] + cb_ref[...]
    for part in range(HYENA_ORDER + 1):
        for cblk in range(CH_BLOCKS):
            lo = part * HYENA_W + cblk * LANES
            parts_ref[part, 0, cblk] = res[:, lo:lo + LANES]


def _inproj(x2d, nw, w_bf, qnw, knw, pmat, ctab, stab, cw, cb, B, L):
    tm, halo = ROW_TILE, HALO_ROWS
    assert tm % KV_CHUNK == 0 and L % tm == 0
    T = B * L
    nlb = L // tm
    rpb = tm // halo
    UW = (HYENA_ORDER + 1) * HYENA_W
    row = lambda i: (i, 0)
    const = lambda i: (0, 0)
    pos = lambda i: (i % nlb, 0)
    heads = lambda i: (i // nlb, 0, i % nlb, 0)
    return pl.pallas_call(
        functools.partial(_inproj_body, nlb=nlb),
        grid=(T // tm,),
        in_specs=[
            pl.BlockSpec((tm, D_MODEL), row),
            pl.BlockSpec((halo, D_MODEL), lambda i: (jnp.maximum(i * rpb - 1, 0), 0)),
            pl.BlockSpec((halo, D_MODEL), lambda i: (jnp.minimum((i + 1) * rpb, T // halo - 1), 0)),
            pl.BlockSpec((1, D_MODEL), const),
            pl.BlockSpec((D_MODEL, D_IN_PROJ), const),
            pl.BlockSpec((1, LANES), const),
            pl.BlockSpec((1, LANES), const),
            pl.BlockSpec((LANES, LANES), const),
            pl.BlockSpec((tm, LANES), pos),
            pl.BlockSpec((tm, LANES), pos),
            pl.BlockSpec((SHORT_CONV, UW), const),
            pl.BlockSpec((1, UW), const),
        ],
        out_specs=[
            pl.BlockSpec((1, N_Q_HEADS, HEAD_DIM, tm), lambda i: (i // nlb, 0, 0, i % nlb)),
            pl.BlockSpec((1, N_KV_HEADS, tm, HEAD_DIM), heads),
            pl.BlockSpec((1, N_KV_HEADS, tm // KV_CHUNK, V_ROWS, KV_CHUNK),
                         lambda i: (i // nlb, 0, i % nlb, 0, 0)),
            pl.BlockSpec((tm, ATTN_W), row),
            pl.BlockSpec((HYENA_ORDER + 1, 1, CH_BLOCKS, tm, LANES), lambda i: (0, i // nlb, 0, i % nlb, 0)),
            pl.BlockSpec((tm, HYENA_W), row),
        ],
        out_shape=[
            jax.ShapeDtypeStruct((B, N_Q_HEADS, HEAD_DIM, L), BF16),
            jax.ShapeDtypeStruct((B, N_KV_HEADS, L, HEAD_DIM), BF16),
            jax.ShapeDtypeStruct((B, N_KV_HEADS, L // KV_CHUNK, V_ROWS, KV_CHUNK), BF16),
            jax.ShapeDtypeStruct((T, ATTN_W), BF16),
            jax.ShapeDtypeStruct((HYENA_ORDER + 1, B, CH_BLOCKS, L, LANES), F32),
            jax.ShapeDtypeStruct((T, HYENA_W), BF16),
        ],
        compiler_params=_cparams("parallel"),
        name="inproj",
    )(x2d, x2d, x2d, nw, w_bf, qnw, knw, pmat, ctab, stab, cw, cb)


def _attn_body(qt_ref, k_ref, vt_ref, o_ref, s_ref, p_ref, al_ref, cm_ref, m_ref, acc_ref, *, tq, nk):
    cols = Q_PER_KV * tq
    ncb = cols // LANES
    m_ref[...] = jnp.full(m_ref.shape, -jnp.inf, F32)
    acc_ref[...] = jnp.zeros(acc_ref.shape, F32)

    def scores(j, slot):
        start = pl.multiple_of(j * KV_CHUNK, KV_CHUNK)
        k = k_ref[0, 0, pl.ds(start, KV_CHUNK), :]
        for h in range(Q_PER_KV):
            for q0 in range(0, tq, Q_SUB):
                c0 = h * tq + q0
                r = _dot(k, qt_ref[0, h, :, q0:q0 + Q_SUB])
                for i in range(Q_SUB // LANES):
                    s_ref[slot, c0 // LANES + i] = r[:, i * LANES:(i + 1) * LANES]
                cm_ref[slot, :, c0:c0 + Q_SUB] = jnp.max(r, axis=0, keepdims=True)

    def accumulate(j, slot):
        vt = vt_ref[0, 0, j]
        for i in range(cols // Q_SUB):
            cs = slice(i * Q_SUB, (i + 1) * Q_SUB)
            nb = Q_SUB // LANES
            p = jnp.concatenate([p_ref[slot, nb * i + t] for t in range(nb)], axis=1)
            acc_ref[:, cs] = al_ref[slot, :, cs] * acc_ref[:, cs] + _dot(vt, p)

    def softmax(slot):
        m_old = m_ref[...]
        m_new = jnp.maximum(m_old, cm_ref[slot])
        m_ref[...] = m_new
        al_ref[slot] = jnp.exp2(m_old - m_new)
        for c in range(ncb):
            p_ref[slot, c] = jnp.exp2(s_ref[slot, c] - m_new[:, c * LANES:(c + 1) * LANES]).astype(BF16)

    scores(0, 0)
    scores(1, 1)
    softmax(0)
    scores(2, 0)
    accumulate(0, 0)
    softmax(1)

    def step(jj, carry):
        j = 2 * jj
        scores(j + 1, 1)
        accumulate(j - 1, 1)
        softmax(0)
        scores(jnp.minimum(j + 2, nk - 1), 0)
        accumulate(j, 0)
        softmax(1)
        return carry

    lax.fori_loop(1, nk // 2, step, 0)
    accumulate(nk - 1, 1)
    o = acc_ref[0:HEAD_DIM, :] / acc_ref[HEAD_DIM:HEAD_DIM + 1, :]
    for h in range(Q_PER_KV):
        o_ref[0, h] = o[:, h * tq:(h + 1) * tq].astype(o_ref.dtype)


def _attention(qt, k, vt, B, L):
    tq = Q_TILE
    nk = L // KV_CHUNK
    assert nk % 2 == 0 and nk >= 4
    cols = Q_PER_KV * tq
    ncb = cols // LANES
    body = functools.partial(_attn_body, tq=tq, nk=nk)
    return pl.pallas_call(
        body,
        grid=(B, N_KV_HEADS, L // tq),
        in_specs=[
            pl.BlockSpec((1, Q_PER_KV, HEAD_DIM, tq), lambda b, g, i: (b, g, 0, i)),
            pl.BlockSpec((1, 1, L, HEAD_DIM), lambda b, g, i: (b, g, 0, 0)),
            pl.BlockSpec((1, 1, nk, V_ROWS, KV_CHUNK), lambda b, g, i: (b, g, 0, 0, 0)),
        ],
        out_specs=pl.BlockSpec((1, Q_PER_KV, HEAD_DIM, tq), lambda b, g, i: (b, g, 0, i)),
        out_shape=jax.ShapeDtypeStruct((B, N_Q_HEADS, HEAD_DIM, L), BF16),
        scratch_shapes=[
            pltpu.VMEM((2, ncb, KV_CHUNK, LANES), F32),
            pltpu.VMEM((2, ncb, KV_CHUNK, LANES), BF16),
            pltpu.VMEM((2, 1, cols), F32),
            pltpu.VMEM((2, 1, cols), F32),
            pltpu.VMEM((1, cols), F32),
            pltpu.VMEM((V_ROWS, cols), F32),
        ],
        compiler_params=_cparams("parallel", "parallel", "parallel"),
        name="attention",
    )(qt, k, vt)


def _filt_body(z_ref, valid_ref, w1_ref, b1_ref, w2_ref, b2_ref, fr_ref, w3_ref, dec_ref,
               buf_ref, sum_ref):
    i = pl.program_id(0)
    z = z_ref[...]
    fr = fr_ref[...]
    h = jnp.sin(fr * (_dot_f32(z, w1_ref[...]) + b1_ref[...]))
    h = jnp.sin(fr * (_dot_f32(h, w2_ref[...]) + b2_ref[...]))
    h = jnp.concatenate([_dot_f32(h, w3_ref[0, 0]), _dot_f32(h, w3_ref[0, 1])], axis=0)
    t = jnp.concatenate([z[:, 0:1], z[:, FILT_HID:FILT_HID + 1]], axis=0)
    win = jnp.exp(-t * jnp.abs(dec_ref[0])) + MOD_SHIFT
    out = h * win * valid_ref[...]
    for c in range(FILT_BLOCKS):
        buf_ref[c] = out[:, c * LANES:(c + 1) * LANES]

    @pl.when(i == 0)
    def _():
        sum_ref[...] = jnp.zeros_like(sum_ref)

    sum_ref[...] += jnp.sum(jnp.abs(out), axis=0, keepdims=True)


def _filters(ztab, valid, w1, b1, w2, b2, fr, w3d, decd, L):
    rb = ROW_TILE
    n = 2 * L
    nb = n // rb
    half = nb // 2
    W = HYENA_ORDER * HYENA_W
    const = lambda i: (0, 0)
    return pl.pallas_call(
        _filt_body,
        grid=(nb,),
        in_specs=[
            pl.BlockSpec((rb // 2, LANES), lambda i: (i, 0)),
            pl.BlockSpec((rb, 1), lambda i: (i, 0)),
            pl.BlockSpec((LANES, LANES), const),
            pl.BlockSpec((1, LANES), const),
            pl.BlockSpec((LANES, LANES), const),
            pl.BlockSpec((1, LANES), const),
            pl.BlockSpec((1, LANES), const),
            pl.BlockSpec((1, 2, LANES, W), lambda i: (i // half, 0, 0, 0)),
            pl.BlockSpec((1, 1, W), lambda i: (i // half, 0, 0)),
        ],
        out_specs=[
            pl.BlockSpec((FILT_BLOCKS, rb, LANES), lambda i: (0, i, 0)),
            pl.BlockSpec((1, W), const),
        ],
        out_shape=[
            jax.ShapeDtypeStruct((FILT_BLOCKS, n, LANES), F32),
            jax.ShapeDtypeStruct((1, W), F32),
        ],
        compiler_params=_cparams("arbitrary"),
        name="filters",
    )(ztab, valid, w1, b1, w2, b2, fr, w3d, decd)


def _strided_rows(ref, lead):
    n = len(lead)
    nblk, count, stride = ref.shape[n:n + 3]
    return ref.reshape(ref.shape[:n + 1] + (count * stride, LANES)), nblk, count, stride


def _gather_rows(ref, lead, r):
    flat, nblk, count, stride = _strided_rows(ref, lead)
    return jnp.concatenate([flat[lead + (c, pl.ds(r, count, stride=stride), slice(None))]
                            for c in range(nblk)], axis=1)


def _scatter_rows(ref, lead, r, val):
    flat, nblk, count, stride = _strided_rows(ref, lead)
    for c in range(nblk):
        flat[lead + (c, pl.ds(r, count, stride=stride), slice(None))] = val[:, c * LANES:(c + 1) * LANES]


def _pack_pairs(y):
    return pltpu.bitcast(y.astype(BF16), PACKED)


def _unpack_pairs(w):
    return pltpu.bitcast(w, BF16)


def _tile_row(ref, lead, k):
    return jnp.concatenate([ref[lead + (c, k)] for c in range(ref.shape[len(lead)])], axis=1)


def _s1_body(a_ref, w_ref, o_ref):
    rb = a_ref.shape[3]
    for r in range(rb):
        z = jnp.concatenate([_gather_rows(a_ref, (m,), r) for m in range(2)], axis=0)
        y = _pack_pairs(_dot(w_ref[...], z.astype(BF16)))
        _scatter_rows(o_ref, (0,), r, y)


def _s1(src, which, w1m, P, N1):
    N1h = N1 // 2
    return pl.pallas_call(
        _s1_body,
        grid=(P, FFT_N2 // R_BLOCK),
        in_specs=[
            pl.BlockSpec((None, 2, None, CH_BLOCKS, N1h, R_BLOCK, LANES), lambda p, j: (which, 0, p, 0, 0, j, 0)),
            pl.BlockSpec((2 * N1, N1), lambda p, j: (0, 0)),
        ],
        out_specs=pl.BlockSpec((1, CH_BLOCKS, N1, R_BLOCK, LANES), lambda p, j: (p, 0, 0, j, 0)),
        out_shape=jax.ShapeDtypeStruct((P, CH_BLOCKS, N1, FFT_N2, LANES), PACKED),
        compiler_params=_cparams("parallel", "parallel"),
        name="dft_n1",
    )(src, w1m)


def _s1f_body(b_ref, s_ref, w_ref, o_ref):
    rb = b_ref.shape[2]
    for r in range(rb):
        z = (_gather_rows(b_ref, (), r) / s_ref[...]).astype(BF16)
        _scatter_rows(o_ref, (), r, _pack_pairs(_dot(w_ref[...], z)))


def _s1_filter(buf4, ssum, w1f, N1):
    W = FILT_BLOCKS * LANES
    return pl.pallas_call(
        _s1f_body,
        grid=(FFT_N2 // R_BLOCK,),
        in_specs=[
            pl.BlockSpec((FILT_BLOCKS, N1, R_BLOCK, LANES), lambda j: (0, 0, j, 0)),
            pl.BlockSpec((1, W), lambda j: (0, 0)),
            pl.BlockSpec((2 * N1, N1), lambda j: (0, 0)),
        ],
        out_specs=pl.BlockSpec((FILT_BLOCKS, N1, R_BLOCK, LANES), lambda j: (0, 0, j, 0)),
        out_shape=jax.ShapeDtypeStruct((FILT_BLOCKS, N1, FFT_N2, LANES), PACKED),
        compiler_params=_cparams("parallel"),
        name="dft_n1_filter",
    )(buf4, ssum, w1f)


def _midf_body(x_ref, g_ref, h_ref):
    for k in range(x_ref.shape[1]):
        z = _unpack_pairs(_tile_row(x_ref, (), k))
        h_ref[k] = _dot(g_ref[k], z).astype(h_ref.dtype)


def _mid_filter(hpre, g, N1):
    W = FILT_BLOCKS * LANES
    kb = K_BLOCK
    return pl.pallas_call(
        _midf_body,
        grid=(N1 // kb,),
        in_specs=[
            pl.BlockSpec((FILT_BLOCKS, kb, FFT_N2, LANES), lambda i: (0, i, 0, 0)),
            pl.BlockSpec((kb, 2 * FFT_N2, 2 * FFT_N2), lambda i: (i, 0, 0)),
        ],
        out_specs=pl.BlockSpec((kb, 2 * FFT_N2, W), lambda i: (i, 0, 0)),
        out_shape=jax.ShapeDtypeStruct((N1, 2 * FFT_N2, W), BF16),
        compiler_params=_cparams("parallel"),
        name="dft_n2_filter",
    )(hpre, g)


def _mid_body(x_ref, g_ref, gi_ref, h_ref, o_ref, d_ref, y_ref):
    kb = x_ref.shape[2]

    def forward(k):
        z = _unpack_pairs(_tile_row(x_ref, (0,), k))
        d_ref[k % 2] = _dot(g_ref[k], z)

    def product(k):
        hh = h_ref[k].astype(F32)
        dr, di = d_ref[k % 2, :FFT_N2], d_ref[k % 2, FFT_N2:]
        hr, hi = hh[:FFT_N2], hh[FFT_N2:]
        y_ref[k % 2, :FFT_N2] = (dr * hr - di * hi).astype(BF16)
        y_ref[k % 2, FFT_N2:] = (dr * hi + di * hr).astype(BF16)

    def inverse(k):
        e = _dot(gi_ref[k], y_ref[k % 2])
        ew = _pack_pairs(e)
        for c in range(o_ref.shape[1]):
            o_ref[0, c, k] = ew[:, c * LANES:(c + 1) * LANES]

    forward(0)
    for k in range(kb):
        if k + 1 < kb:
            forward(k + 1)
        if k > 0:
            inverse(k - 1)
        product(k)
    inverse(kb - 1)


def _mid(x5, g, gi, hspec, order, P, N1):
    kb = K_BLOCK
    blk = pl.BlockSpec((1, CH_BLOCKS, kb, FFT_N2, LANES), lambda i, p: (p, 0, i, 0, 0))
    return pl.pallas_call(
        _mid_body,
        grid=(N1 // kb, P),
        in_specs=[
            blk,
            pl.BlockSpec((kb, 2 * FFT_N2, 2 * FFT_N2), lambda i, p: (i, 0, 0)),
            pl.BlockSpec((kb, 2 * FFT_N2, 2 * FFT_N2), lambda i, p: (i, 0, 0)),
            pl.BlockSpec((kb, 2 * FFT_N2, HYENA_W), lambda i, p: (i, 0, order)),
        ],
        out_specs=blk,
        out_shape=jax.ShapeDtypeStruct((P, CH_BLOCKS, N1, FFT_N2, LANES), PACKED),
        scratch_shapes=[pltpu.VMEM((2, 2 * FFT_N2, HYENA_W), F32),
                        pltpu.VMEM((2, 2 * FFT_N2, HYENA_W), BF16)],
        compiler_params=_cparams("parallel", "parallel"),
        name="dft_n2_conv",
    )(x5, g, gi, hspec)


def _s1inv_body(e_ref, w_ref, src_ref, mul_ref, bias_ref, *rest, chain):
    if chain:
        w1_ref, o_ref, o1_ref, y_ref = rest
    else:
        o_ref, y_ref = rest
    n1h, rb = src_ref.shape[2:4]
    for r in range(rb):
        e = _unpack_pairs(_gather_rows(e_ref, (0,), r))
        y = _dot(w_ref[...], e)
        for ro in range(2):
            _scatter_rows(y_ref, (ro,), r, y[ro * n1h:(ro + 1) * n1h])
    zz = mul_ref[...] * (y_ref[...] + src_ref[...] * bias_ref[...][None, :, None])
    o_ref[...] = zz
    if chain:
        y_ref[...] = zz
        _s1_body(y_ref, w1_ref, o1_ref)


def _s1inv(e5, w1i, src, src_which, mul, mul_which, bias3, P, N1, w1m=None):
    N1h = N1 // 2
    chain = w1m is not None
    member = lambda which: pl.BlockSpec((None, 2, None, CH_BLOCKS, N1h, R_BLOCK, LANES),
                                        lambda p, j: (which, 0, p, 0, 0, j, 0))
    packed_blk = pl.BlockSpec((1, CH_BLOCKS, N1, R_BLOCK, LANES), lambda p, j: (p, 0, 0, j, 0))
    in_specs = [
        packed_blk,
        pl.BlockSpec((N1, 2 * N1), lambda p, j: (0, 0)),
        member(src_which),
        member(mul_which),
        pl.BlockSpec((CH_BLOCKS, 1, LANES), lambda p, j: (0, 0, 0)),
    ]
    out_specs = [member(0)]
    out_shape = [jax.ShapeDtypeStruct((1, 2, P, CH_BLOCKS, N1h, FFT_N2, LANES), F32)]
    args = [e5, w1i, src, mul, bias3]
    if chain:
        in_specs.append(pl.BlockSpec((2 * N1, N1), lambda p, j: (0, 0)))
        out_specs.append(packed_blk)
        out_shape.append(jax.ShapeDtypeStruct((P, CH_BLOCKS, N1, FFT_N2, LANES), PACKED))
        args.append(w1m)
    return pl.pallas_call(
        functools.partial(_s1inv_body, chain=chain),
        grid=(P, FFT_N2 // R_BLOCK),
        in_specs=in_specs,
        out_specs=out_specs,
        out_shape=out_shape,
        scratch_shapes=[pltpu.VMEM((2, CH_BLOCKS, N1h, R_BLOCK, LANES), F32)],
        compiler_params=_cparams("parallel", "parallel"),
        name="idft_n1_dft_n1" if chain else "idft_n1",
    )(*args)


def _outproj_body(a_ref, ga_ref, z_ref, gh_ref, x_ref, wa_ref, wh_ref, w_ref, fw_ref, o_ref, *, final):
    def norm_gate(y, w, g):
        ms = jnp.mean(y * y, axis=-1, keepdims=True)
        g = g.astype(F32)
        return (y * lax.rsqrt(ms + EPS) * w) * (g * (1.0 / (1.0 + jnp.exp(-g))))

    attn = a_ref[0].astype(F32).reshape(ATTN_W, a_ref.shape[-1]).T
    oa = norm_gate(attn, wa_ref[...], ga_ref[...]).astype(BF16)
    zz = jnp.concatenate([z_ref[0, c] for c in range(CH_BLOCKS)], axis=1)
    oh = norm_gate(zz, wh_ref[...], gh_ref[...]).astype(BF16)
    y = x_ref[...] + (_dot(oa, w_ref[0:ATTN_W, :]) + _dot(oh, w_ref[ATTN_W:MIX_W, :]))
    if final:
        ms = jnp.mean(y * y, axis=-1, keepdims=True)
        y = y * lax.rsqrt(ms + EPS) * fw_ref[...]
    o_ref[...] = y


def _outproj(attn_t, ga, zz4, gh, x2d, wa, wh, w_bf, fw, final):
    tm = ROW_TILE
    T = x2d.shape[0]
    nlb = zz4.shape[2] // tm
    row = lambda i: (i, 0)
    const = lambda i: (0, 0)
    body = functools.partial(_outproj_body, final=final)
    return pl.pallas_call(
        body,
        grid=(T // tm,),
        in_specs=[
            pl.BlockSpec((1, N_Q_HEADS, HEAD_DIM, tm), lambda i: (i // nlb, 0, 0, i % nlb)),
            pl.BlockSpec((tm, ATTN_W), row),
            pl.BlockSpec((1, CH_BLOCKS, tm, LANES), lambda i: (i // nlb, 0, i % nlb, 0)),
            pl.BlockSpec((tm, HYENA_W), row),
            pl.BlockSpec((tm, D_MODEL), row),
            pl.BlockSpec((1, ATTN_W), const),
            pl.BlockSpec((1, HYENA_W), const),
            pl.BlockSpec((MIX_W, D_MODEL), const),
            pl.BlockSpec((1, D_MODEL), const),
        ],
        out_specs=pl.BlockSpec((tm, D_MODEL), row),
        out_shape=jax.ShapeDtypeStruct((T, D_MODEL), F32),
        compiler_params=_cparams("parallel"),
        name="outproj",
    )(attn_t, ga, zz4, gh, x2d, wa, wh, w_bf, fw)


def _rope_tables(L):
    t = jnp.arange(L, dtype=jnp.int32)
    pos = jnp.stack([t // GRID_W, t % GRID_W], axis=-1).astype(F32)
    freqs = ROPE_THETA ** (-jnp.arange(ROPE_FREQS, dtype=F32) / ROPE_FREQS)
    ang = pos[:, :, None] * freqs
    cos, sin = jnp.cos(ang), jnp.sin(ang)
    c_head = jnp.stack([cos, cos], axis=2).reshape(L, HEAD_DIM)
    s_head = jnp.stack([-sin, sin], axis=2).reshape(L, HEAD_DIM)
    reps = LANES // HEAD_DIM
    return jnp.tile(c_head, (1, reps)), jnp.tile(s_head, (1, reps))


def _head_mean_matrix():
    idx = jnp.arange(LANES) // HEAD_DIM
    return jnp.where(idx[:, None] == idx[None, :], 1.0 / HEAD_DIM, 0.0).astype(BF16)


def _dft_tables(L):
    N = 2 * L
    N1 = N // FFT_N2
    N1h = N1 // 2
    two_pi = 2.0 * math.pi
    k1 = jnp.arange(N1, dtype=jnp.int32)
    th = ((k1[:, None] * k1[None, :]) % N1).astype(F32) * (two_pi / N1)
    c, s = jnp.cos(th), jnp.sin(th)
    ch, sh = c[:, :N1h], s[:, :N1h]
    w1 = jnp.stack([jnp.concatenate([ch, sh], axis=1),
                    jnp.concatenate([-sh, ch], axis=1)], axis=1).reshape(2 * N1, N1)
    w1f = jnp.stack([c, -s], axis=1).reshape(2 * N1, N1)
    ct, st = ch.T, sh.T
    top = jnp.stack([ct, -st], axis=2).reshape(N1h, 2 * N1)
    bot = jnp.stack([st, ct], axis=2).reshape(N1h, 2 * N1)
    w1i = jnp.concatenate([top, bot], axis=0) * (1.0 / N)
    k2 = jnp.arange(FFT_N2, dtype=jnp.int32)
    freq = k1[:, None, None] + N1 * k2[None, :, None]
    ph = ((freq * k2[None, None, :]) % N).astype(F32) * (two_pi / N)
    cp, sp = jnp.cos(ph), jnp.sin(ph)
    g = jnp.concatenate([jnp.stack([cp, sp], axis=3).reshape(N1, FFT_N2, 2 * FFT_N2),
                         jnp.stack([-sp, cp], axis=3).reshape(N1, FFT_N2, 2 * FFT_N2)], axis=1)
    cpt, spt = jnp.swapaxes(cp, 1, 2), jnp.swapaxes(sp, 1, 2)
    gi = jnp.stack([jnp.concatenate([cpt, -spt], axis=2),
                    jnp.concatenate([spt, cpt], axis=2)], axis=2).reshape(N1, 2 * FFT_N2, 2 * FFT_N2)
    return w1.astype(BF16), w1f.astype(BF16), w1i.astype(BF16), g.astype(BF16), gi.astype(BF16)


def _filter_positions(L):
    m = jnp.arange(2 * L, dtype=jnp.int32)
    pos = jnp.where(m < L, m, 2 * L - m)
    valid = (m != L).astype(F32)[:, None]
    pos = jnp.where(m == L, 0, pos).astype(F32)
    t = pos / (L - 1)
    w = 2.0 * math.pi * pos / L
    bands = jnp.linspace(1e-4, FILT_BANDS - 1, FILT_BANDS, dtype=F32)
    ang = w[:, None] * bands[None, :]
    z = jnp.concatenate([t[:, None], jnp.cos(ang), -jnp.sin(ang)], axis=-1)
    z = jnp.pad(z, ((0, 0), (0, FILT_HID - FILT_EMB)))
    hb = ROW_TILE // 2
    z = z.reshape(2 * L // ROW_TILE, 2, hb, FILT_HID)
    z = jnp.transpose(z, (0, 2, 1, 3)).reshape(L, 2 * FILT_HID)
    return z, valid


def _two_row_weights(w):
    zero = jnp.zeros_like(w)
    return jnp.concatenate([jnp.concatenate([w, zero], axis=1), jnp.concatenate([zero, w], axis=1)], axis=0)


def _pad_to(a, shape):
    return jnp.pad(a, [(0, s - d) for d, s in zip(a.shape, shape)])


def _trunk(x, norm_w, w_in, q_norm_w, k_norm_w, conv_w, conv_b, filt_w1, filt_b1,
           filt_w2, filt_b2, filt_w3, filt_freq, filt_decay, hyena_bias,
           attn_out_norm_w, hyena_out_norm_w, w_out, final_norm_w):
    B, L, _ = x.shape
    assert B % 2 == 0 and L % 1024 == 0
    P = B // 2
    N1 = 2 * L // FFT_N2
    N1h = N1 // 2
    T = B * L

    ctab, stab = _rope_tables(L)
    pmat = _head_mean_matrix()
    w1m, w1f, w1i, gmat, gimat = _dft_tables(L)
    ztab, valid = _filter_positions(L)
    reps = LANES // HEAD_DIM

    x2d = x.reshape(T, D_MODEL)
    for l in range(DEPTH):
        q, k, v, ga, uc, gh = _inproj(
            x2d, norm_w[l][None], w_in[l].astype(BF16),
            jnp.tile(q_norm_w[l], reps)[None], jnp.tile(k_norm_w[l], reps)[None],
            pmat, ctab, stab, conv_w[l], conv_b[l][None], B, L)
        attn = _attention(q, k, v, B, L)

        W = HYENA_ORDER * HYENA_W
        w3 = filt_w3[l].reshape(FILT_HID, HYENA_ORDER, 2, HYENA_W)
        w3 = jnp.transpose(w3, (2, 0, 1, 3)).reshape(2, FILT_HID, W)
        zero = jnp.zeros_like(w3)
        w3d = jnp.stack([jnp.concatenate([w3, zero], axis=1), jnp.concatenate([zero, w3], axis=1)], axis=1)
        dec = filt_decay[l].reshape(HYENA_ORDER, 2, HYENA_W)
        decd = jnp.transpose(dec, (1, 0, 2)).reshape(2, 1, W)
        hid = (FILT_HID, FILT_HID)
        buf, ssum = _filters(
            ztab, valid,
            _two_row_weights(_pad_to(filt_w1[l], hid)), jnp.tile(filt_b1[l], 2)[None],
            _two_row_weights(filt_w2[l]), jnp.tile(filt_b2[l], 2)[None],
            jnp.tile(filt_freq[l], 2)[None], w3d, decd, L)
        hpre = _s1_filter(buf.reshape(FILT_BLOCKS, N1, FFT_N2, LANES), ssum, w1f, N1)
        hspec = _mid_filter(hpre, gmat, N1)

        parts = uc.reshape(HYENA_ORDER + 1, 2, P, CH_BLOCKS, N1h, FFT_N2, LANES)
        zz = parts
        o1 = _s1(zz, 0, w1m, P, N1)
        for o in range(HYENA_ORDER):
            e = _mid(o1, gmat, gimat, hspec, o, P, N1)
            bias3 = hyena_bias[l, o].reshape(CH_BLOCKS, 1, LANES)
            if o + 1 < HYENA_ORDER:
                zz, o1 = _s1inv(e, w1i, zz, 0, parts, o + 1, bias3, P, N1, w1m=w1m)
            else:
                zz, = _s1inv(e, w1i, zz, 0, parts, o + 1, bias3, P, N1)
        zz4 = zz.reshape(B, CH_BLOCKS, L, LANES)

        x2d = _outproj(attn, ga, zz4, gh, x2d,
                       attn_out_norm_w[l][None], hyena_out_norm_w[l][None],
                       w_out[l].astype(BF16), final_norm_w[None], final=(l == DEPTH - 1))
    return x2d.reshape(B, L, D_MODEL)


def kernel(x_prompt, x_sample, norm_w, w_in, q_norm_w, k_norm_w, conv_w, conv_b, filt_w1, filt_b1, filt_w2, filt_b2, filt_w3, filt_freq, filt_decay, hyena_bias, attn_out_norm_w, hyena_out_norm_w, w_out, final_norm_w):
    weights = (norm_w, w_in, q_norm_w, k_norm_w, conv_w, conv_b, filt_w1, filt_b1, filt_w2, filt_b2,
               filt_w3, filt_freq, filt_decay, hyena_bias, attn_out_norm_w, hyena_out_norm_w, w_out,
               final_norm_w)
    return (_trunk(x_prompt, *weights), _trunk(x_sample, *weights))
```
